```python
import math
import jax, jax.numpy as jnp
from jax import lax
import numpy as np

D_MODEL = 4096
BATCH = 2
SEQ = 8192
DEPTH = 4

HEAD_DIM = 64
RWKV_WIDTH = (3 * D_MODEL) // 8
ATTN_WIDTH = (3 * D_MODEL) // 8
CONV_WIDTH = D_MODEL - RWKV_WIDTH - ATTN_WIDTH
RWKV_HEADS = RWKV_WIDTH // HEAD_DIM
ATTN_HEADS = ATTN_WIDTH // HEAD_DIM

DECAY_LORA = max(32, int(round(1.8 * RWKV_WIDTH ** 0.5 / 32)) * 32)
AAA_LORA = max(32, int(round(1.8 * RWKV_WIDTH ** 0.5 / 32)) * 32)
GATE_LORA = max(32, int(round(0.6 * RWKV_WIDTH ** 0.8 / 32)) * 32)
RWKV_LN_EPS = 64e-5

DILATED_PATTERNS = ((128, 1), (512, 4), (2048, 16))
HEADS_PER_PATTERN = ATTN_HEADS // len(DILATED_PATTERNS)
ATTN_BLOCK = 128
REL_BUCKETS = 32
REL_MAX_DIST = 2048
NEG_INF = -1e30

CONV_TAPS = 31

N_GROUPS = 8
EXPERTS_PER_GROUP = 8
N_EXPERTS = N_GROUPS * EXPERTS_PER_GROUP
FINE_TOP_K = 2
EXPERT_FF = 128
MOE_BLOCK = 128

PLE_DIM = 256

RMS_EPS = 1e-6
LN_EPS = 1e-5

RWKV_PROJ = 3 * RWKV_WIDTH + DECAY_LORA + AAA_LORA + GATE_LORA
ATTN_PROJ = 3 * ATTN_WIDTH
CONV_PROJ = 2 * CONV_WIDTH
IN_PROJ = RWKV_PROJ + ATTN_PROJ + CONV_PROJ
RWKV_SPLITS = (RWKV_WIDTH, 2 * RWKV_WIDTH, 3 * RWKV_WIDTH,
               3 * RWKV_WIDTH + DECAY_LORA, 3 * RWKV_WIDTH + DECAY_LORA + AAA_LORA)

kernel_name = 'hymba_style_rwkv7_dilated_conformer_hmoe'


def rms_norm(x, g, eps=RMS_EPS):
    xf = x.astype(jnp.float32)
    y = xf * lax.rsqrt(jnp.mean(xf * xf, axis=-1, keepdims=True) + eps)
    return (y * g.astype(jnp.float32)).astype(x.dtype)


def layer_norm(x, g, b, eps=LN_EPS):
    xf = x.astype(jnp.float32)
    mu = jnp.mean(xf, axis=-1, keepdims=True)
    var = jnp.mean(jnp.square(xf - mu), axis=-1, keepdims=True)
    y = (xf - mu) * lax.rsqrt(var + eps) * g.astype(jnp.float32) + b.astype(jnp.float32)
    return y.astype(x.dtype)


def token_shift(t):
    return jnp.pad(t, ((0, 0), (1, 0), (0, 0)))[:, :-1]


def t5_causal_bucket(dist):
    max_exact = REL_BUCKETS // 2
    d_f = jnp.maximum(dist, 1).astype(jnp.float32)
    large = max_exact + (jnp.log(d_f / max_exact) / math.log(REL_MAX_DIST / max_exact)
                         * (REL_BUCKETS - max_exact)).astype(jnp.int32)
    large = jnp.minimum(large, REL_BUCKETS - 1)
    return jnp.where(dist < max_exact, dist, large)


def rwkv7_scan(r, w, k, v, a, bvec):
    bsz, _, h, n = r.shape

    def step(state, inp):
        r_t, w_t, k_t, v_t, a_t, b_t = inp
        sa = jnp.einsum('bhvk,bhk->bhv', state, a_t)
        state = (state * w_t[:, :, None, :] + sa[..., None] * b_t[:, :, None, :]
                 + v_t[..., None] * k_t[:, :, None, :])
        return state, jnp.einsum('bhvk,bhk->bhv', state, r_t)

    xs = tuple(jnp.swapaxes(t, 0, 1) for t in (r, w, k, v, a, bvec))
    _, ys = lax.scan(step, jnp.zeros((bsz, h, n, n), jnp.float32), xs)
    return jnp.swapaxes(ys, 0, 1)


def rwkv7_time_mix(proj, mu, w0, w2, a0, a2, g2, k_k, k_a, r_k, lnx_g, lnx_b):
    f32 = jnp.float32
    b, s, _ = proj.shape
    proj = proj + (token_shift(proj) - proj) * mu
    r, k, v, w_lo, a_lo, g_lo = jnp.split(proj, RWKV_SPLITS, axis=-1)
    w_log = -jax.nn.softplus(-(w0 + jnp.tanh(w_lo) @ w2).astype(f32)) - 0.5
    decay = jnp.exp(-jnp.exp(w_log))
    a = jax.nn.sigmoid((a0 + a_lo @ a2).astype(f32))
    g = jax.nn.sigmoid(g_lo) @ g2

    def heads(t):
        return t.astype(f32).reshape(b, s, RWKV_HEADS, HEAD_DIM)

    kk = heads(k * k_k)
    kk = kk / jnp.maximum(jnp.linalg.norm(kk, axis=-1, keepdims=True), 1e-12)
    k = k.astype(f32) * (1.0 + (a - 1.0) * k_a.astype(f32))
    r_h, k_h, v_h, a_h, w_h = heads(r), heads(k), heads(v), heads(a), heads(decay)
    y = rwkv7_scan(r_h, w_h, k_h, v_h, -kk, kk * a_h)
    mean = jnp.mean(y, axis=-1, keepdims=True)
    var = jnp.mean(jnp.square(y - mean), axis=-1, keepdims=True)
    y = ((y - mean) * lax.rsqrt(var + RWKV_LN_EPS)).reshape(b, s, RWKV_WIDTH) * lnx_g + lnx_b
    bonus = jnp.sum(r_h * k_h * r_k.astype(f32), axis=-1, keepdims=True) * v_h
    return (y + bonus.reshape(b, s, RWKV_WIDTH)) * g


def dilated_pattern_attention(q, k, v, bias_table, window, dil):
    bsz, s, h, e = q.shape
    n_sub = s // dil
    nb = -(-n_sub // ATTN_BLOCK)
    lp = nb * ATTN_BLOCK
    span = window // dil

    def to_blocks(t):
        t = jnp.swapaxes(t.reshape(bsz, n_sub, dil, h, e), 1, 2)
        t = jnp.pad(t, ((0, 0), (0, 0), (0, lp - n_sub), (0, 0), (0, 0)))
        return t.reshape(bsz, dil, nb, ATTN_BLOCK, h, e)

    def with_prev(t):
        prev = jnp.pad(t, ((0, 0), (0, 0), (1, 0), (0, 0), (0, 0), (0, 0)))[:, :, :-1]
        return jnp.concatenate([prev, t], axis=3)

    qb = to_blocks(q)
    kw = with_prev(to_blocks(k))
    vw = with_prev(to_blocks(v))

    qi = np.arange(ATTN_BLOCK)[:, None]
    ki = np.arange(2 * ATTN_BLOCK)[None, :]
    off = qi + ATTN_BLOCK - ki
    band = (off >= 0) & (off <= span)
    valid = band[None] & ((np.arange(nb)[:, None, None] > 0) | (ki >= ATTN_BLOCK)[None])
    bucket = t5_causal_bucket(jnp.asarray(np.clip(off, 0, span) * dil, jnp.int32))
    bias = jnp.transpose(bias_table[bucket], (2, 0, 1)).astype(jnp.float32)

    scores = jnp.einsum('brnqhe,brnkhe->brnhqk', qb, kw).astype(jnp.float32) * (HEAD_DIM ** -0.5)
    scores = jnp.where(valid[None, None, :, None], scores + bias, NEG_INF)
    m = jnp.max(scores, axis=-1, keepdims=True)
    p = jnp.exp(scores - m)
    den = jnp.sum(p, axis=-1)
    o = jnp.einsum('brnhqk,brnkhe->brnqhe', p.astype(vw.dtype), vw).astype(jnp.float32)
    o = o / jnp.swapaxes(den, -1, -2)[..., None]
    lse = jnp.swapaxes(m[..., 0] + jnp.log(den), -1, -2)

    def from_blocks(t):
        t = t.reshape((bsz, dil, lp) + t.shape[4:])[:, :, :n_sub]
        return jnp.swapaxes(t, 1, 2).reshape((bsz, s) + t.shape[3:])

    return from_blocks(o), from_blocks(lse)


def dilated_attention(proj, q_norm_g, k_norm_g, rel_bias):
    b, s, _ = proj.shape
    q, k, v = [t.reshape(b, s, ATTN_HEADS, HEAD_DIM) for t in jnp.split(proj, 3, axis=-1)]
    q = rms_norm(q, q_norm_g)
    k = rms_norm(k, k_norm_g)
    outs, lses = [], []
    for gi, (window, dil) in enumerate(DILATED_PATTERNS):
        hs = slice(gi * HEADS_PER_PATTERN, (gi + 1) * HEADS_PER_PATTERN)
        o, lse = dilated_pattern_attention(q[:, :, hs], k[:, :, hs], v[:, :, hs],
                                           rel_bias[:, hs], window, dil)
        outs.append(o)
        lses.append(lse)
    alpha = jax.nn.softmax(jnp.stack(lses, axis=0), axis=0)
    out = jnp.concatenate([outs[gi] * alpha[gi][..., None] for gi in range(len(DILATED_PATTERNS))],
                          axis=2)
    return out.reshape(b, s, ATTN_WIDTH)


def conformer_conv(proj, conv_w, conv_b, ln_g, ln_b):
    val, gate = jnp.split(proj, 2, axis=-1)
    u = val * jax.nn.sigmoid(gate)
    u = lax.conv_general_dilated(u.astype(conv_w.dtype), conv_w[:, None, :],
                                 window_strides=(1,), padding=((CONV_TAPS - 1, 0),),
                                 dimension_numbers=('NWC', 'WIO', 'NWC'),
                                 feature_group_count=CONV_WIDTH) + conv_b
    u = layer_norm(u, ln_g, ln_b)
    return jax.nn.silu(u)


def hierarchical_moe(h, rc_w, rc_b, rf_w, rf_b, w_in, w_out):
    b, s, d = h.shape
    n = b * s
    xt = h.reshape(n, d)
    pc = jax.nn.softmax((xt @ rc_w).astype(jnp.float32) + rc_b, axis=-1)
    pg, g_idx = lax.top_k(pc, 1)
    f_logits = ((xt @ rf_w).astype(jnp.float32) + rf_b).reshape(n, N_GROUPS, EXPERTS_PER_GROUP)
    idx = jnp.broadcast_to(g_idx[:, :, None], (n, 1, EXPERTS_PER_GROUP))
    f_in = jnp.take_along_axis(f_logits, idx, axis=1)[:, 0]
    fv, f_idx = lax.top_k(f_in, FINE_TOP_K)
    gate_w = jax.nn.softmax(fv, axis=-1) * pg
    expert_id = g_idx * EXPERTS_PER_GROUP + f_idx
    combine = jnp.sum(jax.nn.one_hot(expert_id, N_EXPERTS, dtype=jnp.float32) * gate_w[..., None],
                      axis=1)
    nblk = n // MOE_BLOCK

    def block(args):
        xb, cb = args
        gu = jnp.einsum('td,edf->tef', xb, w_in)
        gt, up = jnp.split(gu, 2, axis=-1)
        hid = jax.nn.silu(gt) * up * cb[..., None].astype(gu.dtype)
        return jnp.einsum('tef,efd->td', hid, w_out)

    y = lax.map(block, (xt.reshape(nblk, MOE_BLOCK, d), combine.reshape(nblk, MOE_BLOCK, N_EXPERTS)))
    return y.reshape(b, s, d)


def setup_inputs(seed: int = 0) -> dict:
    key = jax.random.key(seed)
    ks = iter(jax.random.split(key, 40))
    f32 = jnp.float32

    def nrm(shape, scale):
        return jax.random.normal(next(ks), shape, f32) * scale

    def gain(shape):
        return 1.0 + nrm(shape, 0.02)

    return {
        'x': nrm((BATCH, SEQ, D_MODEL), 1.0),
        'p': nrm((DEPTH, BATCH, SEQ, PLE_DIM), 1.0),
        'norm_mix_g': gain((DEPTH, D_MODEL)),
        'w_in': nrm((DEPTH, D_MODEL, IN_PROJ), D_MODEL ** -0.5),
        'rwkv_mu': jax.random.uniform(next(ks), (DEPTH, RWKV_PROJ), f32),
        'rwkv_w0': nrm((DEPTH, RWKV_WIDTH), 0.5),
        'rwkv_w2': nrm((DEPTH, DECAY_LORA, RWKV_WIDTH), 0.5 * DECAY_LORA ** -0.5),
        'rwkv_a0': nrm((DEPTH, RWKV_WIDTH), 0.3),
        'rwkv_a2': nrm((DEPTH, AAA_LORA, RWKV_WIDTH), 0.5 * AAA_LORA ** -0.5),
        'rwkv_g2': nrm((DEPTH, GATE_LORA, RWKV_WIDTH), GATE_LORA ** -0.5),
        'rwkv_k_k': 0.85 + nrm((DEPTH, RWKV_WIDTH), 0.05),
        'rwkv_k_a': 1.0 + nrm((DEPTH, RWKV_WIDTH), 0.05),
        'rwkv_r_k': nrm((DEPTH, RWKV_HEADS, HEAD_DIM), 0.1),
        'rwkv_lnx_g': gain((DEPTH, RWKV_WIDTH)),
        'rwkv_lnx_b': nrm((DEPTH, RWKV_WIDTH), 0.02),
        'q_norm_g': gain((DEPTH, HEAD_DIM)),
        'k_norm_g': gain((DEPTH, HEAD_DIM)),
        'rel_bias': nrm((REL_BUCKETS, ATTN_HEADS), 0.5),
        'conv_w': nrm((DEPTH, CONV_TAPS, CONV_WIDTH), CONV_TAPS ** -0.5),
        'conv_b': nrm((DEPTH, CONV_WIDTH), 0.02),
        'conv_ln_g': gain((DEPTH, CONV_WIDTH)),
        'conv_ln_b': nrm((DEPTH, CONV_WIDTH), 0.02),
        'w_out': nrm((DEPTH, D_MODEL, D_MODEL), D_MODEL ** -0.5),
        'norm_ffn_g': gain((DEPTH, D_MODEL)),
        'router_c_w': nrm((DEPTH, D_MODEL, N_GROUPS), D_MODEL ** -0.5),
        'router_c_b': nrm((DEPTH, N_GROUPS), 0.01),
        'router_f_w': nrm((DEPTH, D_MODEL, N_EXPERTS), D_MODEL ** -0.5),
        'router_f_b': nrm((DEPTH, N_EXPERTS), 0.01),
        'expert_w_in': nrm((DEPTH, N_EXPERTS, D_MODEL, 2 * EXPERT_FF), D_MODEL ** -0.5),
        'expert_w_out': nrm((DEPTH, N_EXPERTS, EXPERT_FF, D_MODEL), EXPERT_FF ** -0.5),
        'ple_norm_g': gain((DEPTH, D_MODEL)),
        'ple_gate_w': nrm((DEPTH, D_MODEL, D_MODEL), D_MODEL ** -0.5),
        'ple_proj': nrm((DEPTH, PLE_DIM, D_MODEL), PLE_DIM ** -0.5),
    }


def reference(x, p, norm_mix_g, w_in, rwkv_mu, rwkv_w0, rwkv_w2, rwkv_a0, rwkv_a2, rwkv_g2,
              rwkv_k_k, rwkv_k_a, rwkv_r_k, rwkv_lnx_g, rwkv_lnx_b, q_norm_g, k_norm_g, rel_bias,
              conv_w, conv_b, conv_ln_g, conv_ln_b, w_out, norm_ffn_g, router_c_w, router_c_b,
              router_f_w, router_f_b, expert_w_in, expert_w_out, ple_norm_g, ple_gate_w, ple_proj):
    h = x
    for i in range(DEPTH):
        xn = rms_norm(h, norm_mix_g[i])
        proj = xn @ w_in[i]
        p_rwkv, p_attn, p_conv = jnp.split(proj, (RWKV_PROJ, RWKV_PROJ + ATTN_PROJ), axis=-1)
        y_rwkv = rwkv7_time_mix(p_rwkv, rwkv_mu[i], rwkv_w0[i], rwkv_w2[i], rwkv_a0[i], rwkv_a2[i],
                                rwkv_g2[i], rwkv_k_k[i], rwkv_k_a[i], rwkv_r_k[i],
                                rwkv_lnx_g[i], rwkv_lnx_b[i])
        y_attn = dilated_attention(p_attn, q_norm_g[i], k_norm_g[i], rel_bias)
        y_conv = conformer_conv(p_conv, conv_w[i], conv_b[i], conv_ln_g[i], conv_ln_b[i])
        mix = jnp.concatenate([y_rwkv.astype(h.dtype), y_attn.astype(h.dtype),
                               y_conv.astype(h.dtype)], axis=-1)
        h = h + mix @ w_out[i]
        h = h + hierarchical_moe(rms_norm(h, norm_ffn_g[i]), router_c_w[i], router_c_b[i],
                                 router_f_w[i], router_f_b[i], expert_w_in[i], expert_w_out[i])
        gate = jax.nn.sigmoid(rms_norm(h, ple_norm_g[i]) @ ple_gate_w[i])
        h = h + (p[i] @ ple_proj[i]) * gate
    return h
```

```python
import functools
import math

import numpy as np
import jax
import jax.numpy as jnp
from jax import lax
from jax.experimental import pallas as pl
from jax.experimental.pallas import tpu as pltpu

F32 = jnp.float32
BF16 = jnp.bfloat16

HEAD_DIM = 64
DILATED_PATTERNS = ((128, 1), (512, 4), (2048, 16))
ATTN_BLOCK = 128
REL_BUCKETS = 32
REL_MAX_DIST = 2048
NEG_INF = -1e30
CONV_TAPS = 31
N_GROUPS = 8
EXPERTS_PER_GROUP = 8
N_EXPERTS = N_GROUPS * EXPERTS_PER_GROUP
RMS_EPS = 1e-6
LN_EPS = 1e-5
RWKV_LN_EPS = 64e-5

LANES = 128
SUBLANES = 8
VMEM_LIMIT_BYTES = 56 * 1024 * 1024

RWKV_CHUNK = 64
CONV_HALO = 32

NN = (((1,), (0,)), ((), ()))
NT = (((1,), (1,)), ((), ()))
TN = (((0,), (0,)), ((), ()))


def _cparams(*sem):
    return pltpu.CompilerParams(dimension_semantics=sem, vmem_limit_bytes=VMEM_LIMIT_BYTES)


def _dot(a, b, dims=NN):
    return lax.dot_general(a, b, dims, preferred_element_type=F32)


def _split(x):
    hi = x.astype(BF16)
    lo = (x - hi.astype(F32)).astype(BF16)
    return hi, lo


def _dot3(a, b, dims=NN):
    ah, al = _split(a)
    bh, bl = _split(b)
    return _dot(ah, bh, dims) + (_dot(ah, bl, dims) + _dot(al, bh, dims))


def _dot2(a, b_bf16, dims=NN):
    ah, al = _split(a)
    return _dot(ah, b_bf16, dims) + _dot(al, b_bf16, dims)


def _sigmoid(x):
    return 1.0 / (1.0 + jnp.exp(-x))


def _seg_matrix():
    r = lax.broadcasted_iota(jnp.int32, (LANES, LANES), 0) // HEAD_DIM
    c = lax.broadcasted_iota(jnp.int32, (LANES, LANES), 1) // HEAD_DIM
    return (r == c).astype(BF16)


def _head_sum(x, seg):
    w = x.shape[1]
    cols = [_dot2(x[:, j:j + LANES], seg) for j in range(0, w, LANES)]
    return cols[0] if len(cols) == 1 else jnp.concatenate(cols, axis=1)


def _rmsnorm_kernel(x_ref, g_ref, o_ref):
    x = x_ref[...]
    ms = jnp.mean(x * x, axis=-1, keepdims=True)
    o_ref[...] = (x * lax.rsqrt(ms + RMS_EPS) * g_ref[...]).astype(o_ref.dtype)


def rmsnorm_cast(x, g, tm=512):
    n, d = x.shape
    tm = min(tm, n)
    return pl.pallas_call(
        _rmsnorm_kernel,
        grid=(n // tm,),
        in_specs=[pl.BlockSpec((tm, d), lambda i: (i, 0)), pl.BlockSpec((1, d), lambda i: (0, 0))],
        out_specs=pl.BlockSpec((tm, d), lambda i: (i, 0)),
        out_shape=jax.ShapeDtypeStruct((n, d), BF16),
        compiler_params=_cparams("parallel"),
        name="rmsnorm",
    )(x, g.reshape(1, d))


def _rmsnorm_router_kernel(x_ref, g_ref, rw_ref, o_ref, z_ref):
    x = x_ref[...]
    ms = jnp.mean(x * x, axis=-1, keepdims=True)
    xn = x * lax.rsqrt(ms + RMS_EPS) * g_ref[...]
    o_ref[...] = xn.astype(o_ref.dtype)
    z_ref[...] = _dot3(xn, rw_ref[...])


def rmsnorm_router(x, g, router_w, tm=256):
    n, d = x.shape
    tm = min(tm, n)
    nr = router_w.shape[1]
    return pl.pallas_call(
        _rmsnorm_router_kernel,
        grid=(n // tm,),
        in_specs=[pl.BlockSpec((tm, d), lambda i: (i, 0)), pl.BlockSpec((1, d), lambda i: (0, 0)),
                  pl.BlockSpec((d, nr), lambda i: (0, 0))],
        out_specs=[pl.BlockSpec((tm, d), lambda i: (i, 0)), pl.BlockSpec((tm, nr), lambda i: (i, 0))],
        out_shape=[jax.ShapeDtypeStruct((n, d), BF16), jax.ShapeDtypeStruct((n, nr), F32)],
        compiler_params=_cparams("parallel"),
        name="rmsnorm_router",
    )(x, g.reshape(1, d), router_w)


def _mm_kernel(a_ref, w_ref, o_ref):
    o_ref[...] = _dot(a_ref[...], w_ref[...]).astype(o_ref.dtype)


def _mm_res_kernel(a_ref, w_ref, r_ref, o_ref):
    o_ref[...] = r_ref[...] + _dot(a_ref[...], w_ref[...])


def _pick(n, pref):
    for t in pref:
        if n % t == 0:
            return t
    return n


def matmul(a, w, out_dtype=F32, residual=None, tm=1024, tn=None, name="matmul"):
    m, k = a.shape
    n = w.shape[1]
    tm = _pick(m, (tm, 512, 256, 128))
    tn = tn or _pick(n, (1024, 768, 512, 384, 256, 128))
    grid = (m // tm, n // tn)
    a_spec = pl.BlockSpec((tm, k), lambda i, j: (i, 0))
    w_spec = pl.BlockSpec((k, tn), lambda i, j: (0, j))
    o_spec = pl.BlockSpec((tm, tn), lambda i, j: (i, j))
    if residual is None:
        return pl.pallas_call(
            _mm_kernel, grid=grid, in_specs=[a_spec, w_spec], out_specs=o_spec,
            out_shape=jax.ShapeDtypeStruct((m, n), out_dtype),
            compiler_params=_cparams("parallel", "arbitrary"), name=name,
        )(a, w)
    return pl.pallas_call(
        _mm_res_kernel, grid=grid, in_specs=[a_spec, w_spec, o_spec], out_specs=o_spec,
        out_shape=jax.ShapeDtypeStruct((m, n), F32),
        compiler_params=_cparams("parallel", "arbitrary"), name=name,
    )(a, w, residual)


def _rwkv_prep_kernel(x_ref, prev_ref, mu_ref, w0_ref, w2_ref, a0_ref, a2_ref, g2_ref, kk_ref, ka_ref,
                      rk_ref, r_o, lw_o, k_o, v_o, a_o, b_o, g_o, bonus_o, *, seq, width):
    t = x_ref.shape[0]
    i = pl.program_id(0)
    x = x_ref[...]
    seq_start = (i * t) % seq == 0
    prev = jnp.where(seq_start, 0.0, prev_ref[SUBLANES - 1:SUBLANES, :])
    rows = lax.broadcasted_iota(jnp.int32, x.shape, 0)
    shifted = jnp.where(rows == 0, prev, pltpu.roll(x, 1, axis=0))
    x = x + (shifted - x) * mu_ref[...]
    w = width
    r = x[:, 0:w]
    k = x[:, w:2 * w]
    v = x[:, 2 * w:3 * w]
    o = 3 * w
    w_lo = x[:, o:o + LANES]
    a_lo = x[:, o + LANES:o + 2 * LANES]
    g_lo = x[:, o + 2 * LANES:o + 4 * LANES]
    seg = _seg_matrix()

    wv = w0_ref[...] + _dot3(jnp.tanh(w_lo), w2_ref[...])
    w_log = -(jnp.maximum(-wv, 0.0) + jnp.log(1.0 + jnp.exp(-jnp.abs(wv)))) - 0.5
    lw_o[...] = -jnp.exp(w_log)
    a = _sigmoid(a0_ref[...] + _dot3(a_lo, a2_ref[...]))
    g_o[...] = _dot3(_sigmoid(g_lo), g2_ref[...])

    kk = k * kk_ref[...]
    nrm = jnp.sqrt(_head_sum(kk * kk, seg))
    kk = kk / jnp.maximum(nrm, 1e-12)
    k = k * (1.0 + (a - 1.0) * ka_ref[...])
    r_o[...] = r
    k_o[...] = k
    v_o[...] = v
    a_o[...] = -kk
    b_o[...] = kk * a
    bonus_o[...] = _head_sum(r * k * rk_ref[...], seg) * v


def rwkv_prep(proj, mu, w0, w2, a0, a2, g2, k_k, k_a, r_k, seq, tm=128):
    n, wp = proj.shape
    width = w0.shape[-1]
    tm = min(tm, seq)
    row = lambda i: (i, 0)
    fix = lambda i: (0, 0)
    vec = pl.BlockSpec((1, width), fix)
    out = pl.BlockSpec((tm, width), row)
    per = tm // SUBLANES
    return pl.pallas_call(
        functools.partial(_rwkv_prep_kernel, seq=seq, width=width),
        grid=(n // tm,),
        in_specs=[pl.BlockSpec((tm, wp), row),
                  pl.BlockSpec((SUBLANES, wp), lambda i: (jnp.maximum(i * per - 1, 0), 0)),
                  pl.BlockSpec((1, wp), fix), vec, pl.BlockSpec(w2.shape, fix), vec,
                  pl.BlockSpec(a2.shape, fix), pl.BlockSpec(g2.shape, fix), vec, vec, vec],
        out_specs=[out] * 8,
        out_shape=[jax.ShapeDtypeStruct((n, width), F32)] * 8,
        compiler_params=_cparams("parallel"), name="rwkv_prep",
    )(proj, proj, mu, w0, w2, a0, a2, g2, k_k, k_a, r_k)


def _rwkv_scan_kernel(r_ref, lw_ref, k_ref, v_ref, a_ref, b_ref, y_ref, st_ref, *, chunk):
    c_len = chunk
    hd = HEAD_DIM

    @pl.when(pl.program_id(2) == 0)
    def _():
        st_ref[...] = jnp.zeros_like(st_ref)

    nchunks = r_ref.shape[0] // c_len
    row = lax.broadcasted_iota(jnp.int32, (c_len, c_len), 0)
    col = lax.broadcasted_iota(jnp.int32, (c_len, c_len), 1)
    strict = row > col
    incl = row >= col
    eye_c = (row == col).astype(F32)
    rk = lax.broadcasted_iota(jnp.int32, (hd, hd), 0)
    ck = lax.broadcasted_iota(jnp.int32, (hd, hd), 1)
    eye_k = (rk == ck).astype(F32)
    ridx = lax.broadcasted_iota(jnp.int32, (c_len, LANES), 0)

    def chunk_body(c, carry):
        rows = pl.ds(pl.multiple_of(c * c_len, c_len), c_len)
        lw = lw_ref[rows, :]
        cum = lw
        sh = 1
        while sh < c_len:
            cum = cum + jnp.where(ridx >= sh, pltpu.roll(cum, sh, axis=0), 0.0)
            sh *= 2
        total = cum[c_len - 1:c_len, :]
        e_neg = jnp.exp(-cum)
        rt = r_ref[rows, :] * jnp.exp(cum)
        at = a_ref[rows, :] * jnp.exp(cum - lw)
        b = b_ref[rows, :]
        k = k_ref[rows, :]
        bt = b * e_neg
        kt = k * e_neg
        e_end = jnp.exp(total - cum)
        bh = b * e_end
        kh = k * e_end
        v = v_ref[rows, :]
        g_end = jnp.exp(total)
        ys = []
        for h in range(LANES // hd):
            sl = slice(h * hd, (h + 1) * hd)
            at_h, rt_h, bt_h, kt_h, v_h = at[:, sl], rt[:, sl], bt[:, sl], kt[:, sl], v[:, sl]
            a_ab = jnp.where(strict, _dot3(at_h, bt_h, NT), 0.0)
            a_ak = jnp.where(strict, _dot3(at_h, kt_h, NT), 0.0)
            a_rb = jnp.where(incl, _dot3(rt_h, bt_h, NT), 0.0)
            a_rk = jnp.where(incl, _dot3(rt_h, kt_h, NT), 0.0)
            inv = eye_c + a_ab
            lp = a_ab
            p = 2
            while p < c_len:
                lp = _dot3(lp, lp)
                inv = inv + _dot3(inv, lp)
                p *= 2
            s0 = st_ref[h]
            u = _dot3(inv, _dot3(at_h, s0) + _dot3(a_ak, v_h))
            ys.append(_dot3(rt_h, s0) + _dot3(a_rb, u) + _dot3(a_rk, v_h))
            lhs = jnp.concatenate([bh[:, sl], kh[:, sl], eye_k * g_end[:, sl]], axis=0)
            rhs = jnp.concatenate([u, v_h, s0], axis=0)
            st_ref[h] = _dot3(lhs, rhs, TN)
        y_ref[rows, :] = jnp.concatenate(ys, axis=1)
        return carry

    lax.fori_loop(0, nchunks, chunk_body, 0)


def rwkv_scan(r, lw, k, v, a, b, batch, seq, tblk=512):
    n, width = r.shape
    tblk = min(tblk, seq)
    nt = seq // tblk
    spec = pl.BlockSpec((tblk, LANES), lambda bi, hp, tb: (bi * nt + tb, hp))
    return pl.pallas_call(
        functools.partial(_rwkv_scan_kernel, chunk=RWKV_CHUNK),
        grid=(batch, width // LANES, nt),
        in_specs=[spec] * 6,
        out_specs=spec,
        out_shape=jax.ShapeDtypeStruct((n, width), F32),
        scratch_shapes=[pltpu.VMEM((LANES // HEAD_DIM, HEAD_DIM, HEAD_DIM), F32)],
        compiler_params=_cparams("parallel", "parallel", "arbitrary"), name="rwkv_scan",
    )(r, lw, k, v, a, b)


def _rwkv_post_kernel(y_ref, bonus_ref, g_ref, lg_ref, lb_ref, o_ref):
    seg = _seg_matrix()
    y = y_ref[...]
    inv_n = 1.0 / HEAD_DIM
    mean = _head_sum(y, seg) * inv_n
    yc = y - mean
    var = _head_sum(yc * yc, seg) * inv_n
    yn = yc * lax.rsqrt(var + RWKV_LN_EPS) * lg_ref[...] + lb_ref[...]
    o_ref[...] = ((yn + bonus_ref[...]) * g_ref[...]).astype(o_ref.dtype)


def rwkv_post(y, bonus, g, lnx_g, lnx_b, tm=256):
    n, width = y.shape
    tm = _pick(n, (tm, 128))
    row = pl.BlockSpec((tm, width), lambda i: (i, 0))
    vec = pl.BlockSpec((1, width), lambda i: (0, 0))
    return pl.pallas_call(
        _rwkv_post_kernel, grid=(n // tm,),
        in_specs=[row, row, row, vec, vec], out_specs=row,
        out_shape=jax.ShapeDtypeStruct((n, width), BF16),
        compiler_params=_cparams("parallel"), name="rwkv_post",
    )(y, bonus, g, lnx_g, lnx_b)


def _t5_causal_bucket(dist):
    max_exact = REL_BUCKETS // 2
    d_f = jnp.maximum(dist, 1).astype(F32)
    large = max_exact + (jnp.log(d_f / max_exact) / math.log(REL_MAX_DIST / max_exact)
                         * (REL_BUCKETS - max_exact)).astype(jnp.int32)
    large = jnp.minimum(large, REL_BUCKETS - 1)
    return jnp.where(dist < max_exact, dist, large)


def _pattern_bias(rel_bias_g, window, dil):
    span = window // dil
    qi = np.arange(ATTN_BLOCK)[:, None]
    ki = np.arange(2 * ATTN_BLOCK)[None, :]
    off = qi + ATTN_BLOCK - ki
    band = (off >= 0) & (off <= span)
    bucket = _t5_causal_bucket(jnp.asarray(np.clip(off, 0, span) * dil, jnp.int32))
    bias = jnp.transpose(rel_bias_g[bucket], (2, 0, 1)).astype(F32)
    return jnp.where(jnp.asarray(band)[None], bias, NEG_INF)


def _attn_kernel(q_ref, kp_ref, kc_ref, vp_ref, vc_ref, bias_ref, qg_ref, kg_ref, o_ref, lse_ref):
    hd = HEAD_DIM
    nheads = q_ref.shape[1] // hd
    first = pl.program_id(1) == 0
    ki = lax.broadcasted_iota(jnp.int32, (ATTN_BLOCK, 2 * ATTN_BLOCK), 1)
    no_prev = jnp.logical_and(first, ki < ATTN_BLOCK)
    scale = hd ** -0.5

    def norm(x, g):
        ms = jnp.mean(x * x, axis=-1, keepdims=True)
        return x * lax.rsqrt(ms + RMS_EPS) * g

    outs, lses = [], []
    for h in range(nheads):
        sl = slice(h * hd, (h + 1) * hd)
        q = norm(q_ref[:, sl], qg_ref[...]).astype(BF16)
        k = jnp.concatenate([norm(kp_ref[:, sl], kg_ref[...]), norm(kc_ref[:, sl], kg_ref[...])],
                            axis=0).astype(BF16)
        v = jnp.concatenate([vp_ref[:, sl], vc_ref[:, sl]], axis=0).astype(BF16)
        s = _dot(q, k, NT) * scale + bias_ref[h]
        s = jnp.where(no_prev, NEG_INF, s)
        m = jnp.max(s, axis=-1, keepdims=True)
        p = jnp.exp(s - m)
        den = jnp.sum(p, axis=-1, keepdims=True)
        outs.append(_dot(p.astype(BF16), v) / den)
        lses.append(jnp.broadcast_to(m + jnp.log(den), (ATTN_BLOCK, hd)))
    o_ref[...] = jnp.concatenate(outs, axis=1)
    lse_ref[...] = jnp.concatenate(lses, axis=1)


def attn_pattern(proj, bias, q_g, k_g, gi, dil, batch, seq, width):
    n = proj.shape[0]
    npat = len(DILATED_PATTERNS)
    pw = width // npat
    nsub = seq // dil
    nb = nsub // ATTN_BLOCK
    sec = width // pw
    rowlen = 3 * sec
    x = proj.reshape(batch * nsub, dil * 3 * width)

    def spec(section, prev):
        def imap(b, nblk, r):
            blk = jnp.maximum(nblk - 1, 0) if prev else nblk
            return (b * nb + blk, r * rowlen + section * sec + gi)
        return pl.BlockSpec((ATTN_BLOCK, pw), imap)

    fix3 = pl.BlockSpec(bias.shape, lambda b, nblk, r: (0, 0, 0))
    vec = pl.BlockSpec((1, HEAD_DIM), lambda b, nblk, r: (0, 0))
    ospec = pl.BlockSpec((ATTN_BLOCK, pw), lambda b, nblk, r: (b * nb + nblk, r))
    o, lse = pl.pallas_call(
        _attn_kernel, grid=(batch, nb, dil),
        in_specs=[spec(0, False), spec(1, True), spec(1, False), spec(2, True), spec(2, False), fix3, vec, vec],
        out_specs=[ospec, ospec],
        out_shape=[jax.ShapeDtypeStruct((batch * nsub, dil * pw), F32)] * 2,
        compiler_params=_cparams("parallel", "parallel", "parallel"), name=f"attn_d{dil}",
    )(x, x, x, x, x, bias, q_g, k_g)
    return o.reshape(n, pw), lse.reshape(n, pw)


def _attn_mix_kernel(o0, o1, o2, l0, l1, l2, out_ref):
    a, b, c = l0[...], l1[...], l2[...]
    m = jnp.maximum(jnp.maximum(a, b), c)
    ea, eb, ec = jnp.exp(a - m), jnp.exp(b - m), jnp.exp(c - m)
    inv = 1.0 / (ea + eb + ec)
    out_ref[...] = jnp.concatenate([o0[...] * (ea * inv), o1[...] * (eb * inv), o2[...] * (ec * inv)],
                                   axis=1).astype(out_ref.dtype)


def attn_mix(outs, lses, tm=512):
    n, pw = outs[0].shape
    tm = _pick(n, (tm, 256, 128))
    row = pl.BlockSpec((tm, pw), lambda i: (i, 0))
    return pl.pallas_call(
        _attn_mix_kernel, grid=(n // tm,), in_specs=[row] * 6,
        out_specs=pl.BlockSpec((tm, 3 * pw), lambda i: (i, 0)),
        out_shape=jax.ShapeDtypeStruct((n, 3 * pw), BF16),
        compiler_params=_cparams("parallel"), name="attn_mix",
    )(*outs, *lses)


def _conv_kernel(x_ref, halo_ref, w_ref, b_ref, lg_ref, lb_ref, o_ref, ext_ref, *, seq, width):
    t = x_ref.shape[0]
    i = pl.program_id(0)
    seq_start = (i * t) % seq == 0

    def glu(x):
        return x[:, :width] * _sigmoid(x[:, width:])

    ext_ref[0:CONV_HALO, :] = jnp.where(seq_start, 0.0, glu(halo_ref[...]))
    ext_ref[CONV_HALO:, :] = glu(x_ref[...])
    base = CONV_HALO - (CONV_TAPS - 1)
    acc = jnp.broadcast_to(b_ref[...], (t, width))
    for j in range(CONV_TAPS):
        acc = acc + w_ref[j:j + 1, :] * ext_ref[base + j:base + j + t, :]
    mu = jnp.mean(acc, axis=-1, keepdims=True)
    d = acc - mu
    var = jnp.mean(d * d, axis=-1, keepdims=True)
    y = d * lax.rsqrt(var + LN_EPS) * lg_ref[...] + lb_ref[...]
    o_ref[...] = (y * _sigmoid(y)).astype(o_ref.dtype)


def conv_module(proj, conv_w, conv_b, ln_g, ln_b, seq, tm=256):
    n, w2 = proj.shape
    width = w2 // 2
    tm = min(tm, seq)
    per = tm // CONV_HALO
    fix = lambda i: (0, 0)
    vec = pl.BlockSpec((1, width), fix)
    return pl.pallas_call(
        functools.partial(_conv_kernel, seq=seq, width=width), grid=(n // tm,),
        in_specs=[pl.BlockSpec((tm, w2), lambda i: (i, 0)),
                  pl.BlockSpec((CONV_HALO, w2), lambda i: (jnp.maximum(i * per - 1, 0), 0)),
                  pl.BlockSpec((CONV_TAPS, width), fix), vec, vec, vec],
        out_specs=pl.BlockSpec((tm, width), lambda i: (i, 0)),
        out_shape=jax.ShapeDtypeStruct((n, width), BF16),
        scratch_shapes=[pltpu.VMEM((tm + CONV_HALO, width), F32)],
        compiler_params=_cparams("parallel"), name="conv_module",
    )(proj, proj, conv_w, conv_b, ln_g, ln_b)


ROUTER_LANES = LANES


def _route_kernel(z_ref, bias_ref, c_ref):
    z = z_ref[...] + bias_ref[...]
    lane = lax.broadcasted_iota(jnp.int32, z.shape, 1).astype(F32)
    ninf = -jnp.inf
    big = float(ROUTER_LANES)
    zc = jnp.where(lane < N_GROUPS, z, ninf)
    mc = jnp.max(zc, axis=-1, keepdims=True)
    pg = 1.0 / jnp.sum(jnp.exp(zc - mc), axis=-1, keepdims=True)
    g_idx = jnp.min(jnp.where(zc == mc, lane, big), axis=-1, keepdims=True)
    lo = N_GROUPS + g_idx * EXPERTS_PER_GROUP
    zf = jnp.where(jnp.logical_and(lane >= lo, lane < lo + EXPERTS_PER_GROUP), z, ninf)
    m1 = jnp.max(zf, axis=-1, keepdims=True)
    i1 = jnp.min(jnp.where(zf == m1, lane, big), axis=-1, keepdims=True)
    zf2 = jnp.where(lane == i1, ninf, zf)
    m2 = jnp.max(zf2, axis=-1, keepdims=True)
    i2 = jnp.min(jnp.where(zf2 == m2, lane, big), axis=-1, keepdims=True)
    e2 = jnp.exp(m2 - m1)
    w1 = pg / (1.0 + e2)
    w2 = pg * e2 / (1.0 + e2)
    c_ref[...] = jnp.where(lane == i1, w1, 0.0) + jnp.where(lane == i2, w2, 0.0)


def route(logits, bias, tm=512):
    n, w = logits.shape
    tm = _pick(n, (tm, 256, 128))
    row = pl.BlockSpec((tm, w), lambda i: (i, 0))
    return pl.pallas_call(
        _route_kernel, grid=(n // tm,),
        in_specs=[row, pl.BlockSpec((1, w), lambda i: (0, 0))], out_specs=row,
        out_shape=jax.ShapeDtypeStruct((n, w), F32),
        compiler_params=_cparams("parallel"), name="route",
    )(logits, bias)


def _moe_dense_kernel(x_ref, c_ref, wi_ref, wo_ref, h_ref, o_ref, *, ff):
    e = pl.program_id(1)

    @pl.when(e == 0)
    def _():
        o_ref[...] = h_ref[...]

    lane = lax.broadcasted_iota(jnp.int32, c_ref.shape, 1)
    gate = jnp.sum(jnp.where(lane == e + N_GROUPS, c_ref[...], 0.0), axis=-1, keepdims=True)
    gu = _dot(x_ref[...], wi_ref[0])
    gt, up = gu[:, :ff], gu[:, ff:]
    hid = gt * _sigmoid(gt) * up * gate
    o_ref[...] += _dot(hid.astype(BF16), wo_ref[0])


def moe_dense(xn, combine, w_in, w_out, h, tm=512):
    n, d = xn.shape
    ne, _, ff2 = w_in.shape
    tm = _pick(n, (tm, 256, 128))
    row = lambda i, e: (i, 0)
    return pl.pallas_call(
        functools.partial(_moe_dense_kernel, ff=ff2 // 2), grid=(n // tm, ne),
        in_specs=[pl.BlockSpec((tm, d), row), pl.BlockSpec((tm, combine.shape[1]), row),
                  pl.BlockSpec((1, d, ff2), lambda i, e: (e, 0, 0)),
                  pl.BlockSpec((1, ff2 // 2, d), lambda i, e: (e, 0, 0)),
                  pl.BlockSpec((tm, d), row)],
        out_specs=pl.BlockSpec((tm, d), row),
        out_shape=jax.ShapeDtypeStruct((n, d), F32),
        compiler_params=_cparams("parallel", "arbitrary"), name="moe_dense",
    )(xn, combine, w_in, w_out, h)


def _ple_kernel(a_ref, w_ref, p_ref, pw_ref, h_ref, o_ref):
    gate = _sigmoid(_dot(a_ref[...], w_ref[...]))
    o_ref[...] = h_ref[...] + _dot(p_ref[...], pw_ref[...]) * gate


def ple(xn, gate_w, p, proj_w, h, tm=1024, tn=512):
    n, d = xn.shape
    pd = p.shape[1]
    tm = _pick(n, (tm, 512, 256, 128))
    tn = _pick(d, (tn, 256, 128))
    tile = pl.BlockSpec((tm, tn), lambda i, j: (i, j))
    return pl.pallas_call(
        _ple_kernel, grid=(n // tm, d // tn),
        in_specs=[pl.BlockSpec((tm, d), lambda i, j: (i, 0)), pl.BlockSpec((d, tn), lambda i, j: (0, j)),
                  pl.BlockSpec((tm, pd), lambda i, j: (i, 0)), pl.BlockSpec((pd, tn), lambda i, j: (0, j)), tile],
        out_specs=tile,
        out_shape=jax.ShapeDtypeStruct((n, d), F32),
        compiler_params=_cparams("parallel", "arbitrary"), name="ple",
    )(xn, gate_w, p, proj_w, h)


def _pad_cols(x, total):
    return jnp.pad(x, ((0, 0), (0, total - x.shape[1])))


def _pad_rows(x, total):
    return jnp.pad(x, ((0, total - x.shape[0]), (0, 0)))


def _pack_rwkv_cols(x, width, lora):
    dl, al, gl = lora
    o = 3 * width
    return jnp.concatenate([
        x[:, :o], _pad_cols(x[:, o:o + dl], LANES), _pad_cols(x[:, o + dl:o + dl + al], LANES),
        _pad_cols(x[:, o + dl + al:o + dl + al + gl], 2 * LANES)], axis=1)


def kernel(x, p, norm_mix_g, w_in, rwkv_mu, rwkv_w0, rwkv_w2, rwkv_a0, rwkv_a2, rwkv_g2, rwkv_k_k, rwkv_k_a, rwkv_r_k, rwkv_lnx_g, rwkv_lnx_b, q_norm_g, k_norm_g, rel_bias, conv_w, conv_b, conv_ln_g, conv_ln_b, w_out, norm_ffn_g, router_c_w, router_c_b, router_f_w, router_f_b, expert_w_in, expert_w_out, ple_norm_g, ple_gate_w, ple_proj):
    batch, seq, d = x.shape
    depth = w_in.shape[0]
    n = batch * seq
    rw = rwkv_w0.shape[-1]
    lora = (rwkv_w2.shape[1], rwkv_a2.shape[1], rwkv_g2.shape[1])
    assert max(lora[0], lora[1]) <= LANES and lora[2] <= 2 * LANES
    rwkv_proj = 3 * rw + sum(lora)
    cw = conv_w.shape[-1]
    aw = d - rw - cw
    npat = len(DILATED_PATTERNS)
    hpp = aw // HEAD_DIM // npat
    row = lambda v: v.reshape(1, -1)

    h = x.reshape(n, d)
    for i in range(depth):
        w_i = w_in[i]
        w_rwkv = _pack_rwkv_cols(w_i[:, :rwkv_proj], rw, lora).astype(BF16)
        w_attn = w_i[:, rwkv_proj:rwkv_proj + 3 * aw].astype(BF16)
        w_conv = w_i[:, rwkv_proj + 3 * aw:].astype(BF16)
        mu = _pack_rwkv_cols(row(rwkv_mu[i]), rw, lora)
        w2 = _pad_rows(rwkv_w2[i], LANES)
        a2 = _pad_rows(rwkv_a2[i], LANES)
        g2 = _pad_rows(rwkv_g2[i], 2 * LANES)

        xn = rmsnorm_cast(h, norm_mix_g[i])
        p_rwkv = matmul(xn, w_rwkv, name="proj_rwkv")
        p_attn = matmul(xn, w_attn, name="proj_attn")
        p_conv = matmul(xn, w_conv, name="proj_conv")

        r, lw, k, v, a, b, g, bonus = rwkv_prep(
            p_rwkv, mu, row(rwkv_w0[i]), w2, row(rwkv_a0[i]), a2, g2, row(rwkv_k_k[i]), row(rwkv_k_a[i]),
            row(rwkv_r_k[i]), seq)
        y = rwkv_scan(r, lw, k, v, a, b, batch, seq)
        y_rwkv = rwkv_post(y, bonus, g, row(rwkv_lnx_g[i]), row(rwkv_lnx_b[i]))

        outs, lses = [], []
        for gi, (window, dil) in enumerate(DILATED_PATTERNS):
            bias = _pattern_bias(rel_bias[:, gi * hpp:(gi + 1) * hpp], window, dil)
            o, lse = attn_pattern(p_attn, bias, row(q_norm_g[i]), row(k_norm_g[i]), gi, dil, batch, seq, aw)
            outs.append(o)
            lses.append(lse)
        y_attn = attn_mix(outs, lses)

        y_conv = conv_module(p_conv, conv_w[i], row(conv_b[i]), row(conv_ln_g[i]), row(conv_ln_b[i]), seq)

        mix = jnp.concatenate([y_rwkv, y_attn, y_conv], axis=1)
        h = matmul(mix, w_out[i].astype(BF16), residual=h, name="out_proj")

        router_w = _pad_cols(jnp.concatenate([router_c_w[i], router_f_w[i]], axis=1), ROUTER_LANES)
        router_b = _pad_cols(row(jnp.concatenate([router_c_b[i], router_f_b[i]])), ROUTER_LANES)
        xn, logits = rmsnorm_router(h, norm_ffn_g[i], router_w)
        combine = route(logits, router_b)
        h = moe_dense(xn, combine, expert_w_in[i].astype(BF16), expert_w_out[i].astype(BF16), h)

        xn = rmsnorm_cast(h, ple_norm_g[i])
        h = ple(xn, ple_gate_w[i].astype(BF16), p[i].reshape(n, -1).astype(BF16), ple_proj[i].astype(BF16), h)
    return h.reshape(batch, seq, d)
```

```python
import functools
import math

import numpy as np
import jax
import jax.numpy as jnp
from jax import lax
from jax.experimental import pallas as pl
from jax.experimental.pallas import tpu as pltpu

F32 = jnp.float32
BF16 = jnp.bfloat16

HEAD_DIM = 64
DILATED_PATTERNS = ((128, 1), (512, 4), (2048, 16))
ATTN_BLOCK = 128
REL_BUCKETS = 32
REL_MAX_DIST = 2048
NEG_INF = -1e30
CONV_TAPS = 31
N_GROUPS = 8
EXPERTS_PER_GROUP = 8
N_EXPERTS = N_GROUPS * EXPERTS_PER_GROUP
RMS_EPS = 1e-6
LN_EPS = 1e-5
RWKV_LN_EPS = 64e-5

LANES = 128
SUBLANES = 8
VMEM_LIMIT_BYTES = 56 * 1024 * 1024

RWKV_CHUNK = 64
RWKV_WAVE = 8
RWKV_TBLK = 1024
ATTN_GROUP = 4
CONV_HALO = 32

NN = (((1,), (0,)), ((), ()))
NT = (((1,), (1,)), ((), ()))
TN = (((0,), (0,)), ((), ()))


def _cparams(*sem):
    return pltpu.CompilerParams(dimension_semantics=sem, vmem_limit_bytes=VMEM_LIMIT_BYTES)


def _dot(a, b, dims=NN):
    return lax.dot_general(a, b, dims, preferred_element_type=F32)


def _split(x):
    hi = x.astype(BF16)
    lo = (x - hi.astype(F32)).astype(BF16)
    return hi, lo


def _dot3(a, b, dims=NN):
    ah, al = _split(a)
    bh, bl = _split(b)
    return _dot(ah, bh, dims) + (_dot(ah, bl, dims) + _dot(al, bh, dims))


def _dot2(a, b_bf16, dims=NN):
    ah, al = _split(a)
    return _dot(ah, b_bf16, dims) + _dot(al, b_bf16, dims)


def _sigmoid(x):
    return 1.0 / (1.0 + jnp.exp(-x))


def _seg_matrix():
    r = lax.broadcasted_iota(jnp.int32, (LANES, LANES), 0) // HEAD_DIM
    c = lax.broadcasted_iota(jnp.int32, (LANES, LANES), 1) // HEAD_DIM
    return (r == c).astype(BF16)


def _head_sum(x, seg):
    w = x.shape[1]
    cols = [_dot2(x[:, j:j + LANES], seg) for j in range(0, w, LANES)]
    return cols[0] if len(cols) == 1 else jnp.concatenate(cols, axis=1)


def _rmsnorm_kernel(x_ref, g_ref, o_ref):
    x = x_ref[...]
    ms = jnp.mean(x * x, axis=-1, keepdims=True)
    o_ref[...] = (x * lax.rsqrt(ms + RMS_EPS) * g_ref[...]).astype(o_ref.dtype)


def rmsnorm_cast(x, g, tm=512):
    n, d = x.shape
    tm = min(tm, n)
    return pl.pallas_call(
        _rmsnorm_kernel,
        grid=(n // tm,),
        in_specs=[pl.BlockSpec((tm, d), lambda i: (i, 0)), pl.BlockSpec((1, d), lambda i: (0, 0))],
        out_specs=pl.BlockSpec((tm, d), lambda i: (i, 0)),
        out_shape=jax.ShapeDtypeStruct((n, d), BF16),
        compiler_params=_cparams("parallel"),
        name="rmsnorm",
    )(x, g.reshape(1, d))


def _mm_kernel(a_ref, w_ref, o_ref):
    o_ref[...] = _dot(a_ref[...], w_ref[...]).astype(o_ref.dtype)


def _mm_res_kernel(a_ref, w_ref, r_ref, o_ref):
    o_ref[...] = r_ref[...] + _dot(a_ref[...], w_ref[...])


def _pick(n, pref):
    for t in pref:
        if n % t == 0:
            return t
    return n


def matmul(a, w, out_dtype=F32, residual=None, tm=1024, tn=None, name="matmul"):
    m, k = a.shape
    n = w.shape[1]
    tm = _pick(m, (tm, 512, 256, 128))
    tn = tn or _pick(n, (1024, 768, 512, 384, 256, 128))
    grid = (m // tm, n // tn)
    a_spec = pl.BlockSpec((tm, k), lambda i, j: (i, 0))
    w_spec = pl.BlockSpec((k, tn), lambda i, j: (0, j))
    o_spec = pl.BlockSpec((tm, tn), lambda i, j: (i, j))
    if residual is None:
        return pl.pallas_call(
            _mm_kernel, grid=grid, in_specs=[a_spec, w_spec], out_specs=o_spec,
            out_shape=jax.ShapeDtypeStruct((m, n), out_dtype),
            compiler_params=_cparams("parallel", "arbitrary"), name=name,
        )(a, w)
    return pl.pallas_call(
        _mm_res_kernel, grid=grid, in_specs=[a_spec, w_spec, o_spec], out_specs=o_spec,
        out_shape=jax.ShapeDtypeStruct((m, n), F32),
        compiler_params=_cparams("parallel", "arbitrary"), name=name,
    )(a, w, residual)


def _rwkv_prep_kernel(x_ref, prev_ref, mu_ref, w0_ref, w2_ref, a0_ref, a2_ref, g2_ref, kk_ref, ka_ref,
                      rk_ref, r_o, lw_o, k_o, v_o, a_o, b_o, g_o, bonus_o, *, seq, width):
    t = x_ref.shape[0]
    i = pl.program_id(0)
    x = x_ref[...]
    seq_start = (i * t) % seq == 0
    prev = jnp.where(seq_start, 0.0, prev_ref[SUBLANES - 1:SUBLANES, :])
    rows = lax.broadcasted_iota(jnp.int32, x.shape, 0)
    shifted = jnp.where(rows == 0, prev, pltpu.roll(x, 1, axis=0))
    x = x + (shifted - x) * mu_ref[...]
    w = width
    r = x[:, 0:w]
    k = x[:, w:2 * w]
    v = x[:, 2 * w:3 * w]
    o = 3 * w
    w_lo = x[:, o:o + LANES]
    a_lo = x[:, o + LANES:o + 2 * LANES]
    g_lo = x[:, o + 2 * LANES:o + 4 * LANES]
    seg = _seg_matrix()

    wv = w0_ref[...] + _dot3(jnp.tanh(w_lo), w2_ref[...])
    w_log = -(jnp.maximum(-wv, 0.0) + jnp.log(1.0 + jnp.exp(-jnp.abs(wv)))) - 0.5
    lw_o[...] = -jnp.exp(w_log)
    a = _sigmoid(a0_ref[...] + _dot3(a_lo, a2_ref[...]))
    g_o[...] = _dot3(_sigmoid(g_lo), g2_ref[...])

    kk = k * kk_ref[...]
    nrm = jnp.sqrt(_head_sum(kk * kk, seg))
    kk = kk / jnp.maximum(nrm, 1e-12)
    k = k * (1.0 + (a - 1.0) * ka_ref[...])
    r_o[...] = r
    k_o[...] = k
    v_o[...] = v
    a_o[...] = -kk
    b_o[...] = kk * a
    bonus_o[...] = _head_sum(r * k * rk_ref[...], seg) * v


def rwkv_prep(proj, mu, w0, w2, a0, a2, g2, k_k, k_a, r_k, seq, tm=128):
    n, wp = proj.shape
    width = w0.shape[-1]
    tm = min(tm, seq)
    row = lambda i: (i, 0)
    fix = lambda i: (0, 0)
    vec = pl.BlockSpec((1, width), fix)
    out = pl.BlockSpec((tm, width), row)
    per = tm // SUBLANES
    return pl.pallas_call(
        functools.partial(_rwkv_prep_kernel, seq=seq, width=width),
        grid=(n // tm,),
        in_specs=[pl.BlockSpec((tm, wp), row),
                  pl.BlockSpec((SUBLANES, wp), lambda i: (jnp.maximum(i * per - 1, 0), 0)),
                  pl.BlockSpec((1, wp), fix), vec, pl.BlockSpec(w2.shape, fix), vec,
                  pl.BlockSpec(a2.shape, fix), pl.BlockSpec(g2.shape, fix), vec, vec, vec],
        out_specs=[out] * 8,
        out_shape=[jax.ShapeDtypeStruct((n, width), F32)] * 8,
        compiler_params=_cparams("parallel"), name="rwkv_prep",
    )(proj, proj, mu, w0, w2, a0, a2, g2, k_k, k_a, r_k)


SCAN_PASSES = dict(gram=1, gkv=1, inv=1, t=1, r=1, y=1, m=3, s=3)


def _pdot(a, b, dims, passes):
    if passes == 1:
        return _dot(a.astype(BF16), b.astype(BF16), dims)
    return _dot3(a, b, dims)


def _rwkv_scan_kernel(r_ref, lw_ref, k_ref, v_ref, a_ref, b_ref, y_ref, st_ref, *, chunk, wave):
    c_len = chunk
    hd = HEAD_DIM
    nheads = LANES // hd
    ps = SCAN_PASSES

    @pl.when(pl.program_id(2) == 0)
    def _():
        st_ref[...] = jnp.zeros_like(st_ref)

    nchunks = r_ref.shape[0] // c_len
    row2 = lax.broadcasted_iota(jnp.int32, (2 * c_len, c_len), 0)
    col2 = lax.broadcasted_iota(jnp.int32, (2 * c_len, c_len), 1)
    mask2 = col2 <= jnp.where(row2 < c_len, row2 - 1, row2 - c_len)
    row = lax.broadcasted_iota(jnp.int32, (c_len, c_len), 0)
    col = lax.broadcasted_iota(jnp.int32, (c_len, c_len), 1)
    eye_c = (row == col).astype(F32)
    eye_k = (lax.broadcasted_iota(jnp.int32, (hd, hd), 0)
             == lax.broadcasted_iota(jnp.int32, (hd, hd), 1)).astype(F32)
    ridx = lax.broadcasted_iota(jnp.int32, (c_len, LANES), 0)

    def load_units(chunks):
        units = []
        for c in chunks:
            rows = slice(c * c_len, (c + 1) * c_len)
            lw = lw_ref[rows, :]
            cum = lw
            sh = 1
            while sh < c_len:
                cum = cum + jnp.where(ridx >= sh, pltpu.roll(cum, sh, axis=0), 0.0)
                sh *= 2
            total = cum[c_len - 1:c_len, :]
            e_neg = jnp.exp(-cum)
            e_end = jnp.exp(total - cum)
            rt = r_ref[rows, :] * jnp.exp(cum)
            at = a_ref[rows, :] * jnp.exp(cum - lw)
            b = b_ref[rows, :]
            k = k_ref[rows, :]
            bt, kt, bh, kh = b * e_neg, k * e_neg, b * e_end, k * e_end
            v = v_ref[rows, :]
            g_end = jnp.exp(total)
            for h in range(nheads):
                sl = slice(h * hd, (h + 1) * hd)
                units.append(dict(c=c, h=h, at=at[:, sl], rt=rt[:, sl], v=v[:, sl], bt=bt[:, sl], kt=kt[:, sl],
                                  bh=bh[:, sl], kh=kh[:, sl], g=g_end[:, sl]))
        return units

    def phase1(units):
        for u in units:
            u["l2"] = jnp.concatenate([u["at"], u["rt"]], axis=0)
            u["gb"] = jnp.where(mask2, _pdot(u["l2"], u["bt"], NT, ps["gram"]), 0.0)
        yield
        for u in units:
            u["gk"] = jnp.where(mask2, _pdot(u["l2"], u["kt"], NT, ps["gram"]), 0.0)
            u["inv"] = eye_c + u["gb"][:c_len]
            u["lp"] = u["gb"][:c_len]
        yield
        for u in units:
            u["gkv"] = _pdot(u["gk"], u["v"], NN, ps["gkv"])
        yield
        p = 2
        while p < c_len:
            for u in units:
                u["lp"] = _pdot(u["lp"], u["lp"], NN, ps["inv"])
            yield
            for u in units:
                u["inv"] = u["inv"] + _pdot(u["inv"], u["lp"], NN, ps["inv"])
            yield
            p *= 2
        for u in units:
            u["ta"] = _pdot(u["inv"], u["at"], NN, ps["t"])
        yield
        for u in units:
            u["u0"] = _pdot(u["inv"], u["gkv"][:c_len], NN, ps["t"])
        yield
        for u in units:
            u["rhat"] = u["rt"] + _pdot(u["gb"][c_len:], u["ta"], NN, ps["r"])
        yield
        for u in units:
            u["y1"] = _pdot(u["gb"][c_len:], u["u0"], NN, ps["r"]) + u["gkv"][c_len:]
        yield
        for u in units:
            u["m"] = eye_k * u["g"] + _pdot(u["bh"], u["ta"], TN, ps["m"])
        yield
        for u in units:
            u["nm"] = _pdot(jnp.concatenate([u["bh"], u["kh"]], axis=0),
                            jnp.concatenate([u["u0"], u["v"]], axis=0), TN, ps["m"])
        yield

    state = [st_ref[h] for h in range(nheads)]

    def phase2_steps(units):
        by_chunk = {}
        for u in units:
            by_chunk.setdefault(u["c"], []).append(u)

        def step(c):
            ys = []
            for u in by_chunk[c]:
                h = u["h"]
                ys.append(_pdot(u["rhat"], state[h], NN, ps["y"]) + u["y1"])
                state[h] = _pdot(u["m"], state[h], NN, ps["s"]) + u["nm"]
            y_ref[c * c_len:(c + 1) * c_len, :] = jnp.concatenate(ys, axis=1)

        return [functools.partial(step, c) for c in sorted(by_chunk)]

    pending = []
    for w0 in range(0, nchunks, wave):
        units = load_units(range(w0, min(w0 + wave, nchunks)))
        for _ in phase1(units):
            if pending:
                pending.pop(0)()
        for step in pending:
            step()
        pending = phase2_steps(units)
    for step in pending:
        step()
    for h in range(nheads):
        st_ref[h] = state[h]


def rwkv_scan(r, lw, k, v, a, b, batch, seq, tblk=RWKV_TBLK):
    n, width = r.shape
    tblk = min(tblk, seq)
    nt = seq // tblk
    spec = pl.BlockSpec((tblk, LANES), lambda bi, hp, tb: (bi * nt + tb, hp))
    return pl.pallas_call(
        functools.partial(_rwkv_scan_kernel, chunk=RWKV_CHUNK, wave=RWKV_WAVE),
        grid=(batch, width // LANES, nt),
        in_specs=[spec] * 6,
        out_specs=spec,
        out_shape=jax.ShapeDtypeStruct((n, width), F32),
        scratch_shapes=[pltpu.VMEM((LANES // HEAD_DIM, HEAD_DIM, HEAD_DIM), F32)],
        compiler_params=_cparams("parallel", "parallel", "arbitrary"), name="rwkv_scan",
    )(r, lw, k, v, a, b)


def _rwkv_post_kernel(y_ref, bonus_ref, g_ref, lg_ref, lb_ref, o_ref):
    seg = _seg_matrix()
    y = y_ref[...]
    inv_n = 1.0 / HEAD_DIM
    mean = _head_sum(y, seg) * inv_n
    yc = y - mean
    var = _head_sum(yc * yc, seg) * inv_n
    yn = yc * lax.rsqrt(var + RWKV_LN_EPS) * lg_ref[...] + lb_ref[...]
    o_ref[...] = ((yn + bonus_ref[...]) * g_ref[...]).astype(o_ref.dtype)


def rwkv_post(y, bonus, g, lnx_g, lnx_b, tm=256):
    n, width = y.shape
    tm = _pick(n, (tm, 128))
    row = pl.BlockSpec((tm, width), lambda i: (i, 0))
    vec = pl.BlockSpec((1, width), lambda i: (0, 0))
    return pl.pallas_call(
        _rwkv_post_kernel, grid=(n // tm,),
        in_specs=[row, row, row, vec, vec], out_specs=row,
        out_shape=jax.ShapeDtypeStruct((n, width), BF16),
        compiler_params=_cparams("parallel"), name="rwkv_post",
    )(y, bonus, g, lnx_g, lnx_b)


def _t5_causal_bucket(dist):
    max_exact = REL_BUCKETS // 2
    d_f = jnp.maximum(dist, 1).astype(F32)
    large = max_exact + (jnp.log(d_f / max_exact) / math.log(REL_MAX_DIST / max_exact)
                         * (REL_BUCKETS - max_exact)).astype(jnp.int32)
    large = jnp.minimum(large, REL_BUCKETS - 1)
    return jnp.where(dist < max_exact, dist, large)


def _pattern_bias(rel_bias_g, window, dil):
    span = window // dil
    qi = np.arange(ATTN_BLOCK)[:, None]
    ki = np.arange(2 * ATTN_BLOCK)[None, :]
    off = qi + ATTN_BLOCK - ki
    band = (off >= 0) & (off <= span)
    bucket = _t5_causal_bucket(jnp.asarray(np.clip(off, 0, span) * dil, jnp.int32))
    bias = jnp.transpose(rel_bias_g[bucket], (2, 0, 1)).astype(F32)
    return jnp.where(jnp.asarray(band)[None], bias, NEG_INF)


def _attn_blocks(blocks, bias_ref, qg, kg, no_prev):
    hd = HEAD_DIM
    scale = hd ** -0.5

    def norm(x, g):
        ms = jnp.mean(x * x, axis=-1, keepdims=True)
        return x * lax.rsqrt(ms + RMS_EPS) * g

    units = []
    for bi, (q, kp, kc, vp, vc) in enumerate(blocks):
        for h in range(q.shape[1] // hd):
            sl = slice(h * hd, (h + 1) * hd)
            units.append(dict(b=bi, h=h, q=q[:, sl], kp=kp[:, sl], kc=kc[:, sl], vp=vp[:, sl], vc=vc[:, sl]))
    for u in units:
        u["qn"] = norm(u["q"], qg).astype(BF16)
        u["kn"] = jnp.concatenate([norm(u["kp"], kg), norm(u["kc"], kg)], axis=0).astype(BF16)
        u["vb"] = jnp.concatenate([u["vp"], u["vc"]], axis=0).astype(BF16)
    for u in units:
        s = _dot(u["qn"], u["kn"], NT) * scale + bias_ref[u["h"]]
        u["s"] = jnp.where(no_prev, NEG_INF, s)
    for u in units:
        u["m"] = jnp.max(u["s"], axis=-1, keepdims=True)
    for u in units:
        u["p"] = jnp.exp(u["s"] - u["m"])
        u["den"] = jnp.sum(u["p"], axis=-1, keepdims=True)
    for u in units:
        u["o"] = _dot(u["p"].astype(BF16), u["vb"]) / u["den"]
        u["lse"] = jnp.broadcast_to(u["m"] + jnp.log(u["den"]), (ATTN_BLOCK, hd))
    res = []
    for bi in range(len(blocks)):
        mine = [u for u in units if u["b"] == bi]
        res.append((jnp.concatenate([u["o"] for u in mine], axis=1),
                    jnp.concatenate([u["lse"] for u in mine], axis=1)))
    return res


def _no_prev_mask():
    ki = lax.broadcasted_iota(jnp.int32, (ATTN_BLOCK, 2 * ATTN_BLOCK), 1)
    return jnp.logical_and(pl.program_id(1) == 0, ki < ATTN_BLOCK)


def _attn_kernel(q_ref, kp_ref, kc_ref, vp_ref, vc_ref, bias_ref, qg_ref, kg_ref, o_ref, lse_ref):
    (o, lse), = _attn_blocks([(q_ref[...], kp_ref[...], kc_ref[...], vp_ref[...], vc_ref[...])], bias_ref,
                             qg_ref[...], kg_ref[...], _no_prev_mask())
    o_ref[...] = o
    lse_ref[...] = lse


def _attn_dilated_kernel(q_ref, kp_ref, kc_ref, vp_ref, vc_ref, bias_ref, qg_ref, kg_ref, o_ref, lse_ref, *, dil):
    no_prev = _no_prev_mask()
    group = min(dil, ATTN_GROUP)

    def body(rg, carry):
        rows = [pl.ds(rg * group + j, ATTN_BLOCK, stride=dil) for j in range(group)]
        blocks = [(q_ref[rw, :], kp_ref[rw, :], kc_ref[rw, :], vp_ref[rw, :], vc_ref[rw, :]) for rw in rows]
        for rw, (o, lse) in zip(rows, _attn_blocks(blocks, bias_ref, qg_ref[...], kg_ref[...], no_prev)):
            o_ref[rw, :] = o
            lse_ref[rw, :] = lse
        return carry

    lax.fori_loop(0, dil // group, body, 0)


def attn_dilated(proj, bias, q_g, k_g, gi, dil, batch, seq, width):
    n = proj.shape[0]
    npat = len(DILATED_PATTERNS)
    pw = width // npat
    rows = ATTN_BLOCK * dil
    nb = seq // rows
    sec = width // LANES
    hp_per = pw // LANES
    heads_pp = LANES // HEAD_DIM

    def spec(section, prev):
        def imap(b, nblk, hp):
            blk = jnp.maximum(nblk - 1, 0) if prev else nblk
            return (b * nb + blk, section * sec + gi * hp_per + hp)
        return pl.BlockSpec((rows, LANES), imap)

    bias_spec = pl.BlockSpec((heads_pp,) + bias.shape[1:], lambda b, nblk, hp: (hp, 0, 0))
    vec = pl.BlockSpec((1, HEAD_DIM), lambda b, nblk, hp: (0, 0))
    ospec = pl.BlockSpec((rows, LANES), lambda b, nblk, hp: (b * nb + nblk, hp))
    return pl.pallas_call(
        functools.partial(_attn_dilated_kernel, dil=dil), grid=(batch, nb, hp_per),
        in_specs=[spec(0, False), spec(1, True), spec(1, False), spec(2, True), spec(2, False), bias_spec, vec, vec],
        out_specs=[ospec, ospec],
        out_shape=[jax.ShapeDtypeStruct((n, pw), F32)] * 2,
        compiler_params=_cparams("parallel", "parallel", "parallel"), name=f"attn_d{dil}",
    )(proj, proj, proj, proj, proj, bias, q_g, k_g)


def attn_pattern(proj, bias, q_g, k_g, gi, dil, batch, seq, width):
    n = proj.shape[0]
    npat = len(DILATED_PATTERNS)
    pw = width // npat
    nsub = seq // dil
    nb = nsub // ATTN_BLOCK
    sec = width // pw
    rowlen = 3 * sec
    x = proj.reshape(batch * nsub, dil * 3 * width)

    def spec(section, prev):
        def imap(b, nblk, r):
            blk = jnp.maximum(nblk - 1, 0) if prev else nblk
            return (b * nb + blk, r * rowlen + section * sec + gi)
        return pl.BlockSpec((ATTN_BLOCK, pw), imap)

    fix3 = pl.BlockSpec(bias.shape, lambda b, nblk, r: (0, 0, 0))
    vec = pl.BlockSpec((1, HEAD_DIM), lambda b, nblk, r: (0, 0))
    ospec = pl.BlockSpec((ATTN_BLOCK, pw), lambda b, nblk, r: (b * nb + nblk, r))
    o, lse = pl.pallas_call(
        _attn_kernel, grid=(batch, nb, dil),
        in_specs=[spec(0, False), spec(1, True), spec(1, False), spec(2, True), spec(2, False), fix3, vec, vec],
        out_specs=[ospec, ospec],
        out_shape=[jax.ShapeDtypeStruct((batch * nsub, dil * pw), F32)] * 2,
        compiler_params=_cparams("parallel", "parallel", "parallel"), name=f"attn_d{dil}",
    )(x, x, x, x, x, bias, q_g, k_g)
    return o.reshape(n, pw), lse.reshape(n, pw)


def _attn_mix_kernel(o0, o1, o2, l0, l1, l2, out_ref):
    a, b, c = l0[...], l1[...], l2[...]
    m = jnp.maximum(jnp.maximum(a, b), c)
    ea, eb, ec = jnp.exp(a - m), jnp.exp(b - m), jnp.exp(c - m)
    inv = 1.0 / (ea + eb + ec)
    out_ref[...] = jnp.concatenate([o0[...] * (ea * inv), o1[...] * (eb * inv), o2[...] * (ec * inv)],
                                   axis=1).astype(out_ref.dtype)


def attn_mix(outs, lses, tm=512):
    n, pw = outs[0].shape
    tm = _pick(n, (tm, 256, 128))
    row = pl.BlockSpec((tm, pw), lambda i: (i, 0))
    return pl.pallas_call(
        _attn_mix_kernel, grid=(n // tm,), in_specs=[row] * 6,
        out_specs=pl.BlockSpec((tm, 3 * pw), lambda i: (i, 0)),
        out_shape=jax.ShapeDtypeStruct((n, 3 * pw), BF16),
        compiler_params=_cparams("parallel"), name="attn_mix",
    )(*outs, *lses)


def _conv_kernel(x_ref, halo_ref, w_ref, b_ref, lg_ref, lb_ref, o_ref, ext_ref, *, seq, width):
    t = x_ref.shape[0]
    i = pl.program_id(0)
    seq_start = (i * t) % seq == 0

    def glu(x):
        return x[:, :width] * _sigmoid(x[:, width:])

    ext_ref[0:CONV_HALO, :] = jnp.where(seq_start, 0.0, glu(halo_ref[...]))
    ext_ref[CONV_HALO:, :] = glu(x_ref[...])
    base = CONV_HALO - (CONV_TAPS - 1)
    acc = jnp.broadcast_to(b_ref[...], (t, width))
    for j in range(CONV_TAPS):
        acc = acc + w_ref[j:j + 1, :] * ext_ref[base + j:base + j + t, :]
    mu = jnp.mean(acc, axis=-1, keepdims=True)
    d = acc - mu
    var = jnp.mean(d * d, axis=-1, keepdims=True)
    y = d * lax.rsqrt(var + LN_EPS) * lg_ref[...] + lb_ref[...]
    o_ref[...] = (y * _sigmoid(y)).astype(o_ref.dtype)


def conv_module(proj, conv_w, conv_b, ln_g, ln_b, seq, tm=256):
    n, w2 = proj.shape
    width = w2 // 2
    tm = min(tm, seq)
    per = tm // CONV_HALO
    fix = lambda i: (0, 0)
    vec = pl.BlockSpec((1, width), fix)
    return pl.pallas_call(
        functools.partial(_conv_kernel, seq=seq, width=width), grid=(n // tm,),
        in_specs=[pl.BlockSpec((tm, w2), lambda i: (i, 0)),
                  pl.BlockSpec((CONV_HALO, w2), lambda i: (jnp.maximum(i * per - 1, 0), 0)),
                  pl.BlockSpec((CONV_TAPS, width), fix), vec, vec, vec],
        out_specs=pl.BlockSpec((tm, width), lambda i: (i, 0)),
        out_shape=jax.ShapeDtypeStruct((n, width), BF16),
        scratch_shapes=[pltpu.VMEM((tm + CONV_HALO, width), F32)],
        compiler_params=_cparams("parallel"), name="conv_module",
    )(proj, proj, conv_w, conv_b, ln_g, ln_b)


ROUTER_LANES = LANES


def _route_kernel(x_ref, g_ref, rw_ref, bias_ref, info_ref, cnt_ref, run_ref):
    t = x_ref.shape[0]

    @pl.when(pl.program_id(0) == 0)
    def _():
        run_ref[...] = jnp.zeros_like(run_ref)

    x = x_ref[...]
    ms = jnp.mean(x * x, axis=-1, keepdims=True)
    z = _dot3(x * lax.rsqrt(ms + RMS_EPS) * g_ref[...], rw_ref[...]) + bias_ref[...]
    lane = lax.broadcasted_iota(jnp.int32, z.shape, 1).astype(F32)
    ninf = -jnp.inf
    big = float(ROUTER_LANES)
    zc = jnp.where(lane < N_GROUPS, z, ninf)
    mc = jnp.max(zc, axis=-1, keepdims=True)
    pg = 1.0 / jnp.sum(jnp.exp(zc - mc), axis=-1, keepdims=True)
    g_idx = jnp.min(jnp.where(zc == mc, lane, big), axis=-1, keepdims=True)
    lo = N_GROUPS + g_idx * EXPERTS_PER_GROUP
    zf = jnp.where(jnp.logical_and(lane >= lo, lane < lo + EXPERTS_PER_GROUP), z, ninf)
    m1 = jnp.max(zf, axis=-1, keepdims=True)
    i1 = jnp.min(jnp.where(zf == m1, lane, big), axis=-1, keepdims=True)
    zf2 = jnp.where(lane == i1, ninf, zf)
    m2 = jnp.max(zf2, axis=-1, keepdims=True)
    i2 = jnp.min(jnp.where(zf2 == m2, lane, big), axis=-1, keepdims=True)
    e2 = jnp.exp(m2 - m1)
    w1 = pg / (1.0 + e2)
    w2 = pg * e2 / (1.0 + e2)
    onehot = jnp.where(jnp.logical_or(lane == i1, lane == i2), 1.0, 0.0)
    tri = (lax.broadcasted_iota(jnp.int32, (t, t), 0) > lax.broadcasted_iota(jnp.int32, (t, t), 1)).astype(BF16)
    before = _dot(tri, onehot.astype(BF16)) + run_ref[...]
    r1 = jnp.sum(jnp.where(lane == i1, before, 0.0), axis=-1, keepdims=True)
    r2 = jnp.sum(jnp.where(lane == i2, before, 0.0), axis=-1, keepdims=True)
    run_ref[...] += jnp.sum(onehot, axis=0, keepdims=True)
    cnt_ref[...] = run_ref[...]
    cols = (i1 - N_GROUPS, i2 - N_GROUPS, w1, w2, r1, r2)
    info = jnp.zeros_like(z)
    for j, c in enumerate(cols):
        info = jnp.where(lane == float(j), c, info)
    info_ref[...] = info


def route(x, g, router_w, bias, tm=256):
    n, d = x.shape
    w = router_w.shape[1]
    tm = _pick(n, (tm, 128))
    fix = lambda i: (0, 0)
    row = pl.BlockSpec((tm, w), lambda i: (i, 0))
    one = pl.BlockSpec((1, w), fix)
    return pl.pallas_call(
        _route_kernel, grid=(n // tm,),
        in_specs=[pl.BlockSpec((tm, d), lambda i: (i, 0)), pl.BlockSpec((1, d), fix), pl.BlockSpec((d, w), fix), one],
        out_specs=[row, one],
        out_shape=[jax.ShapeDtypeStruct((n, w), F32), jax.ShapeDtypeStruct((1, w), F32)],
        scratch_shapes=[pltpu.VMEM((1, w), F32)],
        compiler_params=_cparams("arbitrary"), name="route",
    )(x, g.reshape(1, d), router_w, bias)


MOE_TILE = 256


def moe_plan(info, counts, n_tiles):
    cnt = counts[0, N_GROUPS:N_GROUPS + N_EXPERTS].astype(jnp.int32)
    padded = (cnt + MOE_TILE - 1) // MOE_TILE * MOE_TILE
    ends = jnp.cumsum(padded)
    off = ends - padded
    ii = info[:, :6].astype(jnp.int32)
    d1 = off[ii[:, 0]] + ii[:, 4]
    d2 = off[ii[:, 1]] + ii[:, 5]
    tile_expert = jnp.minimum(jnp.searchsorted(ends, jnp.arange(n_tiles, dtype=jnp.int32) * MOE_TILE, side="right"),
                              N_EXPERTS - 1).astype(jnp.int32)
    n_used = (ends[-1:] // MOE_TILE).astype(jnp.int32)
    return d1, d2, tile_expert, n_used


def _moe_ffn_kernel(d1_ref, d2_ref, texp_ref, nused_ref, h_hbm, g_ref, wi_ref, wo_ref, ys_ref,
                    src_ref, xbuf, sem, *, n_tok, ff):
    i = pl.program_id(0)
    n_used = nused_ref[0]
    n_rows = src_ref.shape[0]

    def row_copy(tok, slot, j):
        return pltpu.make_async_copy(h_hbm.at[pl.ds(tok, 1), :], xbuf.at[slot, pl.ds(j, 1), :], sem.at[slot])

    def gather(tile, slot):
        def body(j, carry):
            row_copy(src_ref[tile * MOE_TILE + j], slot, j).start()
            return carry
        lax.fori_loop(0, MOE_TILE, body, 0)

    @pl.when(i == 0)
    def _():
        def clear(j, carry):
            src_ref[j] = 0
            return carry
        lax.fori_loop(0, n_rows, clear, 0)

        def fill(tok, carry):
            src_ref[d1_ref[tok]] = tok
            src_ref[d2_ref[tok]] = tok
            return carry
        lax.fori_loop(0, n_tok, fill, 0)
        gather(0, 0)

    @pl.when(i < n_used)
    def _():
        slot = i % 2

        @pl.when(i + 1 < n_used)
        def _():
            gather(i + 1, 1 - slot)

        pltpu.make_async_copy(h_hbm.at[pl.ds(0, MOE_TILE), :], xbuf.at[slot], sem.at[slot]).wait()
        x = xbuf[slot]
        ms = jnp.mean(x * x, axis=-1, keepdims=True)
        xn = (x * lax.rsqrt(ms + RMS_EPS) * g_ref[...]).astype(BF16)
        gu = _dot(xn, wi_ref[0])
        gt, up = gu[:, :ff], gu[:, ff:]
        hid = gt * _sigmoid(gt) * up
        ys_ref[...] = _dot(hid.astype(BF16), wo_ref[0])

    @pl.when(i >= n_used)
    def _():
        ys_ref[...] = jnp.zeros_like(ys_ref)


def moe_ffn(h, g, w_in, w_out, d1, d2, tile_expert, n_used):
    n, d = h.shape
    ne, _, ff2 = w_in.shape
    n_tiles = tile_expert.shape[0]
    n_rows = n_tiles * MOE_TILE
    last = lambda i, nu: jnp.minimum(i, nu[0] - 1)
    grid_spec = pltpu.PrefetchScalarGridSpec(
        num_scalar_prefetch=4, grid=(n_tiles,),
        in_specs=[pl.BlockSpec(memory_space=pl.ANY),
                  pl.BlockSpec((1, d), lambda i, d1, d2, te, nu: (0, 0)),
                  pl.BlockSpec((1, d, ff2), lambda i, d1, d2, te, nu: (te[last(i, nu)], 0, 0)),
                  pl.BlockSpec((1, ff2 // 2, d), lambda i, d1, d2, te, nu: (te[last(i, nu)], 0, 0))],
        out_specs=pl.BlockSpec((MOE_TILE, d), lambda i, d1, d2, te, nu: (i, 0)),
        scratch_shapes=[pltpu.SMEM((n_rows,), jnp.int32), pltpu.VMEM((2, MOE_TILE, d), F32),
                        pltpu.SemaphoreType.DMA((2,))])
    return pl.pallas_call(
        functools.partial(_moe_ffn_kernel, n_tok=n, ff=ff2 // 2), grid_spec=grid_spec,
        out_shape=jax.ShapeDtypeStruct((n_rows, d), F32),
        compiler_params=_cparams("arbitrary"), name="moe_ffn",
    )(d1, d2, tile_expert, n_used, h, g.reshape(1, d), w_in, w_out)


def _moe_combine_kernel(d1_ref, d2_ref, ys_hbm, h_ref, info_ref, g_ref, o_ref, xn_ref, ybuf, sem):
    t = h_ref.shape[0]
    base = pl.program_id(0) * t

    def issue(j, carry):
        pltpu.make_async_copy(ys_hbm.at[pl.ds(d1_ref[base + j], 1), :], ybuf.at[0, pl.ds(j, 1), :], sem.at[0]).start()
        pltpu.make_async_copy(ys_hbm.at[pl.ds(d2_ref[base + j], 1), :], ybuf.at[1, pl.ds(j, 1), :], sem.at[1]).start()
        return carry
    lax.fori_loop(0, t, issue, 0)
    for s in range(2):
        pltpu.make_async_copy(ys_hbm.at[pl.ds(0, t), :], ybuf.at[s], sem.at[s]).wait()
    info = info_ref[...]
    h = h_ref[...] + info[:, 2:3] * ybuf[0] + info[:, 3:4] * ybuf[1]
    o_ref[...] = h
    ms = jnp.mean(h * h, axis=-1, keepdims=True)
    xn_ref[...] = (h * lax.rsqrt(ms + RMS_EPS) * g_ref[...]).astype(xn_ref.dtype)


def moe_combine(ys, h, info, g_next, d1, d2, tm=256):
    n, d = h.shape
    tm = _pick(n, (tm, 128))
    row = lambda i, d1, d2: (i, 0)
    tile = pl.BlockSpec((tm, d), row)
    grid_spec = pltpu.PrefetchScalarGridSpec(
        num_scalar_prefetch=2, grid=(n // tm,),
        in_specs=[pl.BlockSpec(memory_space=pl.ANY), tile, pl.BlockSpec((tm, info.shape[1]), row),
                  pl.BlockSpec((1, d), lambda i, d1, d2: (0, 0))],
        out_specs=[tile, tile],
        scratch_shapes=[pltpu.VMEM((2, tm, d), F32), pltpu.SemaphoreType.DMA((2,))])
    return pl.pallas_call(
        _moe_combine_kernel, grid_spec=grid_spec,
        out_shape=[jax.ShapeDtypeStruct((n, d), F32), jax.ShapeDtypeStruct((n, d), BF16)],
        compiler_params=_cparams("arbitrary"), name="moe_combine",
    )(d1, d2, ys, h, info, g_next.reshape(1, d))


def _ple_kernel(a_ref, w_ref, p_ref, pw_ref, h_ref, o_ref):
    gate = _sigmoid(_dot(a_ref[...], w_ref[...]))
    o_ref[...] = h_ref[...] + _dot(p_ref[...], pw_ref[...]) * gate


def ple(xn, gate_w, p, proj_w, h, tm=1024, tn=512):
    n, d = xn.shape
    pd = p.shape[1]
    tm = _pick(n, (tm, 512, 256, 128))
    tn = _pick(d, (tn, 256, 128))
    tile = pl.BlockSpec((tm, tn), lambda i, j: (i, j))
    return pl.pallas_call(
        _ple_kernel, grid=(n // tm, d // tn),
        in_specs=[pl.BlockSpec((tm, d), lambda i, j: (i, 0)), pl.BlockSpec((d, tn), lambda i, j: (0, j)),
                  pl.BlockSpec((tm, pd), lambda i, j: (i, 0)), pl.BlockSpec((pd, tn), lambda i, j: (0, j)), tile],
        out_specs=tile,
        out_shape=jax.ShapeDtypeStruct((n, d), F32),
        compiler_params=_cparams("parallel", "arbitrary"), name="ple",
    )(xn, gate_w, p, proj_w, h)


def _pad_cols(x, total):
    return jnp.pad(x, ((0, 0), (0, total - x.shape[1])))


def _pad_rows(x, total):
    return jnp.pad(x, ((0, total - x.shape[0]), (0, 0)))


def _pack_rwkv_cols(x, width, lora):
    dl, al, gl = lora
    o = 3 * width
    return jnp.concatenate([
        x[:, :o], _pad_cols(x[:, o:o + dl], LANES), _pad_cols(x[:, o + dl:o + dl + al], LANES),
        _pad_cols(x[:, o + dl + al:o + dl + al + gl], 2 * LANES)], axis=1)


def kernel(x, p, norm_mix_g, w_in, rwkv_mu, rwkv_w0, rwkv_w2, rwkv_a0, rwkv_a2, rwkv_g2, rwkv_k_k, rwkv_k_a, rwkv_r_k, rwkv_lnx_g, rwkv_lnx_b, q_norm_g, k_norm_g, rel_bias, conv_w, conv_b, conv_ln_g, conv_ln_b, w_out, norm_ffn_g, router_c_w, router_c_b, router_f_w, router_f_b, expert_w_in, expert_w_out, ple_norm_g, ple_gate_w, ple_proj):
    batch, seq, d = x.shape
    depth = w_in.shape[0]
    n = batch * seq
    rw = rwkv_w0.shape[-1]
    lora = (rwkv_w2.shape[1], rwkv_a2.shape[1], rwkv_g2.shape[1])
    assert max(lora[0], lora[1]) <= LANES and lora[2] <= 2 * LANES
    rwkv_proj = 3 * rw + sum(lora)
    cw = conv_w.shape[-1]
    aw = d - rw - cw
    npat = len(DILATED_PATTERNS)
    hpp = aw // HEAD_DIM // npat
    row = lambda v: v.reshape(1, -1)

    biases = [_pattern_bias(rel_bias[:, gi * hpp:(gi + 1) * hpp], window, dil)
              for gi, (window, dil) in enumerate(DILATED_PATTERNS)]

    h = x.reshape(n, d)
    for i in range(depth):
        w_i = w_in[i]
        w_rwkv = _pack_rwkv_cols(w_i[:, :rwkv_proj], rw, lora).astype(BF16)
        w_attn = w_i[:, rwkv_proj:rwkv_proj + 3 * aw].astype(BF16)
        w_conv = w_i[:, rwkv_proj + 3 * aw:].astype(BF16)
        mu = _pack_rwkv_cols(row(rwkv_mu[i]), rw, lora)
        w2 = _pad_rows(rwkv_w2[i], LANES)
        a2 = _pad_rows(rwkv_a2[i], LANES)
        g2 = _pad_rows(rwkv_g2[i], 2 * LANES)

        xn = rmsnorm_cast(h, norm_mix_g[i])
        p_rwkv = matmul(xn, w_rwkv, name="proj_rwkv")
        p_attn = matmul(xn, w_attn, name="proj_attn")
        p_conv = matmul(xn, w_conv, name="proj_conv")

        r, lw, k, v, a, b, g, bonus = rwkv_prep(
            p_rwkv, mu, row(rwkv_w0[i]), w2, row(rwkv_a0[i]), a2, g2, row(rwkv_k_k[i]), row(rwkv_k_a[i]),
            row(rwkv_r_k[i]), seq)
        y = rwkv_scan(r, lw, k, v, a, b, batch, seq)
        y_rwkv = rwkv_post(y, bonus, g, row(rwkv_lnx_g[i]), row(rwkv_lnx_b[i]))

        outs, lses = [], []
        for gi, (window, dil) in enumerate(DILATED_PATTERNS):
            attn = attn_pattern if dil == 1 else attn_dilated
            o, lse = attn(p_attn, biases[gi], row(q_norm_g[i]), row(k_norm_g[i]), gi, dil, batch, seq, aw)
            outs.append(o)
            lses.append(lse)
        y_attn = attn_mix(outs, lses)

        y_conv = conv_module(p_conv, conv_w[i], row(conv_b[i]), row(conv_ln_g[i]), row(conv_ln_b[i]), seq)

        mix = jnp.concatenate([y_rwkv, y_attn, y_conv], axis=1)
        h = matmul(mix, w_out[i].astype(BF16), residual=h, name="out_proj")

        router_w = _pad_cols(jnp.concatenate([router_c_w[i], router_f_w[i]], axis=1), ROUTER_LANES)
        router_b = _pad_cols(row(jnp.concatenate([router_c_b[i], router_f_b[i]])), ROUTER_LANES)
        info, counts = route(h, norm_ffn_g[i], router_w, router_b)
        n_tiles = (2 * n) // MOE_TILE + N_EXPERTS
        d1, d2, tile_expert, n_used = moe_plan(info, counts, n_tiles)
        ys = moe_ffn(h, norm_ffn_g[i], expert_w_in[i].astype(BF16), expert_w_out[i].astype(BF16),
                     d1, d2, tile_expert, n_used)
        h, xn = moe_combine(ys, h, info, ple_norm_g[i], d1, d2)

        h = ple(xn, ple_gate_w[i].astype(BF16), p[i].reshape(n, -1).astype(BF16), ple_proj[i].astype(BF16), h)
    return h.reshape(batch, seq, d)
```

```python
import functools
import math

import numpy as np
import jax
import jax.numpy as jnp
from jax import lax
from jax.experimental import pallas as pl
from jax.experimental.pallas import tpu as pltpu

F32 = jnp.float32
BF16 = jnp.bfloat16

HEAD_DIM = 64
DILATED_PATTERNS = ((128, 1), (512, 4), (2048, 16))
ATTN_BLOCK = 128
REL_BUCKETS = 32
REL_MAX_DIST = 2048
NEG_INF = -1e30
CONV_TAPS = 31
N_GROUPS = 8
EXPERTS_PER_GROUP = 8
N_EXPERTS = N_GROUPS * EXPERTS_PER_GROUP
RMS_EPS = 1e-6
LN_EPS = 1e-5
RWKV_LN_EPS = 64e-5

LANES = 128
SUBLANES = 8
VMEM_LIMIT_BYTES = 56 * 1024 * 1024

RWKV_CHUNK = 64
RWKV_WAVE = 8
RWKV_TBLK = 1024
ATTN_GROUP = 4
CONV_HALO = 32

NN = (((1,), (0,)), ((), ()))
NT = (((1,), (1,)), ((), ()))
TN = (((0,), (0,)), ((), ()))


def _cparams(*sem):
    return pltpu.CompilerParams(dimension_semantics=sem, vmem_limit_bytes=VMEM_LIMIT_BYTES)


def _dot(a, b, dims=NN):
    return lax.dot_general(a, b, dims, preferred_element_type=F32)


def _split(x):
    hi = x.astype(BF16)
    lo = (x - hi.astype(F32)).astype(BF16)
    return hi, lo


def _dot3(a, b, dims=NN):
    ah, al = _split(a)
    bh, bl = _split(b)
    return _dot(ah, bh, dims) + (_dot(ah, bl, dims) + _dot(al, bh, dims))


def _dot2(a, b_bf16, dims=NN):
    ah, al = _split(a)
    return _dot(ah, b_bf16, dims) + _dot(al, b_bf16, dims)


def _sigmoid(x):
    return 1.0 / (1.0 + jnp.exp(-x))


def _seg_matrix():
    r = lax.broadcasted_iota(jnp.int32, (LANES, LANES), 0) // HEAD_DIM
    c = lax.broadcasted_iota(jnp.int32, (LANES, LANES), 1) // HEAD_DIM
    return (r == c).astype(BF16)


def _head_sum(x, seg):
    w = x.shape[1]
    cols = [_dot2(x[:, j:j + LANES], seg) for j in range(0, w, LANES)]
    return cols[0] if len(cols) == 1 else jnp.concatenate(cols, axis=1)


def _rmsnorm_kernel(x_ref, g_ref, o_ref):
    x = x_ref[...]
    ms = jnp.mean(x * x, axis=-1, keepdims=True)
    o_ref[...] = (x * lax.rsqrt(ms + RMS_EPS) * g_ref[...]).astype(o_ref.dtype)


def rmsnorm_cast(x, g, tm=512):
    n, d = x.shape
    tm = min(tm, n)
    return pl.pallas_call(
        _rmsnorm_kernel,
        grid=(n // tm,),
        in_specs=[pl.BlockSpec((tm, d), lambda i: (i, 0)), pl.BlockSpec((1, d), lambda i: (0, 0))],
        out_specs=pl.BlockSpec((tm, d), lambda i: (i, 0)),
        out_shape=jax.ShapeDtypeStruct((n, d), BF16),
        compiler_params=_cparams("parallel"),
        name="rmsnorm",
    )(x, g.reshape(1, d))


def _mm_kernel(a_ref, w_ref, o_ref):
    o_ref[...] = _dot(a_ref[...], w_ref[...]).astype(o_ref.dtype)


def _mm_res_kernel(a_ref, w_ref, r_ref, o_ref):
    o_ref[...] = r_ref[...] + _dot(a_ref[...], w_ref[...])


def _pick(n, pref):
    for t in pref:
        if n % t == 0:
            return t
    return n


def matmul(a, w, out_dtype=F32, residual=None, tm=1024, tn=None, name="matmul"):
    m, k = a.shape
    n = w.shape[1]
    tm = _pick(m, (tm, 512, 256, 128))
    tn = tn or _pick(n, (1024, 768, 512, 384, 256, 128))
    grid = (m // tm, n // tn)
    a_spec = pl.BlockSpec((tm, k), lambda i, j: (i, 0))
    w_spec = pl.BlockSpec((k, tn), lambda i, j: (0, j))
    o_spec = pl.BlockSpec((tm, tn), lambda i, j: (i, j))
    if residual is None:
        return pl.pallas_call(
            _mm_kernel, grid=grid, in_specs=[a_spec, w_spec], out_specs=o_spec,
            out_shape=jax.ShapeDtypeStruct((m, n), out_dtype),
            compiler_params=_cparams("parallel", "arbitrary"), name=name,
        )(a, w)
    return pl.pallas_call(
        _mm_res_kernel, grid=grid, in_specs=[a_spec, w_spec, o_spec], out_specs=o_spec,
        out_shape=jax.ShapeDtypeStruct((m, n), F32),
        compiler_params=_cparams("parallel", "arbitrary"), name=name,
    )(a, w, residual)


def _rwkv_prep_kernel(x_ref, prev_ref, mu_ref, w0_ref, w2_ref, a0_ref, a2_ref, g2_ref, kk_ref, ka_ref,
                      rk_ref, r_o, lw_o, k_o, v_o, a_o, b_o, g_o, bonus_o, *, seq, width):
    t = x_ref.shape[0]
    i = pl.program_id(0)
    x = x_ref[...]
    seq_start = (i * t) % seq == 0
    prev = jnp.where(seq_start, 0.0, prev_ref[SUBLANES - 1:SUBLANES, :])
    rows = lax.broadcasted_iota(jnp.int32, x.shape, 0)
    shifted = jnp.where(rows == 0, prev, pltpu.roll(x, 1, axis=0))
    x = x + (shifted - x) * mu_ref[...]
    w = width
    r = x[:, 0:w]
    k = x[:, w:2 * w]
    v = x[:, 2 * w:3 * w]
    o = 3 * w
    w_lo = x[:, o:o + LANES]
    a_lo = x[:, o + LANES:o + 2 * LANES]
    g_lo = x[:, o + 2 * LANES:o + 4 * LANES]
    seg = _seg_matrix()

    wv = w0_ref[...] + _dot3(jnp.tanh(w_lo), w2_ref[...])
    w_log = -(jnp.maximum(-wv, 0.0) + jnp.log(1.0 + jnp.exp(-jnp.abs(wv)))) - 0.5
    lw_o[...] = -jnp.exp(w_log)
    a = _sigmoid(a0_ref[...] + _dot3(a_lo, a2_ref[...]))
    g_o[...] = _dot3(_sigmoid(g_lo), g2_ref[...])

    kk = k * kk_ref[...]
    nrm = jnp.sqrt(_head_sum(kk * kk, seg))
    kk = kk / jnp.maximum(nrm, 1e-12)
    k = k * (1.0 + (a - 1.0) * ka_ref[...])
    r_o[...] = r
    k_o[...] = k
    v_o[...] = v
    a_o[...] = -kk
    b_o[...] = kk * a
    bonus_o[...] = _head_sum(r * k * rk_ref[...], seg) * v


def rwkv_prep(proj, mu, w0, w2, a0, a2, g2, k_k, k_a, r_k, seq, tm=128):
    n, wp = proj.shape
    width = w0.shape[-1]
    tm = min(tm, seq)
    row = lambda i: (i, 0)
    fix = lambda i: (0, 0)
    vec = pl.BlockSpec((1, width), fix)
    out = pl.BlockSpec((tm, width), row)
    per = tm // SUBLANES
    return pl.pallas_call(
        functools.partial(_rwkv_prep_kernel, seq=seq, width=width),
        grid=(n // tm,),
        in_specs=[pl.BlockSpec((tm, wp), row),
                  pl.BlockSpec((SUBLANES, wp), lambda i: (jnp.maximum(i * per - 1, 0), 0)),
                  pl.BlockSpec((1, wp), fix), vec, pl.BlockSpec(w2.shape, fix), vec,
                  pl.BlockSpec(a2.shape, fix), pl.BlockSpec(g2.shape, fix), vec, vec, vec],
        out_specs=[out] * 8,
        out_shape=[jax.ShapeDtypeStruct((n, width), F32)] * 8,
        compiler_params=_cparams("parallel"), name="rwkv_prep",
    )(proj, proj, mu, w0, w2, a0, a2, g2, k_k, k_a, r_k)


SCAN_PASSES = dict(gram=1, gkv=1, inv=1, t=1, r=1, y=1, m=1, s=3)


def _pdot(a, b, dims, passes):
    if passes == 1:
        return _dot(a.astype(BF16), b.astype(BF16), dims)
    return _dot3(a, b, dims)


def _rwkv_scan_kernel(r_ref, lw_ref, k_ref, v_ref, a_ref, b_ref, y_ref, st_ref, *, chunk, wave):
    c_len = chunk
    hd = HEAD_DIM
    nheads = LANES // hd
    ps = SCAN_PASSES

    @pl.when(pl.program_id(2) == 0)
    def _():
        st_ref[...] = jnp.zeros_like(st_ref)

    nchunks = r_ref.shape[0] // c_len
    row2 = lax.broadcasted_iota(jnp.int32, (2 * c_len, c_len), 0)
    col2 = lax.broadcasted_iota(jnp.int32, (2 * c_len, c_len), 1)
    mask2 = col2 <= jnp.where(row2 < c_len, row2 - 1, row2 - c_len)
    row = lax.broadcasted_iota(jnp.int32, (c_len, c_len), 0)
    col = lax.broadcasted_iota(jnp.int32, (c_len, c_len), 1)
    eye_c = (row == col).astype(F32)
    eye_k = (lax.broadcasted_iota(jnp.int32, (hd, hd), 0)
             == lax.broadcasted_iota(jnp.int32, (hd, hd), 1)).astype(F32)
    ridx = lax.broadcasted_iota(jnp.int32, (c_len, LANES), 0)

    def load_units(chunks):
        units = []
        for c in chunks:
            rows = slice(c * c_len, (c + 1) * c_len)
            lw = lw_ref[rows, :]
            cum = lw
            sh = 1
            while sh < c_len:
                cum = cum + jnp.where(ridx >= sh, pltpu.roll(cum, sh, axis=0), 0.0)
                sh *= 2
            total = cum[c_len - 1:c_len, :]
            e_neg = jnp.exp(-cum)
            e_end = jnp.exp(total - cum)
            rt = r_ref[rows, :] * jnp.exp(cum)
            at = a_ref[rows, :] * jnp.exp(cum - lw)
            b = b_ref[rows, :]
            k = k_ref[rows, :]
            bt, kt, bh, kh = b * e_neg, k * e_neg, b * e_end, k * e_end
            v = v_ref[rows, :]
            g_end = jnp.exp(total)
            for h in range(nheads):
                sl = slice(h * hd, (h + 1) * hd)
                units.append(dict(c=c, h=h, at=at[:, sl], rt=rt[:, sl], v=v[:, sl], bt=bt[:, sl], kt=kt[:, sl],
                                  bh=bh[:, sl], kh=kh[:, sl], g=g_end[:, sl]))
        return units

    def phase1(units):
        for u in units:
            u["l2"] = jnp.concatenate([u["at"], u["rt"]], axis=0)
            u["gb"] = jnp.where(mask2, _pdot(u["l2"], u["bt"], NT, ps["gram"]), 0.0)
        yield
        for u in units:
            u["gk"] = jnp.where(mask2, _pdot(u["l2"], u["kt"], NT, ps["gram"]), 0.0)
            u["inv"] = eye_c + u["gb"][:c_len]
            u["lp"] = u["gb"][:c_len]
        yield
        for u in units:
            u["gkv"] = _pdot(u["gk"], u["v"], NN, ps["gkv"])
        yield
        p = 2
        while p < c_len:
            for u in units:
                u["lp"] = _pdot(u["lp"], u["lp"], NN, ps["inv"])
            yield
            for u in units:
                u["inv"] = u["inv"] + _pdot(u["inv"], u["lp"], NN, ps["inv"])
            yield
            p *= 2
        for u in units:
            u["ta"] = _pdot(u["inv"], u["at"], NN, ps["t"])
        yield
        for u in units:
            u["u0"] = _pdot(u["inv"], u["gkv"][:c_len], NN, ps["t"])
        yield
        for u in units:
            u["rhat"] = u["rt"] + _pdot(u["gb"][c_len:], u["ta"], NN, ps["r"])
        yield
        for u in units:
            u["y1"] = _pdot(u["gb"][c_len:], u["u0"], NN, ps["r"]) + u["gkv"][c_len:]
        yield
        for u in units:
            u["m"] = eye_k * u["g"] + _pdot(u["bh"], u["ta"], TN, ps["m"])
        yield
        for u in units:
            u["nm"] = _pdot(jnp.concatenate([u["bh"], u["kh"]], axis=0),
                            jnp.concatenate([u["u0"], u["v"]], axis=0), TN, ps["m"])
        yield

    state = [st_ref[h] for h in range(nheads)]

    def phase2_steps(units):
        by_chunk = {}
        for u in units:
            by_chunk.setdefault(u["c"], []).append(u)

        def step(c):
            ys = []
            for u in by_chunk[c]:
                h = u["h"]
                ys.append(_pdot(u["rhat"], state[h], NN, ps["y"]) + u["y1"])
                state[h] = _pdot(u["m"], state[h], NN, ps["s"]) + u["nm"]
            y_ref[c * c_len:(c + 1) * c_len, :] = jnp.concatenate(ys, axis=1)

        return [functools.partial(step, c) for c in sorted(by_chunk)]

    pending = []
    for w0 in range(0, nchunks, wave):
        units = load_units(range(w0, min(w0 + wave, nchunks)))
        for _ in phase1(units):
            if pending:
                pending.pop(0)()
        for step in pending:
            step()
        pending = phase2_steps(units)
    for step in pending:
        step()
    for h in range(nheads):
        st_ref[h] = state[h]


def rwkv_scan(r, lw, k, v, a, b, batch, seq, tblk=RWKV_TBLK):
    n, width = r.shape
    tblk = min(tblk, seq)
    nt = seq // tblk
    spec = pl.BlockSpec((tblk, LANES), lambda bi, hp, tb: (bi * nt + tb, hp))
    return pl.pallas_call(
        functools.partial(_rwkv_scan_kernel, chunk=RWKV_CHUNK, wave=RWKV_WAVE),
        grid=(batch, width // LANES, nt),
        in_specs=[spec] * 6,
        out_specs=spec,
        out_shape=jax.ShapeDtypeStruct((n, width), F32),
        scratch_shapes=[pltpu.VMEM((LANES // HEAD_DIM, HEAD_DIM, HEAD_DIM), F32)],
        compiler_params=_cparams("parallel", "parallel", "arbitrary"), name="rwkv_scan",
    )(r, lw, k, v, a, b)


def _rwkv_post_kernel(y_ref, bonus_ref, g_ref, lg_ref, lb_ref, o_ref):
    seg = _seg_matrix()
    y = y_ref[...]
    inv_n = 1.0 / HEAD_DIM
    mean = _head_sum(y, seg) * inv_n
    yc = y - mean
    var = _head_sum(yc * yc, seg) * inv_n
    yn = yc * lax.rsqrt(var + RWKV_LN_EPS) * lg_ref[...] + lb_ref[...]
    o_ref[...] = ((yn + bonus_ref[...]) * g_ref[...]).astype(o_ref.dtype)


def rwkv_post(y, bonus, g, lnx_g, lnx_b, tm=256):
    n, width = y.shape
    tm = _pick(n, (tm, 128))
    row = pl.BlockSpec((tm, width), lambda i: (i, 0))
    vec = pl.BlockSpec((1, width), lambda i: (0, 0))
    return pl.pallas_call(
        _rwkv_post_kernel, grid=(n // tm,),
        in_specs=[row, row, row, vec, vec], out_specs=row,
        out_shape=jax.ShapeDtypeStruct((n, width), BF16),
        compiler_params=_cparams("parallel"), name="rwkv_post",
    )(y, bonus, g, lnx_g, lnx_b)


def _t5_causal_bucket(dist):
    max_exact = REL_BUCKETS // 2
    d_f = jnp.maximum(dist, 1).astype(F32)
    large = max_exact + (jnp.log(d_f / max_exact) / math.log(REL_MAX_DIST / max_exact)
                         * (REL_BUCKETS - max_exact)).astype(jnp.int32)
    large = jnp.minimum(large, REL_BUCKETS - 1)
    return jnp.where(dist < max_exact, dist, large)


def _pattern_bias(rel_bias_g, window, dil):
    span = window // dil
    qi = np.arange(ATTN_BLOCK)[:, None]
    ki = np.arange(2 * ATTN_BLOCK)[None, :]
    off = qi + ATTN_BLOCK - ki
    band = (off >= 0) & (off <= span)
    bucket = _t5_causal_bucket(jnp.asarray(np.clip(off, 0, span) * dil, jnp.int32))
    onehot = (bucket[:, :, None] == jnp.arange(REL_BUCKETS)[None, None, :]).astype(F32)
    bias = jnp.einsum("qkb,bh->hqk", onehot, rel_bias_g.astype(F32), precision=lax.Precision.HIGHEST)
    return jnp.where(jnp.asarray(band)[None], bias, NEG_INF)


def _attn_blocks(blocks, bias_ref, qg, kg, no_prev):
    hd = HEAD_DIM
    scale = hd ** -0.5
    seg = _seg_matrix()
    ones_k = jnp.ones((2 * ATTN_BLOCK, LANES), BF16)
    head_of_lane = lax.broadcasted_iota(jnp.int32, (ATTN_BLOCK, LANES), 1) // hd

    def norm(x, g):
        ms = _dot((x * x).astype(BF16), seg) * (1.0 / hd)
        return x * lax.rsqrt(ms + RMS_EPS) * g

    pairs = []
    for bi, (q, kp, kc, vp, vc) in enumerate(blocks):
        for c in range(q.shape[1] // LANES):
            sl = slice(c * LANES, (c + 1) * LANES)
            pairs.append(dict(b=bi, c=c, q=q[:, sl], k=jnp.concatenate([kp[:, sl], kc[:, sl]], axis=0),
                              v=jnp.concatenate([vp[:, sl], vc[:, sl]], axis=0)))
    for pr in pairs:
        pr["qn"] = norm(pr["q"], qg)
        pr["kn"] = norm(pr["k"], kg).astype(BF16)
        pr["vb"] = pr["v"].astype(BF16)
    units = [dict(pr=pr, h=h) for pr in pairs for h in range(LANES // hd)]
    for u in units:
        pr = u["pr"]
        qm = jnp.where(head_of_lane == u["h"], pr["qn"], 0.0).astype(BF16)
        s = _dot(qm, pr["kn"], NT) * scale + bias_ref[pr["c"] * (LANES // hd) + u["h"]]
        u["s"] = jnp.where(no_prev, NEG_INF, s)
    for u in units:
        u["m"] = jnp.max(u["s"], axis=-1, keepdims=True)
    for u in units:
        u["p"] = jnp.exp(u["s"] - u["m"]).astype(BF16)
    for u in units:
        u["den"] = _dot(u["p"], ones_k)
    for u in units:
        u["o"] = _dot(u["p"], u["pr"]["vb"]) / u["den"]
        u["lse"] = u["m"] + jnp.log(u["den"])
    res = []
    for bi in range(len(blocks)):
        outs, lses = [], []
        for pr in (p_ for p_ in pairs if p_["b"] == bi):
            u0, u1 = [u for u in units if u["pr"] is pr]
            outs.append(jnp.where(head_of_lane == 0, u0["o"], u1["o"]))
            lses.append(jnp.where(head_of_lane == 0, u0["lse"], u1["lse"]))
        res.append((outs[0] if len(outs) == 1 else jnp.concatenate(outs, axis=1),
                    lses[0] if len(lses) == 1 else jnp.concatenate(lses, axis=1)))
    return res


def _no_prev_mask():
    ki = lax.broadcasted_iota(jnp.int32, (ATTN_BLOCK, 2 * ATTN_BLOCK), 1)
    return jnp.logical_and(pl.program_id(1) == 0, ki < ATTN_BLOCK)


def _attn_kernel(q_ref, kp_ref, kc_ref, vp_ref, vc_ref, bias_ref, qg_ref, kg_ref, o_ref, lse_ref):
    (o, lse), = _attn_blocks([(q_ref[...], kp_ref[...], kc_ref[...], vp_ref[...], vc_ref[...])], bias_ref,
                             qg_ref[...], kg_ref[...], _no_prev_mask())
    o_ref[...] = o
    lse_ref[...] = lse


def _attn_dilated_kernel(q_ref, kp_ref, kc_ref, vp_ref, vc_ref, bias_ref, qg_ref, kg_ref, o_ref, lse_ref, *, dil):
    no_prev = _no_prev_mask()
    group = min(dil, ATTN_GROUP)

    def body(rg, carry):
        rows = [pl.ds(rg * group + j, ATTN_BLOCK, stride=dil) for j in range(group)]
        blocks = [(q_ref[rw, :], kp_ref[rw, :], kc_ref[rw, :], vp_ref[rw, :], vc_ref[rw, :]) for rw in rows]
        for rw, (o, lse) in zip(rows, _attn_blocks(blocks, bias_ref, qg_ref[...], kg_ref[...], no_prev)):
            o_ref[rw, :] = o
            lse_ref[rw, :] = lse
        return carry

    lax.fori_loop(0, dil // group, body, 0)


def attn_dilated(proj, bias, q_g, k_g, gi, dil, batch, seq, width):
    n = proj.shape[0]
    npat = len(DILATED_PATTERNS)
    pw = width // npat
    rows = ATTN_BLOCK * dil
    nb = seq // rows
    sec = width // LANES
    hp_per = pw // LANES
    heads_pp = LANES // HEAD_DIM

    def spec(section, prev):
        def imap(b, nblk, hp):
            blk = jnp.maximum(nblk - 1, 0) if prev else nblk
            return (b * nb + blk, section * sec + gi * hp_per + hp)
        return pl.BlockSpec((rows, LANES), imap)

    bias_spec = pl.BlockSpec((heads_pp,) + bias.shape[1:], lambda b, nblk, hp: (hp, 0, 0))
    vec = pl.BlockSpec((1, LANES), lambda b, nblk, hp: (0, 0))
    q_g, k_g = (jnp.tile(g, (1, heads_pp)) for g in (q_g, k_g))
    ospec = pl.BlockSpec((rows, LANES), lambda b, nblk, hp: (b * nb + nblk, hp))
    return pl.pallas_call(
        functools.partial(_attn_dilated_kernel, dil=dil), grid=(batch, nb, hp_per),
        in_specs=[spec(0, False), spec(1, True), spec(1, False), spec(2, True), spec(2, False), bias_spec, vec, vec],
        out_specs=[ospec, ospec],
        out_shape=[jax.ShapeDtypeStruct((n, pw), F32)] * 2,
        compiler_params=_cparams("parallel", "parallel", "parallel"), name=f"attn_d{dil}",
    )(proj, proj, proj, proj, proj, bias, q_g, k_g)


def attn_pattern(proj, bias, q_g, k_g, gi, dil, batch, seq, width):
    n = proj.shape[0]
    npat = len(DILATED_PATTERNS)
    pw = width // npat
    nsub = seq // dil
    nb = nsub // ATTN_BLOCK
    sec = width // pw
    rowlen = 3 * sec
    x = proj.reshape(batch * nsub, dil * 3 * width)

    def spec(section, prev):
        def imap(b, nblk, r):
            blk = jnp.maximum(nblk - 1, 0) if prev else nblk
            return (b * nb + blk, r * rowlen + section * sec + gi)
        return pl.BlockSpec((ATTN_BLOCK, pw), imap)

    fix3 = pl.BlockSpec(bias.shape, lambda b, nblk, r: (0, 0, 0))
    vec = pl.BlockSpec((1, LANES), lambda b, nblk, r: (0, 0))
    q_g, k_g = (jnp.tile(g, (1, LANES // HEAD_DIM)) for g in (q_g, k_g))
    ospec = pl.BlockSpec((ATTN_BLOCK, pw), lambda b, nblk, r: (b * nb + nblk, r))
    o, lse = pl.pallas_call(
        _attn_kernel, grid=(batch, nb, dil),
        in_specs=[spec(0, False), spec(1, True), spec(1, False), spec(2, True), spec(2, False), fix3, vec, vec],
        out_specs=[ospec, ospec],
        out_shape=[jax.ShapeDtypeStruct((batch * nsub, dil * pw), F32)] * 2,
        compiler_params=_cparams("parallel", "parallel", "parallel"), name=f"attn_d{dil}",
    )(x, x, x, x, x, bias, q_g, k_g)
    return o.reshape(n, pw), lse.reshape(n, pw)


def _attn_mix_kernel(o0, o1, o2, l0, l1, l2, out_ref):
    a, b, c = l0[...], l1[...], l2[...]
    m = jnp.maximum(jnp.maximum(a, b), c)
    ea, eb, ec = jnp.exp(a - m), jnp.exp(b - m), jnp.exp(c - m)
    inv = 1.0 / (ea + eb + ec)
    out_ref[...] = jnp.concatenate([o0[...] * (ea * inv), o1[...] * (eb * inv), o2[...] * (ec * inv)],
                                   axis=1).astype(out_ref.dtype)


def attn_mix(outs, lses, tm=512):
    n, pw = outs[0].shape
    tm = _pick(n, (tm, 256, 128))
    row = pl.BlockSpec((tm, pw), lambda i: (i, 0))
    return pl.pallas_call(
        _attn_mix_kernel, grid=(n // tm,), in_specs=[row] * 6,
        out_specs=pl.BlockSpec((tm, 3 * pw), lambda i: (i, 0)),
        out_shape=jax.ShapeDtypeStruct((n, 3 * pw), BF16),
        compiler_params=_cparams("parallel"), name="attn_mix",
    )(*outs, *lses)


def _conv_kernel(x_ref, halo_ref, w_ref, b_ref, lg_ref, lb_ref, o_ref, ext_ref, *, seq, width):
    t = x_ref.shape[0]
    i = pl.program_id(0)
    seq_start = (i * t) % seq == 0

    def glu(x):
        return x[:, :width] * _sigmoid(x[:, width:])

    ext_ref[0:CONV_HALO, :] = jnp.where(seq_start, 0.0, glu(halo_ref[...]))
    ext_ref[CONV_HALO:, :] = glu(x_ref[...])
    base = CONV_HALO - (CONV_TAPS - 1)
    acc = jnp.broadcast_to(b_ref[...], (t, width))
    for j in range(CONV_TAPS):
        acc = acc + w_ref[j:j + 1, :] * ext_ref[base + j:base + j + t, :]
    mu = jnp.mean(acc, axis=-1, keepdims=True)
    d = acc - mu
    var = jnp.mean(d * d, axis=-1, keepdims=True)
    y = d * lax.rsqrt(var + LN_EPS) * lg_ref[...] + lb_ref[...]
    o_ref[...] = (y * _sigmoid(y)).astype(o_ref.dtype)


def conv_module(proj, conv_w, conv_b, ln_g, ln_b, seq, tm=256):
    n, w2 = proj.shape
    width = w2 // 2
    tm = min(tm, seq)
    per = tm // CONV_HALO
    fix = lambda i: (0, 0)
    vec = pl.BlockSpec((1, width), fix)
    return pl.pallas_call(
        functools.partial(_conv_kernel, seq=seq, width=width), grid=(n // tm,),
        in_specs=[pl.BlockSpec((tm, w2), lambda i: (i, 0)),
                  pl.BlockSpec((CONV_HALO, w2), lambda i: (jnp.maximum(i * per - 1, 0), 0)),
                  pl.BlockSpec((CONV_TAPS, width), fix), vec, vec, vec],
        out_specs=pl.BlockSpec((tm, width), lambda i: (i, 0)),
        out_shape=jax.ShapeDtypeStruct((n, width), BF16),
        scratch_shapes=[pltpu.VMEM((tm + CONV_HALO, width), F32)],
        compiler_params=_cparams("parallel"), name="conv_module",
    )(proj, proj, conv_w, conv_b, ln_g, ln_b)


ROUTER_LANES = LANES


def _pack_bf16_pairs(x):
    w = x.shape[1] // 2
    hi = lax.bitcast_convert_type(x[:, :w].astype(BF16).astype(F32), jnp.uint32)
    lo = lax.bitcast_convert_type(x[:, w:].astype(BF16).astype(F32), jnp.uint32)
    return hi | (lo >> 16)


def _unpack_bf16_pairs(words):
    hi = lax.bitcast_convert_type(words & jnp.uint32(0xFFFF0000), F32)
    lo = lax.bitcast_convert_type(words << 16, F32)
    return hi, lo


def _route_kernel(x_ref, g_ref, rw_ref, bias_ref, info_ref, cnt_ref, xp_ref, run_ref):
    t = x_ref.shape[0]

    @pl.when(pl.program_id(0) == 0)
    def _():
        run_ref[...] = jnp.zeros_like(run_ref)

    x = x_ref[...]
    ms = jnp.mean(x * x, axis=-1, keepdims=True)
    xn = x * lax.rsqrt(ms + RMS_EPS) * g_ref[...]
    xp_ref[...] = _pack_bf16_pairs(xn)
    z = _dot3(xn, rw_ref[...]) + bias_ref[...]
    lane = lax.broadcasted_iota(jnp.int32, z.shape, 1).astype(F32)
    ninf = -jnp.inf
    big = float(ROUTER_LANES)
    zc = jnp.where(lane < N_GROUPS, z, ninf)
    mc = jnp.max(zc, axis=-1, keepdims=True)
    pg = 1.0 / jnp.sum(jnp.exp(zc - mc), axis=-1, keepdims=True)
    g_idx = jnp.min(jnp.where(zc == mc, lane, big), axis=-1, keepdims=True)
    lo = N_GROUPS + g_idx * EXPERTS_PER_GROUP
    zf = jnp.where(jnp.logical_and(lane >= lo, lane < lo + EXPERTS_PER_GROUP), z, ninf)
    m1 = jnp.max(zf, axis=-1, keepdims=True)
    i1 = jnp.min(jnp.where(zf == m1, lane, big), axis=-1, keepdims=True)
    zf2 = jnp.where(lane == i1, ninf, zf)
    m2 = jnp.max(zf2, axis=-1, keepdims=True)
    i2 = jnp.min(jnp.where(zf2 == m2, lane, big), axis=-1, keepdims=True)
    e2 = jnp.exp(m2 - m1)
    w1 = pg / (1.0 + e2)
    w2 = pg * e2 / (1.0 + e2)
    onehot = jnp.where(jnp.logical_or(lane == i1, lane == i2), 1.0, 0.0)
    tri = (lax.broadcasted_iota(jnp.int32, (t, t), 0) > lax.broadcasted_iota(jnp.int32, (t, t), 1)).astype(BF16)
    before = _dot(tri, onehot.astype(BF16)) + run_ref[...]
    r1 = jnp.sum(jnp.where(lane == i1, before, 0.0), axis=-1, keepdims=True)
    r2 = jnp.sum(jnp.where(lane == i2, before, 0.0), axis=-1, keepdims=True)
    run_ref[...] += jnp.sum(onehot, axis=0, keepdims=True)
    cnt_ref[...] = run_ref[...]
    cols = (i1 - N_GROUPS, i2 - N_GROUPS, w1, w2, r1, r2)
    info = jnp.zeros_like(z)
    for j, c in enumerate(cols):
        info = jnp.where(lane == float(j), c, info)
    info_ref[...] = info


def route(x, g, router_w, bias, tm=256):
    n, d = x.shape
    w = router_w.shape[1]
    tm = _pick(n, (tm, 128))
    fix = lambda i: (0, 0)
    row = pl.BlockSpec((tm, w), lambda i: (i, 0))
    one = pl.BlockSpec((1, w), fix)
    return pl.pallas_call(
        _route_kernel, grid=(n // tm,),
        in_specs=[pl.BlockSpec((tm, d), lambda i: (i, 0)), pl.BlockSpec((1, d), fix), pl.BlockSpec((d, w), fix), one],
        out_specs=[row, one, pl.BlockSpec((tm, d // 2), lambda i: (i, 0))],
        out_shape=[jax.ShapeDtypeStruct((n, w), F32), jax.ShapeDtypeStruct((1, w), F32),
                   jax.ShapeDtypeStruct((n, d // 2), jnp.uint32)],
        scratch_shapes=[pltpu.VMEM((1, w), F32)],
        compiler_params=_cparams("arbitrary"), name="route",
    )(x, g.reshape(1, d), router_w, bias)


MOE_TILE = 256


def _moe_dest_kernel(info_ref, off_ref, dd_ref):
    info = info_ref[...]
    lane = lax.broadcasted_iota(jnp.int32, info.shape, 1).astype(F32)
    off = off_ref[...]

    def dest(e, r):
        return jnp.sum(jnp.where(lane == e + N_GROUPS, off, 0.0), axis=-1, keepdims=True) + r

    d1 = dest(info[:, 0:1], info[:, 4:5])
    d2 = dest(info[:, 1:2], info[:, 5:6])
    dd_ref[...] = jnp.where(lane == 0.0, d1, jnp.where(lane == 1.0, d2, 0.0)).astype(jnp.int32)


def moe_plan(info, counts, n_tiles, tm=512):
    n, w = info.shape
    cnt = counts[0, N_GROUPS:N_GROUPS + N_EXPERTS].astype(jnp.int32)
    padded = (cnt + MOE_TILE - 1) // MOE_TILE * MOE_TILE
    ends = jnp.cumsum(padded)
    off_row = jnp.pad((ends - padded).astype(F32), (N_GROUPS, w - N_GROUPS - N_EXPERTS)).reshape(1, w)
    tm = _pick(n, (tm, 256, 128))
    row = pl.BlockSpec((tm, w), lambda i: (i, 0))
    dd = pl.pallas_call(
        _moe_dest_kernel, grid=(n // tm,),
        in_specs=[row, pl.BlockSpec((1, w), lambda i: (0, 0))], out_specs=row,
        out_shape=jax.ShapeDtypeStruct((n, w), jnp.int32),
        compiler_params=_cparams("parallel"), name="moe_dest",
    )(info, off_row)
    tile_start = jnp.arange(n_tiles, dtype=jnp.int32) * MOE_TILE
    tile_expert = jnp.minimum(jnp.sum((tile_start[:, None] >= ends[None, :]).astype(jnp.int32), axis=1),
                              N_EXPERTS - 1)
    n_used = (ends[-1:] // MOE_TILE).astype(jnp.int32)
    return dd[:, 0], dd[:, 1], tile_expert, n_used


DMA_ISSUE_UNROLL = 8


def _moe_ffn_kernel(d1_ref, d2_ref, texp_ref, nused_ref, xp_hbm, wi_ref, wo_ref, ys_ref,
                    src_ref, xbuf, sem, *, n_tok, ff):
    i = pl.program_id(0)
    n_used = nused_ref[0]
    n_rows = src_ref.shape[0]
    half = xp_hbm.shape[1]

    def gather(tile, slot):
        def body(j, carry):
            tok = src_ref[tile * MOE_TILE + j]
            pltpu.make_async_copy(xp_hbm.at[pl.ds(tok, 1), :], xbuf.at[slot, pl.ds(j, 1), :], sem.at[slot]).start()
            return carry
        lax.fori_loop(0, MOE_TILE, body, 0, unroll=DMA_ISSUE_UNROLL)

    @pl.when(i == 0)
    def _():
        def clear(j, carry):
            src_ref[j] = 0
            return carry
        lax.fori_loop(0, n_rows, clear, 0, unroll=DMA_ISSUE_UNROLL)

        def fill(tok, carry):
            src_ref[d1_ref[tok]] = tok
            src_ref[d2_ref[tok]] = tok
            return carry
        lax.fori_loop(0, n_tok, fill, 0, unroll=DMA_ISSUE_UNROLL)
        gather(0, 0)

    @pl.when(i < n_used)
    def _():
        slot = i % 2

        @pl.when(i + 1 < n_used)
        def _():
            gather(i + 1, 1 - slot)

        pltpu.make_async_copy(xp_hbm.at[pl.ds(0, MOE_TILE), :], xbuf.at[slot], sem.at[slot]).wait()
        x_a, x_b = _unpack_bf16_pairs(xbuf[slot])
        gu = (_dot(x_a.astype(BF16), wi_ref[0, :half, :].astype(BF16))
              + _dot(x_b.astype(BF16), wi_ref[0, half:, :].astype(BF16)))
        gt, up = gu[:, :ff], gu[:, ff:]
        hid = gt * _sigmoid(gt) * up
        ys_ref[...] = _pack_bf16_pairs(_dot(hid.astype(BF16), wo_ref[0].astype(BF16)))

    @pl.when(i >= n_used)
    def _():
        ys_ref[...] = jnp.zeros_like(ys_ref)


def moe_ffn(xp, w_in, w_out, d1, d2, tile_expert, n_used):
    n, half = xp.shape
    ne, d, ff2 = w_in.shape
    n_tiles = tile_expert.shape[0]
    n_rows = n_tiles * MOE_TILE
    last = lambda i, nu: jnp.minimum(i, nu[0] - 1)
    grid_spec = pltpu.PrefetchScalarGridSpec(
        num_scalar_prefetch=4, grid=(n_tiles,),
        in_specs=[pl.BlockSpec(memory_space=pl.ANY),
                  pl.BlockSpec((1, d, ff2), lambda i, d1, d2, te, nu: (te[last(i, nu)], 0, 0)),
                  pl.BlockSpec((1, ff2 // 2, d), lambda i, d1, d2, te, nu: (te[last(i, nu)], 0, 0))],
        out_specs=pl.BlockSpec((MOE_TILE, half), lambda i, d1, d2, te, nu: (i, 0)),
        scratch_shapes=[pltpu.SMEM((n_rows,), jnp.int32), pltpu.VMEM((2, MOE_TILE, half), jnp.uint32),
                        pltpu.SemaphoreType.DMA((2,))])
    return pl.pallas_call(
        functools.partial(_moe_ffn_kernel, n_tok=n, ff=ff2 // 2), grid_spec=grid_spec,
        out_shape=jax.ShapeDtypeStruct((n_rows, half), jnp.uint32),
        compiler_params=_cparams("arbitrary"), name="moe_ffn",
    )(d1, d2, tile_expert, n_used, xp, w_in, w_out)


def _moe_combine_kernel(d1_ref, d2_ref, ys_hbm, h_ref, info_ref, g_ref, o_ref, xn_ref, ybuf, sem):
    t = h_ref.shape[0]
    half = ys_hbm.shape[1]
    base = pl.program_id(0) * t

    def issue(j, carry):
        pltpu.make_async_copy(ys_hbm.at[pl.ds(d1_ref[base + j], 1), :], ybuf.at[0, pl.ds(j, 1), :], sem.at[0]).start()
        pltpu.make_async_copy(ys_hbm.at[pl.ds(d2_ref[base + j], 1), :], ybuf.at[1, pl.ds(j, 1), :], sem.at[1]).start()
        return carry
    lax.fori_loop(0, t, issue, 0, unroll=DMA_ISSUE_UNROLL)
    for s in range(2):
        pltpu.make_async_copy(ys_hbm.at[pl.ds(0, t), :], ybuf.at[s], sem.at[s]).wait()
    info = info_ref[...]
    w1, w2 = info[:, 2:3], info[:, 3:4]
    y1a, y1b = _unpack_bf16_pairs(ybuf[0])
    y2a, y2b = _unpack_bf16_pairs(ybuf[1])
    h_a = h_ref[:, :half] + w1 * y1a + w2 * y2a
    h_b = h_ref[:, half:] + w1 * y1b + w2 * y2b
    o_ref[:, :half] = h_a
    o_ref[:, half:] = h_b
    ms = (jnp.sum(h_a * h_a, axis=-1, keepdims=True) + jnp.sum(h_b * h_b, axis=-1, keepdims=True)) / (2 * half)
    scale = lax.rsqrt(ms + RMS_EPS)
    xn_ref[:, :half] = (h_a * scale * g_ref[:, :half]).astype(xn_ref.dtype)
    xn_ref[:, half:] = (h_b * scale * g_ref[:, half:]).astype(xn_ref.dtype)


def moe_combine(ys, h, info, g_next, d1, d2, tm=256):
    n, d = h.shape
    tm = _pick(n, (tm, 128))
    row = lambda i, d1, d2: (i, 0)
    tile = pl.BlockSpec((tm, d), row)
    grid_spec = pltpu.PrefetchScalarGridSpec(
        num_scalar_prefetch=2, grid=(n // tm,),
        in_specs=[pl.BlockSpec(memory_space=pl.ANY), tile, pl.BlockSpec((tm, info.shape[1]), row),
                  pl.BlockSpec((1, d), lambda i, d1, d2: (0, 0))],
        out_specs=[tile, tile],
        scratch_shapes=[pltpu.VMEM((2, tm, d // 2), jnp.uint32), pltpu.SemaphoreType.DMA((2,))])
    return pl.pallas_call(
        _moe_combine_kernel, grid_spec=grid_spec,
        out_shape=[jax.ShapeDtypeStruct((n, d), F32), jax.ShapeDtypeStruct((n, d), BF16)],
        compiler_params=_cparams("arbitrary"), name="moe_combine",
    )(d1, d2, ys, h, info, g_next.reshape(1, d))


def _ple_kernel(a_ref, w_ref, p_ref, pw_ref, h_ref, o_ref):
    gate = _sigmoid(_dot(a_ref[...], w_ref[...]))
    o_ref[...] = h_ref[...] + _dot(p_ref[...], pw_ref[...]) * gate


def ple(xn, gate_w, p, proj_w, h, tm=1024, tn=512):
    n, d = xn.shape
    pd = p.shape[1]
    tm = _pick(n, (tm, 512, 256, 128))
    tn = _pick(d, (tn, 256, 128))
    tile = pl.BlockSpec((tm, tn), lambda i, j: (i, j))
    return pl.pallas_call(
        _ple_kernel, grid=(n // tm, d // tn),
        in_specs=[pl.BlockSpec((tm, d), lambda i, j: (i, 0)), pl.BlockSpec((d, tn), lambda i, j: (0, j)),
                  pl.BlockSpec((tm, pd), lambda i, j: (i, 0)), pl.BlockSpec((pd, tn), lambda i, j: (0, j)), tile],
        out_specs=tile,
        out_shape=jax.ShapeDtypeStruct((n, d), F32),
        compiler_params=_cparams("parallel", "arbitrary"), name="ple",
    )(xn, gate_w, p, proj_w, h)


def _pad_cols(x, total):
    return jnp.pad(x, ((0, 0),) * (x.ndim - 1) + ((0, total - x.shape[-1]),))


def _pad_rows(x, total):
    return jnp.pad(x, ((0, 0),) * (x.ndim - 2) + ((0, total - x.shape[-2]), (0, 0)))


def _pack_rwkv_cols(x, width, lora):
    dl, al, gl = lora
    o = 3 * width
    return jnp.concatenate([
        x[..., :o], _pad_cols(x[..., o:o + dl], LANES), _pad_cols(x[..., o + dl:o + dl + al], LANES),
        _pad_cols(x[..., o + dl + al:o + dl + al + gl], 2 * LANES)], axis=-1)


def kernel(x, p, norm_mix_g, w_in, rwkv_mu, rwkv_w0, rwkv_w2, rwkv_a0, rwkv_a2, rwkv_g2, rwkv_k_k, rwkv_k_a, rwkv_r_k, rwkv_lnx_g, rwkv_lnx_b, q_norm_g, k_norm_g, rel_bias, conv_w, conv_b, conv_ln_g, conv_ln_b, w_out, norm_ffn_g, router_c_w, router_c_b, router_f_w, router_f_b, expert_w_in, expert_w_out, ple_norm_g, ple_gate_w, ple_proj):
    batch, seq, d = x.shape
    depth = w_in.shape[0]
    n = batch * seq
    rw = rwkv_w0.shape[-1]
    lora = (rwkv_w2.shape[1], rwkv_a2.shape[1], rwkv_g2.shape[1])
    assert max(lora[0], lora[1]) <= LANES and lora[2] <= 2 * LANES
    rwkv_proj = 3 * rw + sum(lora)
    cw = conv_w.shape[-1]
    aw = d - rw - cw
    npat = len(DILATED_PATTERNS)
    hpp = aw // HEAD_DIM // npat
    row = lambda v: v.reshape(1, -1)

    biases = [_pattern_bias(rel_bias[:, gi * hpp:(gi + 1) * hpp], window, dil)
              for gi, (window, dil) in enumerate(DILATED_PATTERNS)]

    w_rwkv_all = _pack_rwkv_cols(w_in[..., :rwkv_proj], rw, lora).astype(BF16)
    w_attn_all = w_in[..., rwkv_proj:rwkv_proj + 3 * aw].astype(BF16)
    w_conv_all = w_in[..., rwkv_proj + 3 * aw:].astype(BF16)
    w_out_all = w_out.astype(BF16)
    ple_gate_all = ple_gate_w.astype(BF16)
    ple_proj_all = ple_proj.astype(BF16)
    p_all = p.reshape(depth, n, -1).astype(BF16)
    mu_all = _pack_rwkv_cols(rwkv_mu[:, None, :], rw, lora)
    w2_all = _pad_rows(rwkv_w2, LANES)
    a2_all = _pad_rows(rwkv_a2, LANES)
    g2_all = _pad_rows(rwkv_g2, 2 * LANES)
    router_w_all = _pad_cols(jnp.concatenate([router_c_w, router_f_w], axis=-1), ROUTER_LANES)
    router_b_all = _pad_cols(jnp.concatenate([router_c_b, router_f_b], axis=-1)[:, None, :], ROUTER_LANES)

    h = x.reshape(n, d)
    for i in range(depth):
        xn = rmsnorm_cast(h, norm_mix_g[i])
        p_rwkv = matmul(xn, w_rwkv_all[i], name="proj_rwkv")
        p_attn = matmul(xn, w_attn_all[i], name="proj_attn")
        p_conv = matmul(xn, w_conv_all[i], name="proj_conv")

        r, lw, k, v, a, b, g, bonus = rwkv_prep(
            p_rwkv, mu_all[i], row(rwkv_w0[i]), w2_all[i], row(rwkv_a0[i]), a2_all[i], g2_all[i],
            row(rwkv_k_k[i]), row(rwkv_k_a[i]), row(rwkv_r_k[i]), seq)
        y = rwkv_scan(r, lw, k, v, a, b, batch, seq)
        y_rwkv = rwkv_post(y, bonus, g, row(rwkv_lnx_g[i]), row(rwkv_lnx_b[i]))

        outs, lses = [], []
        for gi, (window, dil) in enumerate(DILATED_PATTERNS):
            attn = attn_pattern if dil == 1 else attn_dilated
            o, lse = attn(p_attn, biases[gi], row(q_norm_g[i]), row(k_norm_g[i]), gi, dil, batch, seq, aw)
            outs.append(o)
            lses.append(lse)
        y_attn = attn_mix(outs, lses)

        y_conv = conv_module(p_conv, conv_w[i], row(conv_b[i]), row(conv_ln_g[i]), row(conv_ln_b[i]), seq)

        mix = jnp.concatenate([y_rwkv, y_attn, y_conv], axis=1)
        h = matmul(mix, w_out_all[i], residual=h, name="out_proj")

        info, counts, xp = route(h, norm_ffn_g[i], router_w_all[i], router_b_all[i])
        n_tiles = (2 * n) // MOE_TILE + N_EXPERTS
        d1, d2, tile_expert, n_used = moe_plan(info, counts, n_tiles)
        ys = moe_ffn(xp, expert_w_in[i], expert_w_out[i], d1, d2, tile_expert, n_used)
        h, xn = moe_combine(ys, h, info, ple_norm_g[i], d1, d2)

        h = ple(xn, ple_gate_all[i], p_all[i], ple_proj_all[i], h)
    return h.reshape(batch, seq, d)
```

```python
import functools
import math

import numpy as np
import jax
import jax.numpy as jnp
from jax import lax
from jax.experimental import pallas as pl
from jax.experimental.pallas import tpu as pltpu

F32 = jnp.float32
BF16 = jnp.bfloat16

HEAD_DIM = 64
DILATED_PATTERNS = ((128, 1), (512, 4), (2048, 16))
ATTN_BLOCK = 128
REL_BUCKETS = 32
REL_MAX_DIST = 2048
NEG_INF = -1e30
CONV_TAPS = 31
N_GROUPS = 8
EXPERTS_PER_GROUP = 8
N_EXPERTS = N_GROUPS * EXPERTS_PER_GROUP
RMS_EPS = 1e-6
LN_EPS = 1e-5
RWKV_LN_EPS = 64e-5

LANES = 128
SUBLANES = 8
VMEM_LIMIT_BYTES = 56 * 1024 * 1024

RWKV_CHUNK = 64
RWKV_HEADS_PER_STEP = 4
RWKV_WAVE = 4
RWKV_TBLK = 1024
ATTN_GROUP = 4
CONV_HALO = 32

NN = (((1,), (0,)), ((), ()))
NT = (((1,), (1,)), ((), ()))
TN = (((0,), (0,)), ((), ()))


def _cparams(*sem):
    return pltpu.CompilerParams(dimension_semantics=sem, vmem_limit_bytes=VMEM_LIMIT_BYTES)


def _dot(a, b, dims=NN):
    return lax.dot_general(a, b, dims, preferred_element_type=F32)


def _split(x):
    hi = x.astype(BF16)
    lo = (x - hi.astype(F32)).astype(BF16)
    return hi, lo


def _dot3(a, b, dims=NN):
    ah, al = _split(a)
    bh, bl = _split(b)
    return _dot(ah, bh, dims) + (_dot(ah, bl, dims) + _dot(al, bh, dims))


def _dot2(a, b_bf16, dims=NN):
    ah, al = _split(a)
    return _dot(ah, b_bf16, dims) + _dot(al, b_bf16, dims)


def _sigmoid(x):
    return 1.0 / (1.0 + jnp.exp(-x))


def _seg_matrix():
    r = lax.broadcasted_iota(jnp.int32, (LANES, LANES), 0) // HEAD_DIM
    c = lax.broadcasted_iota(jnp.int32, (LANES, LANES), 1) // HEAD_DIM
    return (r == c).astype(BF16)


def _head_sum(x, seg):
    w = x.shape[1]
    cols = [_dot2(x[:, j:j + LANES], seg) for j in range(0, w, LANES)]
    return cols[0] if len(cols) == 1 else jnp.concatenate(cols, axis=1)


def _rmsnorm_kernel(x_ref, g_ref, o_ref):
    x = x_ref[...]
    ms = jnp.mean(x * x, axis=-1, keepdims=True)
    o_ref[...] = (x * lax.rsqrt(ms + RMS_EPS) * g_ref[...]).astype(o_ref.dtype)


def rmsnorm_cast(x, g, tm=512):
    n, d = x.shape
    tm = min(tm, n)
    return pl.pallas_call(
        _rmsnorm_kernel,
        grid=(n // tm,),
        in_specs=[pl.BlockSpec((tm, d), lambda i: (i, 0)), pl.BlockSpec((1, d), lambda i: (0, 0))],
        out_specs=pl.BlockSpec((tm, d), lambda i: (i, 0)),
        out_shape=jax.ShapeDtypeStruct((n, d), BF16),
        compiler_params=_cparams("parallel"),
        name="rmsnorm",
    )(x, g.reshape(1, d))


def _mm_kernel(a_ref, w_ref, o_ref):
    o_ref[...] = _dot(a_ref[...], w_ref[...]).astype(o_ref.dtype)


def _mm_res_kernel(a_ref, w_ref, r_ref, o_ref):
    o_ref[...] = r_ref[...] + _dot(a_ref[...], w_ref[...])


def _pick(n, pref):
    for t in pref:
        if n % t == 0:
            return t
    return n


def matmul(a, w, out_dtype=F32, residual=None, tm=1024, tn=None, name="matmul"):
    m, k = a.shape
    n = w.shape[1]
    tm = _pick(m, (tm, 512, 256, 128))
    tn = tn or _pick(n, (1024, 768, 512, 384, 256, 128))
    grid = (m // tm, n // tn)
    a_spec = pl.BlockSpec((tm, k), lambda i, j: (i, 0))
    w_spec = pl.BlockSpec((k, tn), lambda i, j: (0, j))
    o_spec = pl.BlockSpec((tm, tn), lambda i, j: (i, j))
    if residual is None:
        return pl.pallas_call(
            _mm_kernel, grid=grid, in_specs=[a_spec, w_spec], out_specs=o_spec,
            out_shape=jax.ShapeDtypeStruct((m, n), out_dtype),
            compiler_params=_cparams("parallel", "arbitrary"), name=name,
        )(a, w)
    return pl.pallas_call(
        _mm_res_kernel, grid=grid, in_specs=[a_spec, w_spec, o_spec], out_specs=o_spec,
        out_shape=jax.ShapeDtypeStruct((m, n), F32),
        compiler_params=_cparams("parallel", "arbitrary"), name=name,
    )(a, w, residual)


def _rwkv_prep_kernel(x_ref, prev_ref, mu_ref, w0_ref, w2_ref, a0_ref, a2_ref, g2_ref, kk_ref, ka_ref,
                      rk_ref, r_o, lw_o, k_o, v_o, a_o, b_o, g_o, bonus_o, *, seq, width):
    t = x_ref.shape[0]
    i = pl.program_id(0)
    x = x_ref[...]
    seq_start = (i * t) % seq == 0
    prev = jnp.where(seq_start, 0.0, prev_ref[SUBLANES - 1:SUBLANES, :])
    rows = lax.broadcasted_iota(jnp.int32, x.shape, 0)
    shifted = jnp.where(rows == 0, prev, pltpu.roll(x, 1, axis=0))
    x = x + (shifted - x) * mu_ref[...]
    w = width
    r = x[:, 0:w]
    k = x[:, w:2 * w]
    v = x[:, 2 * w:3 * w]
    o = 3 * w
    w_lo = x[:, o:o + LANES]
    a_lo = x[:, o + LANES:o + 2 * LANES]
    g_lo = x[:, o + 2 * LANES:o + 4 * LANES]
    seg = _seg_matrix()

    wv = w0_ref[...] + _dot3(jnp.tanh(w_lo), w2_ref[...])
    w_log = -(jnp.maximum(-wv, 0.0) + jnp.log(1.0 + jnp.exp(-jnp.abs(wv)))) - 0.5
    lw_o[...] = -jnp.exp(w_log)
    a = _sigmoid(a0_ref[...] + _dot3(a_lo, a2_ref[...]))
    g_o[...] = _dot3(_sigmoid(g_lo), g2_ref[...])

    kk = k * kk_ref[...]
    nrm = jnp.sqrt(_head_sum(kk * kk, seg))
    kk = kk / jnp.maximum(nrm, 1e-12)
    k = k * (1.0 + (a - 1.0) * ka_ref[...])
    r_o[...] = r
    k_o[...] = k
    v_o[...] = v
    a_o[...] = -kk
    b_o[...] = kk * a
    bonus_o[...] = _head_sum(r * k * rk_ref[...], seg) * v


def rwkv_prep(proj, mu, w0, w2, a0, a2, g2, k_k, k_a, r_k, seq, tm=128):
    n, wp = proj.shape
    width = w0.shape[-1]
    tm = min(tm, seq)
    row = lambda i: (i, 0)
    fix = lambda i: (0, 0)
    vec = pl.BlockSpec((1, width), fix)
    out = pl.BlockSpec((tm, width), row)
    per = tm // SUBLANES
    return pl.pallas_call(
        functools.partial(_rwkv_prep_kernel, seq=seq, width=width),
        grid=(n // tm,),
        in_specs=[pl.BlockSpec((tm, wp), row),
                  pl.BlockSpec((SUBLANES, wp), lambda i: (jnp.maximum(i * per - 1, 0), 0)),
                  pl.BlockSpec((1, wp), fix), vec, pl.BlockSpec(w2.shape, fix), vec,
                  pl.BlockSpec(a2.shape, fix), pl.BlockSpec(g2.shape, fix), vec, vec, vec],
        out_specs=[out] * 8,
        out_shape=[jax.ShapeDtypeStruct((n, width), F32)] * 8,
        compiler_params=_cparams("parallel"), name="rwkv_prep",
    )(proj, proj, mu, w0, w2, a0, a2, g2, k_k, k_a, r_k)


SCAN_PASSES = dict(gram=1, gkv=1, inv=1, t=1, r=1, y=1, m=1, s=3)


def _pdot(a, b, dims, passes):
    if passes == 1:
        return _dot(a.astype(BF16), b.astype(BF16), dims)
    return _dot3(a, b, dims)


def _rwkv_scan_kernel(r_ref, lw_ref, k_ref, v_ref, a_ref, b_ref, y_ref, st_ref, *, chunk, wave):
    c_len = chunk
    hd = HEAD_DIM
    nheads = r_ref.shape[1] // hd
    ps = SCAN_PASSES

    @pl.when(pl.program_id(2) == 0)
    def _():
        st_ref[...] = jnp.zeros_like(st_ref)

    nchunks = r_ref.shape[0] // c_len
    row2 = lax.broadcasted_iota(jnp.int32, (2 * c_len, 2 * c_len), 0)
    col2 = lax.broadcasted_iota(jnp.int32, (2 * c_len, 2 * c_len), 1) % c_len
    mask2 = col2 <= jnp.where(row2 < c_len, row2 - 1, row2 - c_len)
    rowc = lax.broadcasted_iota(jnp.int32, (c_len, 2 * c_len), 0)
    colc = lax.broadcasted_iota(jnp.int32, (c_len, 2 * c_len), 1)
    right = colc >= c_len
    eye_right = (colc - c_len == rowc).astype(F32)
    zeros_h = jnp.zeros((c_len, hd), F32)
    zeros_w = jnp.zeros((c_len, 2 * c_len), F32)
    eye_k = (lax.broadcasted_iota(jnp.int32, (hd, hd), 0)
             == lax.broadcasted_iota(jnp.int32, (hd, hd), 1)).astype(F32)
    ridx = lax.broadcasted_iota(jnp.int32, (c_len, r_ref.shape[1]), 0)

    def load_units(chunks):
        units = []
        for c in chunks:
            rows = slice(c * c_len, (c + 1) * c_len)
            lw = lw_ref[rows, :]
            cum = lw
            sh = 1
            while sh < c_len:
                cum = cum + jnp.where(ridx >= sh, pltpu.roll(cum, sh, axis=0), 0.0)
                sh *= 2
            total = cum[c_len - 1:c_len, :]
            e_neg = jnp.exp(-cum)
            e_end = jnp.exp(total - cum)
            rt = r_ref[rows, :] * jnp.exp(cum)
            at = a_ref[rows, :] * jnp.exp(cum - lw)
            b = b_ref[rows, :]
            k = k_ref[rows, :]
            bt, kt, bh, kh = b * e_neg, k * e_neg, b * e_end, k * e_end
            v = v_ref[rows, :]
            g_end = jnp.exp(total)
            for h in range(nheads):
                sl = slice(h * hd, (h + 1) * hd)
                units.append(dict(c=c, h=h, at=at[:, sl], rt=rt[:, sl], v=v[:, sl], bt=bt[:, sl], kt=kt[:, sl],
                                  bh=bh[:, sl], kh=kh[:, sl], g=g_end[:, sl]))
        return units

    def phase1(units):
        def below(x):
            return jnp.concatenate([zeros_h, x], axis=0)

        def above(x):
            return jnp.concatenate([x, zeros_h], axis=0)

        for u in units:
            l2 = jnp.concatenate([u["at"], u["rt"]], axis=0)
            bk = jnp.concatenate([u["bt"], u["kt"]], axis=0)
            u["gm"] = jnp.where(mask2, _pdot(l2, bk, NT, ps["gram"]), 0.0)
        yield
        for u in units:
            u["gkv"] = _pdot(u["gm"], below(u["v"]), NN, ps["gkv"])
            u["w"] = jnp.where(right, eye_right, u["gm"][:c_len])
        yield
        p = 1
        while p < c_len:
            for u in units:
                w = u["w"]
                u["w"] = _pdot(w, jnp.concatenate([w, zeros_w], axis=0), NN, ps["inv"]) + jnp.where(right, w, 0.0)
            yield
            p *= 2
        for u in units:
            u["ta"] = _pdot(u["w"], below(u["at"]), NN, ps["t"])
        yield
        for u in units:
            u["u0"] = _pdot(u["w"], below(u["gkv"][:c_len]), NN, ps["t"])
        yield
        for u in units:
            u["rhat"] = u["rt"] + _pdot(u["gm"][c_len:], above(u["ta"]), NN, ps["r"])
        yield
        for u in units:
            u["y1"] = _pdot(u["gm"][c_len:], above(u["u0"]), NN, ps["r"]) + u["gkv"][c_len:]
        yield
        for u in units:
            u["m"] = eye_k * u["g"] + _pdot(u["bh"], u["ta"], TN, ps["m"])
        yield
        for u in units:
            u["nm"] = _pdot(jnp.concatenate([u["bh"], u["kh"]], axis=0),
                            jnp.concatenate([u["u0"], u["v"]], axis=0), TN, ps["m"])
        yield

    state = [st_ref[h] for h in range(nheads)]

    def phase2_steps(units):
        by_chunk = {}
        for u in units:
            by_chunk.setdefault(u["c"], []).append(u)

        def step(c):
            ys = []
            for u in by_chunk[c]:
                h = u["h"]
                ys.append(_pdot(u["rhat"], state[h], NN, ps["y"]) + u["y1"])
                state[h] = _pdot(u["m"], state[h], NN, ps["s"]) + u["nm"]
            y_ref[c * c_len:(c + 1) * c_len, :] = jnp.concatenate(ys, axis=1)

        return [functools.partial(step, c) for c in sorted(by_chunk)]

    pending = []
    for w0 in range(0, nchunks, wave):
        units = load_units(range(w0, min(w0 + wave, nchunks)))
        for _ in phase1(units):
            if pending:
                pending.pop(0)()
        for step in pending:
            step()
        pending = phase2_steps(units)
    for step in pending:
        step()
    for h in range(nheads):
        st_ref[h] = state[h]


def rwkv_scan(r, lw, k, v, a, b, batch, seq, tblk=RWKV_TBLK):
    n, width = r.shape
    tblk = min(tblk, seq)
    nt = seq // tblk
    wl = RWKV_HEADS_PER_STEP * HEAD_DIM
    assert width % wl == 0, (width, wl)
    spec = pl.BlockSpec((tblk, wl), lambda bi, hp, tb: (bi * nt + tb, hp))
    return pl.pallas_call(
        functools.partial(_rwkv_scan_kernel, chunk=RWKV_CHUNK, wave=RWKV_WAVE),
        grid=(batch, width // wl, nt),
        in_specs=[spec] * 6,
        out_specs=spec,
        out_shape=jax.ShapeDtypeStruct((n, width), F32),
        scratch_shapes=[pltpu.VMEM((RWKV_HEADS_PER_STEP, HEAD_DIM, HEAD_DIM), F32)],
        compiler_params=_cparams("parallel", "parallel", "arbitrary"), name="rwkv_scan",
    )(r, lw, k, v, a, b)


def _rwkv_post_kernel(y_ref, bonus_ref, g_ref, lg_ref, lb_ref, o_ref):
    seg = _seg_matrix()
    y = y_ref[...]
    inv_n = 1.0 / HEAD_DIM
    mean = _head_sum(y, seg) * inv_n
    yc = y - mean
    var = _head_sum(yc * yc, seg) * inv_n
    yn = yc * lax.rsqrt(var + RWKV_LN_EPS) * lg_ref[...] + lb_ref[...]
    o_ref[...] = ((yn + bonus_ref[...]) * g_ref[...]).astype(o_ref.dtype)


def rwkv_post(y, bonus, g, lnx_g, lnx_b, tm=256):
    n, width = y.shape
    tm = _pick(n, (tm, 128))
    row = pl.BlockSpec((tm, width), lambda i: (i, 0))
    vec = pl.BlockSpec((1, width), lambda i: (0, 0))
    return pl.pallas_call(
        _rwkv_post_kernel, grid=(n // tm,),
        in_specs=[row, row, row, vec, vec], out_specs=row,
        out_shape=jax.ShapeDtypeStruct((n, width), BF16),
        compiler_params=_cparams("parallel"), name="rwkv_post",
    )(y, bonus, g, lnx_g, lnx_b)


def _t5_causal_bucket(dist):
    max_exact = REL_BUCKETS // 2
    d_f = jnp.maximum(dist, 1).astype(F32)
    large = max_exact + (jnp.log(d_f / max_exact) / math.log(REL_MAX_DIST / max_exact)
                         * (REL_BUCKETS - max_exact)).astype(jnp.int32)
    large = jnp.minimum(large, REL_BUCKETS - 1)
    return jnp.where(dist < max_exact, dist, large)


def _pattern_bias(rel_bias_g, window, dil):
    span = window // dil
    qi = np.arange(ATTN_BLOCK)[:, None]
    ki = np.arange(2 * ATTN_BLOCK)[None, :]
    off = qi + ATTN_BLOCK - ki
    band = (off >= 0) & (off <= span)
    bucket = _t5_causal_bucket(jnp.asarray(np.clip(off, 0, span) * dil, jnp.int32))
    onehot = (bucket[:, :, None] == jnp.arange(REL_BUCKETS)[None, None, :]).astype(F32)
    bias = jnp.einsum("qkb,bh->hqk", onehot, rel_bias_g.astype(F32), precision=lax.Precision.HIGHEST)
    return jnp.where(jnp.asarray(band)[None], bias, NEG_INF)


def _attn_blocks(blocks, bias_ref, qg, kg, no_prev):
    hd = HEAD_DIM
    scale = hd ** -0.5
    seg = _seg_matrix()
    ones_k = jnp.ones((2 * ATTN_BLOCK, LANES), BF16)
    head_of_lane = lax.broadcasted_iota(jnp.int32, (ATTN_BLOCK, LANES), 1) // hd

    def norm(x, g):
        ms = _dot((x * x).astype(BF16), seg) * (1.0 / hd)
        return x * lax.rsqrt(ms + RMS_EPS) * g

    pairs = []
    for bi, (q, kp, kc, vp, vc) in enumerate(blocks):
        for c in range(q.shape[1] // LANES):
            sl = slice(c * LANES, (c + 1) * LANES)
            pairs.append(dict(b=bi, c=c, q=q[:, sl], k=jnp.concatenate([kp[:, sl], kc[:, sl]], axis=0),
                              v=jnp.concatenate([vp[:, sl], vc[:, sl]], axis=0)))
    for pr in pairs:
        pr["qn"] = norm(pr["q"], qg)
        pr["kn"] = norm(pr["k"], kg).astype(BF16)
        pr["vb"] = pr["v"].astype(BF16)
    units = [dict(pr=pr, h=h) for pr in pairs for h in range(LANES // hd)]
    for u in units:
        pr = u["pr"]
        qm = jnp.where(head_of_lane == u["h"], pr["qn"], 0.0).astype(BF16)
        s = _dot(qm, pr["kn"], NT) * scale + bias_ref[pr["c"] * (LANES // hd) + u["h"]]
        u["s"] = jnp.where(no_prev, NEG_INF, s)
    for u in units:
        u["m"] = jnp.max(u["s"], axis=-1, keepdims=True)
    for u in units:
        u["p"] = jnp.exp(u["s"] - u["m"]).astype(BF16)
    for u in units:
        u["den"] = _dot(u["p"], ones_k)
    for u in units:
        u["o"] = _dot(u["p"], u["pr"]["vb"]) / u["den"]
        u["lse"] = u["m"] + jnp.log(u["den"])
    res = []
    for bi in range(len(blocks)):
        outs, lses = [], []
        for pr in (p_ for p_ in pairs if p_["b"] == bi):
            u0, u1 = [u for u in units if u["pr"] is pr]
            outs.append(jnp.where(head_of_lane == 0, u0["o"], u1["o"]))
            lses.append(jnp.where(head_of_lane == 0, u0["lse"], u1["lse"]))
        res.append((outs[0] if len(outs) == 1 else jnp.concatenate(outs, axis=1),
                    lses[0] if len(lses) == 1 else jnp.concatenate(lses, axis=1)))
    return res


def _no_prev_mask():
    ki = lax.broadcasted_iota(jnp.int32, (ATTN_BLOCK, 2 * ATTN_BLOCK), 1)
    return jnp.logical_and(pl.program_id(1) == 0, ki < ATTN_BLOCK)


def _attn_kernel(q_ref, kp_ref, kc_ref, vp_ref, vc_ref, bias_ref, qg_ref, kg_ref, o_ref, lse_ref):
    (o, lse), = _attn_blocks([(q_ref[...], kp_ref[...], kc_ref[...], vp_ref[...], vc_ref[...])], bias_ref,
                             qg_ref[...], kg_ref[...], _no_prev_mask())
    o_ref[...] = o
    lse_ref[...] = lse


def _attn_dilated_kernel(q_ref, kp_ref, kc_ref, vp_ref, vc_ref, bias_ref, qg_ref, kg_ref, o_ref, lse_ref, *, dil):
    no_prev = _no_prev_mask()
    group = min(dil, ATTN_GROUP)

    def body(rg, carry):
        rows = [pl.ds(rg * group + j, ATTN_BLOCK, stride=dil) for j in range(group)]
        blocks = [(q_ref[rw, :], kp_ref[rw, :], kc_ref[rw, :], vp_ref[rw, :], vc_ref[rw, :]) for rw in rows]
        for rw, (o, lse) in zip(rows, _attn_blocks(blocks, bias_ref, qg_ref[...], kg_ref[...], no_prev)):
            o_ref[rw, :] = o
            lse_ref[rw, :] = lse
        return carry

    lax.fori_loop(0, dil // group, body, 0)


def attn_dilated(proj, bias, q_g, k_g, gi, dil, batch, seq, width):
    n = proj.shape[0]
    npat = len(DILATED_PATTERNS)
    pw = width // npat
    rows = ATTN_BLOCK * dil
    nb = seq // rows
    sec = width // LANES
    hp_per = pw // LANES
    heads_pp = LANES // HEAD_DIM

    def spec(section, prev):
        def imap(b, nblk, hp):
            blk = jnp.maximum(nblk - 1, 0) if prev else nblk
            return (b * nb + blk, section * sec + gi * hp_per + hp)
        return pl.BlockSpec((rows, LANES), imap)

    bias_spec = pl.BlockSpec((heads_pp,) + bias.shape[1:], lambda b, nblk, hp: (hp, 0, 0))
    vec = pl.BlockSpec((1, LANES), lambda b, nblk, hp: (0, 0))
    q_g, k_g = (jnp.tile(g, (1, heads_pp)) for g in (q_g, k_g))
    ospec = pl.BlockSpec((rows, LANES), lambda b, nblk, hp: (b * nb + nblk, hp))
    return pl.pallas_call(
        functools.partial(_attn_dilated_kernel, dil=dil), grid=(batch, nb, hp_per),
        in_specs=[spec(0, False), spec(1, True), spec(1, False), spec(2, True), spec(2, False), bias_spec, vec, vec],
        out_specs=[ospec, ospec],
        out_shape=[jax.ShapeDtypeStruct((n, pw), F32)] * 2,
        compiler_params=_cparams("parallel", "parallel", "parallel"), name=f"attn_d{dil}",
    )(proj, proj, proj, proj, proj, bias, q_g, k_g)


def attn_pattern(proj, bias, q_g, k_g, gi, dil, batch, seq, width):
    n = proj.shape[0]
    npat = len(DILATED_PATTERNS)
    pw = width // npat
    nsub = seq // dil
    nb = nsub // ATTN_BLOCK
    sec = width // pw
    rowlen = 3 * sec
    x = proj.reshape(batch * nsub, dil * 3 * width)

    def spec(section, prev):
        def imap(b, nblk, r):
            blk = jnp.maximum(nblk - 1, 0) if prev else nblk
            return (b * nb + blk, r * rowlen + section * sec + gi)
        return pl.BlockSpec((ATTN_BLOCK, pw), imap)

    fix3 = pl.BlockSpec(bias.shape, lambda b, nblk, r: (0, 0, 0))
    vec = pl.BlockSpec((1, LANES), lambda b, nblk, r: (0, 0))
    q_g, k_g = (jnp.tile(g, (1, LANES // HEAD_DIM)) for g in (q_g, k_g))
    ospec = pl.BlockSpec((ATTN_BLOCK, pw), lambda b, nblk, r: (b * nb + nblk, r))
    o, lse = pl.pallas_call(
        _attn_kernel, grid=(batch, nb, dil),
        in_specs=[spec(0, False), spec(1, True), spec(1, False), spec(2, True), spec(2, False), fix3, vec, vec],
        out_specs=[ospec, ospec],
        out_shape=[jax.ShapeDtypeStruct((batch * nsub, dil * pw), F32)] * 2,
        compiler_params=_cparams("parallel", "parallel", "parallel"), name=f"attn_d{dil}",
    )(x, x, x, x, x, bias, q_g, k_g)
    return o.reshape(n, pw), lse.reshape(n, pw)


def _attn_mix_kernel(o0, o1, o2, l0, l1, l2, out_ref):
    a, b, c = l0[...], l1[...], l2[...]
    m = jnp.maximum(jnp.maximum(a, b), c)
    ea, eb, ec = jnp.exp(a - m), jnp.exp(b - m), jnp.exp(c - m)
    inv = 1.0 / (ea + eb + ec)
    out_ref[...] = jnp.concatenate([o0[...] * (ea * inv), o1[...] * (eb * inv), o2[...] * (ec * inv)],
                                   axis=1).astype(out_ref.dtype)


def attn_mix(outs, lses, tm=512):
    n, pw = outs[0].shape
    tm = _pick(n, (tm, 256, 128))
    row = pl.BlockSpec((tm, pw), lambda i: (i, 0))
    return pl.pallas_call(
        _attn_mix_kernel, grid=(n // tm,), in_specs=[row] * 6,
        out_specs=pl.BlockSpec((tm, 3 * pw), lambda i: (i, 0)),
        out_shape=jax.ShapeDtypeStruct((n, 3 * pw), BF16),
        compiler_params=_cparams("parallel"), name="attn_mix",
    )(*outs, *lses)


def _conv_kernel(x_ref, halo_ref, w_ref, b_ref, lg_ref, lb_ref, o_ref, ext_ref, shift_ref, *, seq, width):
    t = x_ref.shape[0]
    i = pl.program_id(0)
    seq_start = (i * t) % seq == 0

    def glu(x):
        return x[:, :width] * _sigmoid(x[:, width:])

    ext_ref[0:CONV_HALO, :] = jnp.where(seq_start, 0.0, glu(halo_ref[...]))
    ext_ref[CONV_HALO:, :] = glu(x_ref[...])
    base = CONV_HALO - (CONV_TAPS - 1)
    acc = jnp.broadcast_to(b_ref[...], (t, width))
    for rho in range(SUBLANES):
        offs = [base + j for j in range(CONV_TAPS) if (base + j) % SUBLANES == rho]
        if not offs:
            continue
        lo = offs[0]
        span = offs[-1] - lo + t
        shift_ref[0:span, :] = ext_ref[lo:lo + span, :]
        for o in offs:
            acc = acc + w_ref[o - base:o - base + 1, :] * shift_ref[o - lo:o - lo + t, :]
    mu = jnp.mean(acc, axis=-1, keepdims=True)
    d = acc - mu
    var = jnp.mean(d * d, axis=-1, keepdims=True)
    y = d * lax.rsqrt(var + LN_EPS) * lg_ref[...] + lb_ref[...]
    o_ref[...] = (y * _sigmoid(y)).astype(o_ref.dtype)


def conv_module(proj, conv_w, conv_b, ln_g, ln_b, seq, tm=256):
    n, w2 = proj.shape
    width = w2 // 2
    tm = min(tm, seq)
    per = tm // CONV_HALO
    fix = lambda i: (0, 0)
    vec = pl.BlockSpec((1, width), fix)
    return pl.pallas_call(
        functools.partial(_conv_kernel, seq=seq, width=width), grid=(n // tm,),
        in_specs=[pl.BlockSpec((tm, w2), lambda i: (i, 0)),
                  pl.BlockSpec((CONV_HALO, w2), lambda i: (jnp.maximum(i * per - 1, 0), 0)),
                  pl.BlockSpec((CONV_TAPS, width), fix), vec, vec, vec],
        out_specs=pl.BlockSpec((tm, width), lambda i: (i, 0)),
        out_shape=jax.ShapeDtypeStruct((n, width), BF16),
        scratch_shapes=[pltpu.VMEM((tm + CONV_HALO, width), F32), pltpu.VMEM((tm + CONV_HALO, width), F32)],
        compiler_params=_cparams("parallel"), name="conv_module",
    )(proj, proj, conv_w, conv_b, ln_g, ln_b)


ROUTER_LANES = LANES


def _pack_bf16_pairs(x):
    w = x.shape[1] // 2
    hi = lax.bitcast_convert_type(x[:, :w].astype(BF16).astype(F32), jnp.uint32)
    lo = lax.bitcast_convert_type(x[:, w:].astype(BF16).astype(F32), jnp.uint32)
    return hi | (lo >> 16)


def _unpack_bf16_pairs(words):
    hi = lax.bitcast_convert_type(words & jnp.uint32(0xFFFF0000), F32)
    lo = lax.bitcast_convert_type(words << 16, F32)
    return hi, lo


def _store_token_major(ref, words):
    t = words.shape[0]
    chunks = words.shape[1] // LANES
    for c in range(chunks):
        ref[pl.ds(c, t, stride=chunks), :] = words[:, c * LANES:(c + 1) * LANES]


def _load_token_major(ref, t, chunks):
    return jnp.concatenate([ref[pl.ds(c, t, stride=chunks), :] for c in range(chunks)], axis=1)


def _route_kernel(x_ref, g_ref, rw_ref, bias_ref, info_ref, cnt_ref, xp_ref, run_ref):
    t = x_ref.shape[0]

    @pl.when(pl.program_id(0) == 0)
    def _():
        run_ref[...] = jnp.zeros_like(run_ref)

    x = x_ref[...]
    ms = jnp.mean(x * x, axis=-1, keepdims=True)
    xn = x * lax.rsqrt(ms + RMS_EPS) * g_ref[...]
    _store_token_major(xp_ref, _pack_bf16_pairs(xn))
    z = _dot3(xn, rw_ref[...]) + bias_ref[...]
    lane = lax.broadcasted_iota(jnp.int32, z.shape, 1).astype(F32)
    ninf = -jnp.inf
    big = float(ROUTER_LANES)
    zc = jnp.where(lane < N_GROUPS, z, ninf)
    mc = jnp.max(zc, axis=-1, keepdims=True)
    pg = 1.0 / jnp.sum(jnp.exp(zc - mc), axis=-1, keepdims=True)
    g_idx = jnp.min(jnp.where(zc == mc, lane, big), axis=-1, keepdims=True)
    lo = N_GROUPS + g_idx * EXPERTS_PER_GROUP
    zf = jnp.where(jnp.logical_and(lane >= lo, lane < lo + EXPERTS_PER_GROUP), z, ninf)
    m1 = jnp.max(zf, axis=-1, keepdims=True)
    i1 = jnp.min(jnp.where(zf == m1, lane, big), axis=-1, keepdims=True)
    zf2 = jnp.where(lane == i1, ninf, zf)
    m2 = jnp.max(zf2, axis=-1, keepdims=True)
    i2 = jnp.min(jnp.where(zf2 == m2, lane, big), axis=-1, keepdims=True)
    e2 = jnp.exp(m2 - m1)
    w1 = pg / (1.0 + e2)
    w2 = pg * e2 / (1.0 + e2)
    onehot = jnp.where(jnp.logical_or(lane == i1, lane == i2), 1.0, 0.0)
    tri = (lax.broadcasted_iota(jnp.int32, (t, t), 0) > lax.broadcasted_iota(jnp.int32, (t, t), 1)).astype(BF16)
    before = _dot(tri, onehot.astype(BF16)) + run_ref[...]
    r1 = jnp.sum(jnp.where(lane == i1, before, 0.0), axis=-1, keepdims=True)
    r2 = jnp.sum(jnp.where(lane == i2, before, 0.0), axis=-1, keepdims=True)
    run_ref[...] += jnp.sum(onehot, axis=0, keepdims=True)
    cnt_ref[...] = run_ref[...]
    cols = (i1 - N_GROUPS, i2 - N_GROUPS, w1, w2, r1, r2)
    info = jnp.zeros_like(z)
    for j, c in enumerate(cols):
        info = jnp.where(lane == float(j), c, info)
    info_ref[...] = info


def route(x, g, router_w, bias, tm=256):
    n, d = x.shape
    w = router_w.shape[1]
    chunks = d // 2 // LANES
    tm = _pick(n, (tm, 128))
    fix = lambda i: (0, 0)
    row = pl.BlockSpec((tm, w), lambda i: (i, 0))
    one = pl.BlockSpec((1, w), fix)
    return pl.pallas_call(
        _route_kernel, grid=(n // tm,),
        in_specs=[pl.BlockSpec((tm, d), lambda i: (i, 0)), pl.BlockSpec((1, d), fix), pl.BlockSpec((d, w), fix), one],
        out_specs=[row, one, pl.BlockSpec((tm * chunks, LANES), lambda i: (i, 0))],
        out_shape=[jax.ShapeDtypeStruct((n, w), F32), jax.ShapeDtypeStruct((1, w), F32),
                   jax.ShapeDtypeStruct((n * chunks, LANES), jnp.uint32)],
        scratch_shapes=[pltpu.VMEM((1, w), F32)],
        compiler_params=_cparams("arbitrary"), name="route",
    )(x, g.reshape(1, d), router_w, bias)


MOE_TILE = 256


def _moe_dest_kernel(info_ref, off_ref, dd_ref):
    info = info_ref[...]
    lane = lax.broadcasted_iota(jnp.int32, info.shape, 1).astype(F32)
    off = off_ref[...]

    def dest(e, r):
        return jnp.sum(jnp.where(lane == e + N_GROUPS, off, 0.0), axis=-1, keepdims=True) + r

    d1 = dest(info[:, 0:1], info[:, 4:5])
    d2 = dest(info[:, 1:2], info[:, 5:6])
    dd_ref[...] = jnp.where(lane == 0.0, d1, jnp.where(lane == 1.0, d2, 0.0)).astype(jnp.int32)


def moe_plan(info, counts, n_tiles, tm=512):
    n, w = info.shape
    cnt = counts[0, N_GROUPS:N_GROUPS + N_EXPERTS].astype(jnp.int32)
    padded = (cnt + MOE_TILE - 1) // MOE_TILE * MOE_TILE
    ends = jnp.cumsum(padded)
    off_row = jnp.pad((ends - padded).astype(F32), (N_GROUPS, w - N_GROUPS - N_EXPERTS)).reshape(1, w)
    tm = _pick(n, (tm, 256, 128))
    row = pl.BlockSpec((tm, w), lambda i: (i, 0))
    dd = pl.pallas_call(
        _moe_dest_kernel, grid=(n // tm,),
        in_specs=[row, pl.BlockSpec((1, w), lambda i: (0, 0))], out_specs=row,
        out_shape=jax.ShapeDtypeStruct((n, w), jnp.int32),
        compiler_params=_cparams("parallel"), name="moe_dest",
    )(info, off_row)
    tile_start = jnp.arange(n_tiles, dtype=jnp.int32) * MOE_TILE
    tile_expert = jnp.minimum(jnp.sum((tile_start[:, None] >= ends[None, :]).astype(jnp.int32), axis=1),
                              N_EXPERTS - 1)
    n_used = (ends[-1:] // MOE_TILE).astype(jnp.int32)
    return dd[:, 0], dd[:, 1], tile_expert, n_used


DMA_ISSUE_UNROLL = 8


def _moe_ffn_kernel(d1_ref, d2_ref, texp_ref, nused_ref, xp_hbm, wi_ref, wo_ref, ys_ref,
                    src_ref, xbuf, sem, *, n_tok, ff):
    i = pl.program_id(0)
    n_used = nused_ref[0]
    n_rows = src_ref.shape[0]
    half = wi_ref.shape[1] // 2
    chunks = half // LANES

    def gather(tile, slot):
        def body(j, carry):
            tok = src_ref[tile * MOE_TILE + j]
            pltpu.make_async_copy(xp_hbm.at[pl.ds(tok * chunks, chunks), :],
                                  xbuf.at[slot, pl.ds(j * chunks, chunks), :], sem.at[slot]).start()
            return carry
        lax.fori_loop(0, MOE_TILE, body, 0, unroll=DMA_ISSUE_UNROLL)

    @pl.when(i == 0)
    def _():
        def clear(j, carry):
            src_ref[j] = 0
            return carry
        lax.fori_loop(0, n_rows, clear, 0, unroll=DMA_ISSUE_UNROLL)

        def fill(tok, carry):
            src_ref[d1_ref[tok]] = tok
            src_ref[d2_ref[tok]] = tok
            return carry
        lax.fori_loop(0, n_tok, fill, 0, unroll=DMA_ISSUE_UNROLL)
        gather(0, 0)

    @pl.when(i < n_used)
    def _():
        slot = i % 2

        @pl.when(i + 1 < n_used)
        def _():
            gather(i + 1, 1 - slot)

        pltpu.make_async_copy(xp_hbm.at[pl.ds(0, MOE_TILE * chunks), :], xbuf.at[slot], sem.at[slot]).wait()
        x_a, x_b = _unpack_bf16_pairs(_load_token_major(xbuf.at[slot], MOE_TILE, chunks))
        gu = (_dot(x_a.astype(BF16), wi_ref[0, :half, :].astype(BF16))
              + _dot(x_b.astype(BF16), wi_ref[0, half:, :].astype(BF16)))
        gt, up = gu[:, :ff], gu[:, ff:]
        hid = gt * _sigmoid(gt) * up
        _store_token_major(ys_ref, _pack_bf16_pairs(_dot(hid.astype(BF16), wo_ref[0].astype(BF16))))

    @pl.when(i >= n_used)
    def _():
        ys_ref[...] = jnp.zeros_like(ys_ref)


def moe_ffn(xp, w_in, w_out, d1, d2, tile_expert, n_used):
    ne, d, ff2 = w_in.shape
    chunks = d // 2 // LANES
    n = xp.shape[0] // chunks
    n_tiles = tile_expert.shape[0]
    n_rows = n_tiles * MOE_TILE
    last = lambda i, nu: jnp.minimum(i, nu[0] - 1)
    grid_spec = pltpu.PrefetchScalarGridSpec(
        num_scalar_prefetch=4, grid=(n_tiles,),
        in_specs=[pl.BlockSpec(memory_space=pl.ANY),
                  pl.BlockSpec((1, d, ff2), lambda i, d1, d2, te, nu: (te[last(i, nu)], 0, 0)),
                  pl.BlockSpec((1, ff2 // 2, d), lambda i, d1, d2, te, nu: (te[last(i, nu)], 0, 0))],
        out_specs=pl.BlockSpec((MOE_TILE * chunks, LANES), lambda i, d1, d2, te, nu: (i, 0)),
        scratch_shapes=[pltpu.SMEM((n_rows,), jnp.int32), pltpu.VMEM((2, MOE_TILE * chunks, LANES), jnp.uint32),
                        pltpu.SemaphoreType.DMA((2,))])
    return pl.pallas_call(
        functools.partial(_moe_ffn_kernel, n_tok=n, ff=ff2 // 2), grid_spec=grid_spec,
        out_shape=jax.ShapeDtypeStruct((n_rows * chunks, LANES), jnp.uint32),
        compiler_params=_cparams("arbitrary"), name="moe_ffn",
    )(d1, d2, tile_expert, n_used, xp, w_in, w_out)


def _moe_combine_kernel(d1_ref, d2_ref, ys_hbm, h_ref, info_ref, g_ref, o_ref, xn_ref, ybuf, sem):
    t = h_ref.shape[0]
    half = h_ref.shape[1] // 2
    chunks = half // LANES
    base = pl.program_id(0) * t

    def issue(j, carry):
        dst = pl.ds(j * chunks, chunks)
        pltpu.make_async_copy(ys_hbm.at[pl.ds(d1_ref[base + j] * chunks, chunks), :], ybuf.at[0, dst, :], sem.at[0]).start()
        pltpu.make_async_copy(ys_hbm.at[pl.ds(d2_ref[base + j] * chunks, chunks), :], ybuf.at[1, dst, :], sem.at[1]).start()
        return carry
    lax.fori_loop(0, t, issue, 0, unroll=DMA_ISSUE_UNROLL)
    for s in range(2):
        pltpu.make_async_copy(ys_hbm.at[pl.ds(0, t * chunks), :], ybuf.at[s], sem.at[s]).wait()
    info = info_ref[...]
    w1, w2 = info[:, 2:3], info[:, 3:4]
    y1a, y1b = _unpack_bf16_pairs(_load_token_major(ybuf.at[0], t, chunks))
    y2a, y2b = _unpack_bf16_pairs(_load_token_major(ybuf.at[1], t, chunks))
    h_a = h_ref[:, :half] + w1 * y1a + w2 * y2a
    h_b = h_ref[:, half:] + w1 * y1b + w2 * y2b
    o_ref[:, :half] = h_a
    o_ref[:, half:] = h_b
    ms = (jnp.sum(h_a * h_a, axis=-1, keepdims=True) + jnp.sum(h_b * h_b, axis=-1, keepdims=True)) / (2 * half)
    scale = lax.rsqrt(ms + RMS_EPS)
    xn_ref[:, :half] = (h_a * scale * g_ref[:, :half]).astype(xn_ref.dtype)
    xn_ref[:, half:] = (h_b * scale * g_ref[:, half:]).astype(xn_ref.dtype)


def moe_combine(ys, h, info, g_next, d1, d2, tm=256):
    n, d = h.shape
    tm = _pick(n, (tm, 128))
    row = lambda i, d1, d2: (i, 0)
    tile = pl.BlockSpec((tm, d), row)
    grid_spec = pltpu.PrefetchScalarGridSpec(
        num_scalar_prefetch=2, grid=(n // tm,),
        in_specs=[pl.BlockSpec(memory_space=pl.ANY), tile, pl.BlockSpec((tm, info.shape[1]), row),
                  pl.BlockSpec((1, d), lambda i, d1, d2: (0, 0))],
        out_specs=[tile, tile],
        scratch_shapes=[pltpu.VMEM((2, tm * (d // 2 // LANES), LANES), jnp.uint32), pltpu.SemaphoreType.DMA((2,))])
    return pl.pallas_call(
        _moe_combine_kernel, grid_spec=grid_spec,
        out_shape=[jax.ShapeDtypeStruct((n, d), F32), jax.ShapeDtypeStruct((n, d), BF16)],
        compiler_params=_cparams("arbitrary"), name="moe_combine",
    )(d1, d2, ys, h, info, g_next.reshape(1, d))


def _ple_kernel(a_ref, w_ref, p_ref, pw_ref, h_ref, o_ref):
    gate = _sigmoid(_dot(a_ref[...], w_ref[...]))
    o_ref[...] = h_ref[...] + _dot(p_ref[...], pw_ref[...]) * gate


def ple(xn, gate_w, p, proj_w, h, tm=1024, tn=512):
    n, d = xn.shape
    pd = p.shape[1]
    tm = _pick(n, (tm, 512, 256, 128))
    tn = _pick(d, (tn, 256, 128))
    tile = pl.BlockSpec((tm, tn), lambda i, j: (i, j))
    return pl.pallas_call(
        _ple_kernel, grid=(n // tm, d // tn),
        in_specs=[pl.BlockSpec((tm, d), lambda i, j: (i, 0)), pl.BlockSpec((d, tn), lambda i, j: (0, j)),
                  pl.BlockSpec((tm, pd), lambda i, j: (i, 0)), pl.BlockSpec((pd, tn), lambda i, j: (0, j)), tile],
        out_specs=tile,
        out_shape=jax.ShapeDtypeStruct((n, d), F32),
        compiler_params=_cparams("parallel", "arbitrary"), name="ple",
    )(xn, gate_w, p, proj_w, h)


def _pad_cols(x, total):
    return jnp.pad(x, ((0, 0),) * (x.ndim - 1) + ((0, total - x.shape[-1]),))


def _pad_rows(x, total):
    return jnp.pad(x, ((0, 0),) * (x.ndim - 2) + ((0, total - x.shape[-2]), (0, 0)))


def _pack_rwkv_cols(x, width, lora):
    dl, al, gl = lora
    o = 3 * width
    return jnp.concatenate([
        x[..., :o], _pad_cols(x[..., o:o + dl], LANES), _pad_cols(x[..., o + dl:o + dl + al], LANES),
        _pad_cols(x[..., o + dl + al:o + dl + al + gl], 2 * LANES)], axis=-1)


def _repack_w_in_kernel(w_ref, o_rwkv, o_attn, o_conv, *, rw, lora, aw):
    dl, al, gl = lora
    t = w_ref.shape[1]
    o = 3 * rw

    def put(dst, start, src_lo, width, padded):
        piece = w_ref[0, :, src_lo:src_lo + width].astype(BF16)
        if padded > width:
            piece = jnp.concatenate([piece, jnp.zeros((t, padded - width), BF16)], axis=1)
        dst[0, :, start:start + padded] = piece

    put(o_rwkv, 0, 0, o, o)
    put(o_rwkv, o, o, dl, LANES)
    put(o_rwkv, o + LANES, o + dl, al, LANES)
    put(o_rwkv, o + 2 * LANES, o + dl + al, gl, 2 * LANES)
    base = o + dl + al + gl
    put(o_attn, 0, base, 3 * aw, 3 * aw)
    put(o_conv, 0, base + 3 * aw, o_conv.shape[2], o_conv.shape[2])


def repack_w_in(w_in, rw, lora, aw, tm=128):
    depth, d, total = w_in.shape
    wr = 3 * rw + 4 * LANES
    wc = total - (3 * rw + sum(lora)) - 3 * aw
    spec = lambda w: pl.BlockSpec((1, tm, w), lambda l, i: (l, i, 0))
    return pl.pallas_call(
        functools.partial(_repack_w_in_kernel, rw=rw, lora=lora, aw=aw), grid=(depth, d // tm),
        in_specs=[spec(total)], out_specs=[spec(wr), spec(3 * aw), spec(wc)],
        out_shape=[jax.ShapeDtypeStruct((depth, d, w), BF16) for w in (wr, 3 * aw, wc)],
        compiler_params=_cparams("parallel", "parallel"), name="repack_w_in",
    )(w_in)


def kernel(x, p, norm_mix_g, w_in, rwkv_mu, rwkv_w0, rwkv_w2, rwkv_a0, rwkv_a2, rwkv_g2, rwkv_k_k, rwkv_k_a, rwkv_r_k, rwkv_lnx_g, rwkv_lnx_b, q_norm_g, k_norm_g, rel_bias, conv_w, conv_b, conv_ln_g, conv_ln_b, w_out, norm_ffn_g, router_c_w, router_c_b, router_f_w, router_f_b, expert_w_in, expert_w_out, ple_norm_g, ple_gate_w, ple_proj):
    batch, seq, d = x.shape
    depth = w_in.shape[0]
    n = batch * seq
    rw = rwkv_w0.shape[-1]
    lora = (rwkv_w2.shape[1], rwkv_a2.shape[1], rwkv_g2.shape[1])
    assert max(lora[0], lora[1]) <= LANES and lora[2] <= 2 * LANES
    rwkv_proj = 3 * rw + sum(lora)
    cw = conv_w.shape[-1]
    aw = d - rw - cw
    npat = len(DILATED_PATTERNS)
    hpp = aw // HEAD_DIM // npat
    row = lambda v: v.reshape(1, -1)

    biases = [_pattern_bias(rel_bias[:, gi * hpp:(gi + 1) * hpp], window, dil)
              for gi, (window, dil) in enumerate(DILATED_PATTERNS)]

    w_rwkv_all, w_attn_all, w_conv_all = repack_w_in(w_in, rw, lora, aw)
    w_out_all = w_out.astype(BF16)
    ple_gate_all = ple_gate_w.astype(BF16)
    ple_proj_all = ple_proj.astype(BF16)
    p_all = p.reshape(depth, n, -1).astype(BF16)
    mu_all = _pack_rwkv_cols(rwkv_mu[:, None, :], rw, lora)
    w2_all = _pad_rows(rwkv_w2, LANES)
    a2_all = _pad_rows(rwkv_a2, LANES)
    g2_all = _pad_rows(rwkv_g2, 2 * LANES)
    router_w_all = _pad_cols(jnp.concatenate([router_c_w, router_f_w], axis=-1), ROUTER_LANES)
    router_b_all = _pad_cols(jnp.concatenate([router_c_b, router_f_b], axis=-1)[:, None, :], ROUTER_LANES)

    h = x.reshape(n, d)
    for i in range(depth):
        xn = rmsnorm_cast(h, norm_mix_g[i])
        p_rwkv = matmul(xn, w_rwkv_all[i], name="proj_rwkv")
        p_attn = matmul(xn, w_attn_all[i], name="proj_attn")
        p_conv = matmul(xn, w_conv_all[i], name="proj_conv")

        r, lw, k, v, a, b, g, bonus = rwkv_prep(
            p_rwkv, mu_all[i], row(rwkv_w0[i]), w2_all[i], row(rwkv_a0[i]), a2_all[i], g2_all[i],
            row(rwkv_k_k[i]), row(rwkv_k_a[i]), row(rwkv_r_k[i]), seq)
        y = rwkv_scan(r, lw, k, v, a, b, batch, seq)
        y_rwkv = rwkv_post(y, bonus, g, row(rwkv_lnx_g[i]), row(rwkv_lnx_b[i]))

        outs, lses = [], []
        for gi, (window, dil) in enumerate(DILATED_PATTERNS):
            attn = attn_pattern if dil == 1 else attn_dilated
            o, lse = attn(p_attn, biases[gi], row(q_norm_g[i]), row(k_norm_g[i]), gi, dil, batch, seq, aw)
            outs.append(o)
            lses.append(lse)
        y_attn = attn_mix(outs, lses)

        y_conv = conv_module(p_conv, conv_w[i], row(conv_b[i]), row(conv_ln_g[i]), row(conv_ln_b[i]), seq)

        mix = jnp.concatenate([y_rwkv, y_attn, y_conv], axis=1)
        h = matmul(mix, w_out_all[i], residual=h, name="out_proj")

        info, counts, xp = route(h, norm_ffn_g[i], router_w_all[i], router_b_all[i])
        n_tiles = (2 * n) // MOE_TILE + N_EXPERTS
        d1, d2, tile_expert, n_used = moe_plan(info, counts, n_tiles)
        ys = moe_ffn(xp, expert_w_in[i], expert_w_out[i], d1, d2, tile_expert, n_used)
        h, xn = moe_combine(ys, h, info, ple_norm_g[i], d1, d2)

        h = ple(xn, ple_gate_all[i], p_all[i], ple_proj_all[i], h)
    return h.reshape(batch, seq, d)
```

```python
import functools
import math

import numpy as np
import jax
import jax.numpy as jnp
from jax import lax
from jax.experimental import pallas as pl
from jax.experimental.pallas import tpu as pltpu

F32 = jnp.float32
BF16 = jnp.bfloat16

HEAD_DIM = 64
DILATED_PATTERNS = ((128, 1), (512, 4), (2048, 16))
ATTN_BLOCK = 128
REL_BUCKETS = 32
REL_MAX_DIST = 2048
NEG_INF = -1e30
CONV_TAPS = 31
N_GROUPS = 8
EXPERTS_PER_GROUP = 8
N_EXPERTS = N_GROUPS * EXPERTS_PER_GROUP
RMS_EPS = 1e-6
LN_EPS = 1e-5
RWKV_LN_EPS = 64e-5

LANES = 128
SUBLANES = 8
VMEM_LIMIT_BYTES = 56 * 1024 * 1024

RWKV_CHUNK = 64
RWKV_HEADS_PER_STEP = 4
RWKV_WAVE = 4
RWKV_TBLK = 1024
ATTN_GROUP = 4
CONV_HALO = 32

NN = (((1,), (0,)), ((), ()))
NT = (((1,), (1,)), ((), ()))
TN = (((0,), (0,)), ((), ()))


def _cparams(*sem):
    return pltpu.CompilerParams(dimension_semantics=sem, vmem_limit_bytes=VMEM_LIMIT_BYTES)


def _dot(a, b, dims=NN):
    return lax.dot_general(a, b, dims, preferred_element_type=F32)


def _split(x):
    hi = x.astype(BF16)
    lo = (x - hi.astype(F32)).astype(BF16)
    return hi, lo


def _dot3(a, b, dims=NN):
    ah, al = _split(a)
    bh, bl = _split(b)
    return _dot(ah, bh, dims) + (_dot(ah, bl, dims) + _dot(al, bh, dims))


def _dot2(a, b_bf16, dims=NN):
    ah, al = _split(a)
    return _dot(ah, b_bf16, dims) + _dot(al, b_bf16, dims)


def _sigmoid(x):
    return 1.0 / (1.0 + jnp.exp(-x))


def _seg_matrix():
    r = lax.broadcasted_iota(jnp.int32, (LANES, LANES), 0) // HEAD_DIM
    c = lax.broadcasted_iota(jnp.int32, (LANES, LANES), 1) // HEAD_DIM
    return (r == c).astype(BF16)


def _head_sum(x, seg):
    w = x.shape[1]
    cols = [_dot2(x[:, j:j + LANES], seg) for j in range(0, w, LANES)]
    return cols[0] if len(cols) == 1 else jnp.concatenate(cols, axis=1)


def _rmsnorm_kernel(x_ref, g_ref, o_ref):
    x = x_ref[...]
    ms = jnp.mean(x * x, axis=-1, keepdims=True)
    o_ref[...] = (x * lax.rsqrt(ms + RMS_EPS) * g_ref[...]).astype(o_ref.dtype)


def rmsnorm_cast(x, g, tm=512):
    n, d = x.shape
    tm = min(tm, n)
    return pl.pallas_call(
        _rmsnorm_kernel,
        grid=(n // tm,),
        in_specs=[pl.BlockSpec((tm, d), lambda i: (i, 0)), pl.BlockSpec((1, d), lambda i: (0, 0))],
        out_specs=pl.BlockSpec((tm, d), lambda i: (i, 0)),
        out_shape=jax.ShapeDtypeStruct((n, d), BF16),
        compiler_params=_cparams("parallel"),
        name="rmsnorm",
    )(x, g.reshape(1, d))


def _mm_kernel(a_ref, w_ref, o_ref):
    o_ref[...] = _dot(a_ref[...], w_ref[...]).astype(o_ref.dtype)


def _mm_cat_res_kernel(*refs):
    *a_refs, w_ref, r_ref, o_ref = refs
    acc = r_ref[...]
    lo = 0
    for a_ref in a_refs:
        k = a_ref.shape[1]
        acc = acc + _dot(a_ref[...], w_ref[lo:lo + k, :])
        lo += k
    o_ref[...] = acc


def _pick(n, pref):
    for t in pref:
        if n % t == 0:
            return t
    return n


def matmul(a, w_all, layer, out_dtype=F32, residual=None, tm=1024, tn=None, name="matmul"):
    pieces = list(a) if isinstance(a, (list, tuple)) else [a]
    m = pieces[0].shape[0]
    k, n = w_all.shape[1:]
    assert sum(p_.shape[1] for p_ in pieces) == k
    tm = _pick(m, (tm, 512, 256, 128))
    tn = tn or _pick(n, (1024, 768, 512, 384, 256, 128))
    grid = (m // tm, n // tn)
    a_specs = [pl.BlockSpec((tm, p_.shape[1]), lambda i, j: (i, 0)) for p_ in pieces]
    w_spec = pl.BlockSpec((None, k, tn), lambda i, j: (layer, 0, j))
    o_spec = pl.BlockSpec((tm, tn), lambda i, j: (i, j))
    if residual is None:
        assert len(pieces) == 1
        return pl.pallas_call(
            _mm_kernel, grid=grid, in_specs=a_specs + [w_spec], out_specs=o_spec,
            out_shape=jax.ShapeDtypeStruct((m, n), out_dtype),
            compiler_params=_cparams("parallel", "arbitrary"), name=name,
        )(*pieces, w_all)
    return pl.pallas_call(
        _mm_cat_res_kernel, grid=grid, in_specs=a_specs + [w_spec, o_spec], out_specs=o_spec,
        out_shape=jax.ShapeDtypeStruct((m, n), F32),
        compiler_params=_cparams("parallel", "arbitrary"), name=name,
    )(*pieces, w_all, residual)


def _rwkv_prep_kernel(x_ref, prev_ref, mu_ref, w0_ref, w2_ref, a0_ref, a2_ref, g2_ref, kk_ref, ka_ref,
                      rk_ref, r_o, lw_o, k_o, v_o, a_o, b_o, g_o, bonus_o, *, seq, width):
    t = x_ref.shape[0]
    i = pl.program_id(0)
    x = x_ref[...]
    seq_start = (i * t) % seq == 0
    prev = jnp.where(seq_start, 0.0, prev_ref[SUBLANES - 1:SUBLANES, :])
    rows = lax.broadcasted_iota(jnp.int32, x.shape, 0)
    shifted = jnp.where(rows == 0, prev, pltpu.roll(x, 1, axis=0))
    x = x + (shifted - x) * mu_ref[...]
    w = width
    r = x[:, 0:w]
    k = x[:, w:2 * w]
    v = x[:, 2 * w:3 * w]
    o = 3 * w
    w_lo = x[:, o:o + LANES]
    a_lo = x[:, o + LANES:o + 2 * LANES]
    g_lo = x[:, o + 2 * LANES:o + 4 * LANES]
    seg = _seg_matrix()

    wv = w0_ref[...] + _dot3(jnp.tanh(w_lo), w2_ref[...])
    w_log = -(jnp.maximum(-wv, 0.0) + jnp.log(1.0 + jnp.exp(-jnp.abs(wv)))) - 0.5
    lw_o[...] = -jnp.exp(w_log)
    a = _sigmoid(a0_ref[...] + _dot3(a_lo, a2_ref[...]))
    g_o[...] = _dot3(_sigmoid(g_lo), g2_ref[...])

    kk = k * kk_ref[...]
    nrm = jnp.sqrt(_head_sum(kk * kk, seg))
    kk = kk / jnp.maximum(nrm, 1e-12)
    k = k * (1.0 + (a - 1.0) * ka_ref[...])
    r_o[...] = r
    k_o[...] = k
    v_o[...] = v
    a_o[...] = -kk
    b_o[...] = kk * a
    bonus_o[...] = _head_sum(r * k * rk_ref[...], seg) * v


def rwkv_prep(proj, mu, w0, w2, a0, a2, g2, k_k, k_a, r_k, seq, tm=128):
    n, wp = proj.shape
    width = w0.shape[-1]
    tm = min(tm, seq)
    row = lambda i: (i, 0)
    fix = lambda i: (0, 0)
    vec = pl.BlockSpec((1, width), fix)
    out = pl.BlockSpec((tm, width), row)
    per = tm // SUBLANES
    return pl.pallas_call(
        functools.partial(_rwkv_prep_kernel, seq=seq, width=width),
        grid=(n // tm,),
        in_specs=[pl.BlockSpec((tm, wp), row),
                  pl.BlockSpec((SUBLANES, wp), lambda i: (jnp.maximum(i * per - 1, 0), 0)),
                  pl.BlockSpec((1, wp), fix), vec, pl.BlockSpec(w2.shape, fix), vec,
                  pl.BlockSpec(a2.shape, fix), pl.BlockSpec(g2.shape, fix), vec, vec, vec],
        out_specs=[out] * 8,
        out_shape=[jax.ShapeDtypeStruct((n, width), F32)] * 8,
        compiler_params=_cparams("parallel"), name="rwkv_prep",
    )(proj, proj, mu, w0, w2, a0, a2, g2, k_k, k_a, r_k)


SCAN_PASSES = dict(gram=1, gkv=1, inv=1, t=1, r=1, y=1, m=1, s=3)


def _pdot(a, b, dims, passes):
    if passes == 1:
        return _dot(a.astype(BF16), b.astype(BF16), dims)
    return _dot3(a, b, dims)


def _rwkv_scan_kernel(r_ref, lw_ref, k_ref, v_ref, a_ref, b_ref, y_ref, st_ref, *, chunk, wave):
    c_len = chunk
    hd = HEAD_DIM
    nheads = r_ref.shape[1] // hd
    ps = SCAN_PASSES

    @pl.when(pl.program_id(2) == 0)
    def _():
        st_ref[...] = jnp.zeros_like(st_ref)

    nchunks = r_ref.shape[0] // c_len
    row2 = lax.broadcasted_iota(jnp.int32, (2 * c_len, 2 * c_len), 0)
    col2 = lax.broadcasted_iota(jnp.int32, (2 * c_len, 2 * c_len), 1) % c_len
    mask2 = col2 <= jnp.where(row2 < c_len, row2 - 1, row2 - c_len)
    rowc = lax.broadcasted_iota(jnp.int32, (c_len, 2 * c_len), 0)
    colc = lax.broadcasted_iota(jnp.int32, (c_len, 2 * c_len), 1)
    right = colc >= c_len
    eye_right = (colc - c_len == rowc).astype(F32)
    zeros_h = jnp.zeros((c_len, hd), F32)
    zeros_w = jnp.zeros((c_len, 2 * c_len), F32)
    eye_k = (lax.broadcasted_iota(jnp.int32, (hd, hd), 0)
             == lax.broadcasted_iota(jnp.int32, (hd, hd), 1)).astype(F32)
    ridx = lax.broadcasted_iota(jnp.int32, (c_len, r_ref.shape[1]), 0)

    def load_units(chunks):
        units = []
        for c in chunks:
            rows = slice(c * c_len, (c + 1) * c_len)
            lw = lw_ref[rows, :]
            cum = lw
            sh = 1
            while sh < c_len:
                cum = cum + jnp.where(ridx >= sh, pltpu.roll(cum, sh, axis=0), 0.0)
                sh *= 2
            total = cum[c_len - 1:c_len, :]
            e_neg = jnp.exp(-cum)
            e_end = jnp.exp(total - cum)
            rt = r_ref[rows, :] * jnp.exp(cum)
            at = a_ref[rows, :] * jnp.exp(cum - lw)
            b = b_ref[rows, :]
            k = k_ref[rows, :]
            bt, kt, bh, kh = b * e_neg, k * e_neg, b * e_end, k * e_end
            v = v_ref[rows, :]
            g_end = jnp.exp(total)
            for h in range(nheads):
                sl = slice(h * hd, (h + 1) * hd)
                units.append(dict(c=c, h=h, at=at[:, sl], rt=rt[:, sl], v=v[:, sl], bt=bt[:, sl], kt=kt[:, sl],
                                  bh=bh[:, sl], kh=kh[:, sl], g=g_end[:, sl]))
        return units

    def phase1(units):
        def below(x):
            return jnp.concatenate([zeros_h, x], axis=0)

        def above(x):
            return jnp.concatenate([x, zeros_h], axis=0)

        for u in units:
            l2 = jnp.concatenate([u["at"], u["rt"]], axis=0)
            bk = jnp.concatenate([u["bt"], u["kt"]], axis=0)
            u["gm"] = jnp.where(mask2, _pdot(l2, bk, NT, ps["gram"]), 0.0)
        yield
        for u in units:
            u["gkv"] = _pdot(u["gm"], below(u["v"]), NN, ps["gkv"])
            u["w"] = jnp.where(right, eye_right, u["gm"][:c_len])
        yield
        p = 1
        while p < c_len:
            for u in units:
                w = u["w"]
                u["w"] = _pdot(w, jnp.concatenate([w, zeros_w], axis=0), NN, ps["inv"]) + jnp.where(right, w, 0.0)
            yield
            p *= 2
        for u in units:
            u["ta"] = _pdot(u["w"], below(u["at"]), NN, ps["t"])
        yield
        for u in units:
            u["u0"] = _pdot(u["w"], below(u["gkv"][:c_len]), NN, ps["t"])
        yield
        for u in units:
            u["rhat"] = u["rt"] + _pdot(u["gm"][c_len:], above(u["ta"]), NN, ps["r"])
        yield
        for u in units:
            u["y1"] = _pdot(u["gm"][c_len:], above(u["u0"]), NN, ps["r"]) + u["gkv"][c_len:]
        yield
        for u in units:
            u["m"] = eye_k * u["g"] + _pdot(u["bh"], u["ta"], TN, ps["m"])
        yield
        for u in units:
            u["nm"] = _pdot(jnp.concatenate([u["bh"], u["kh"]], axis=0),
                            jnp.concatenate([u["u0"], u["v"]], axis=0), TN, ps["m"])
        yield

    state = [st_ref[h] for h in range(nheads)]

    def phase2_steps(units):
        by_chunk = {}
        for u in units:
            by_chunk.setdefault(u["c"], []).append(u)

        def step(c):
            ys = []
            for u in by_chunk[c]:
                h = u["h"]
                ys.append(_pdot(u["rhat"], state[h], NN, ps["y"]) + u["y1"])
                state[h] = _pdot(u["m"], state[h], NN, ps["s"]) + u["nm"]
            y_ref[c * c_len:(c + 1) * c_len, :] = jnp.concatenate(ys, axis=1)

        return [functools.partial(step, c) for c in sorted(by_chunk)]

    pending = []
    for w0 in range(0, nchunks, wave):
        units = load_units(range(w0, min(w0 + wave, nchunks)))
        for _ in phase1(units):
            if pending:
                pending.pop(0)()
        for step in pending:
            step()
        pending = phase2_steps(units)
    for step in pending:
        step()
    for h in range(nheads):
        st_ref[h] = state[h]


def rwkv_scan(r, lw, k, v, a, b, batch, seq, tblk=RWKV_TBLK):
    n, width = r.shape
    tblk = min(tblk, seq)
    nt = seq // tblk
    wl = RWKV_HEADS_PER_STEP * HEAD_DIM
    assert width % wl == 0, (width, wl)
    spec = pl.BlockSpec((tblk, wl), lambda bi, hp, tb: (bi * nt + tb, hp))
    return pl.pallas_call(
        functools.partial(_rwkv_scan_kernel, chunk=RWKV_CHUNK, wave=RWKV_WAVE),
        grid=(batch, width // wl, nt),
        in_specs=[spec] * 6,
        out_specs=spec,
        out_shape=jax.ShapeDtypeStruct((n, width), F32),
        scratch_shapes=[pltpu.VMEM((RWKV_HEADS_PER_STEP, HEAD_DIM, HEAD_DIM), F32)],
        compiler_params=_cparams("parallel", "parallel", "arbitrary"), name="rwkv_scan",
    )(r, lw, k, v, a, b)


def _rwkv_post_kernel(y_ref, bonus_ref, g_ref, lg_ref, lb_ref, o_ref):
    seg = _seg_matrix()
    y = y_ref[...]
    inv_n = 1.0 / HEAD_DIM
    mean = _head_sum(y, seg) * inv_n
    yc = y - mean
    var = _head_sum(yc * yc, seg) * inv_n
    yn = yc * lax.rsqrt(var + RWKV_LN_EPS) * lg_ref[...] + lb_ref[...]
    o_ref[...] = ((yn + bonus_ref[...]) * g_ref[...]).astype(o_ref.dtype)


def rwkv_post(y, bonus, g, lnx_g, lnx_b, tm=256):
    n, width = y.shape
    tm = _pick(n, (tm, 128))
    row = pl.BlockSpec((tm, width), lambda i: (i, 0))
    vec = pl.BlockSpec((1, width), lambda i: (0, 0))
    return pl.pallas_call(
        _rwkv_post_kernel, grid=(n // tm,),
        in_specs=[row, row, row, vec, vec], out_specs=row,
        out_shape=jax.ShapeDtypeStruct((n, width), BF16),
        compiler_params=_cparams("parallel"), name="rwkv_post",
    )(y, bonus, g, lnx_g, lnx_b)


def _t5_causal_bucket(dist):
    max_exact = REL_BUCKETS // 2
    d_f = jnp.maximum(dist, 1).astype(F32)
    large = max_exact + (jnp.log(d_f / max_exact) / math.log(REL_MAX_DIST / max_exact)
                         * (REL_BUCKETS - max_exact)).astype(jnp.int32)
    large = jnp.minimum(large, REL_BUCKETS - 1)
    return jnp.where(dist < max_exact, dist, large)


def _pattern_bias(rel_bias_g, window, dil):
    span = window // dil
    qi = np.arange(ATTN_BLOCK)[:, None]
    ki = np.arange(2 * ATTN_BLOCK)[None, :]
    off = qi + ATTN_BLOCK - ki
    band = (off >= 0) & (off <= span)
    bucket = _t5_causal_bucket(jnp.asarray(np.clip(off, 0, span) * dil, jnp.int32))
    onehot = (bucket[:, :, None] == jnp.arange(REL_BUCKETS)[None, None, :]).astype(F32)
    bias = jnp.einsum("qkb,bh->hqk", onehot, rel_bias_g.astype(F32), precision=lax.Precision.HIGHEST)
    return jnp.where(jnp.asarray(band)[None], bias, NEG_INF)


def _attn_blocks(blocks, bias_ref, qg, kg, no_prev):
    hd = HEAD_DIM
    scale = hd ** -0.5
    seg = _seg_matrix()
    ones_k = jnp.ones((2 * ATTN_BLOCK, LANES), BF16)
    head_of_lane = lax.broadcasted_iota(jnp.int32, (ATTN_BLOCK, LANES), 1) // hd

    def norm(x, g):
        ms = _dot((x * x).astype(BF16), seg) * (1.0 / hd)
        return x * lax.rsqrt(ms + RMS_EPS) * g

    pairs = []
    for bi, (q, kp, kc, vp, vc) in enumerate(blocks):
        for c in range(q.shape[1] // LANES):
            sl = slice(c * LANES, (c + 1) * LANES)
            pairs.append(dict(b=bi, c=c, q=q[:, sl], k=jnp.concatenate([kp[:, sl], kc[:, sl]], axis=0),
                              v=jnp.concatenate([vp[:, sl], vc[:, sl]], axis=0)))
    for pr in pairs:
        pr["qn"] = norm(pr["q"], qg)
        pr["kn"] = norm(pr["k"], kg).astype(BF16)
        pr["vb"] = pr["v"].astype(BF16)
    units = [dict(pr=pr, h=h) for pr in pairs for h in range(LANES // hd)]
    for u in units:
        pr = u["pr"]
        qm = jnp.where(head_of_lane == u["h"], pr["qn"], 0.0).astype(BF16)
        s = _dot(qm, pr["kn"], NT) * scale + bias_ref[pr["c"] * (LANES // hd) + u["h"]]
        u["s"] = jnp.where(no_prev, NEG_INF, s)
    for u in units:
        u["m"] = jnp.max(u["s"], axis=-1, keepdims=True)
    for u in units:
        u["p"] = jnp.exp(u["s"] - u["m"]).astype(BF16)
    for u in units:
        u["den"] = _dot(u["p"], ones_k)
    for u in units:
        u["o"] = _dot(u["p"], u["pr"]["vb"]) / u["den"]
        u["lse"] = u["m"] + jnp.log(u["den"])
    res = []
    for bi in range(len(blocks)):
        outs, lses = [], []
        for pr in (p_ for p_ in pairs if p_["b"] == bi):
            u0, u1 = [u for u in units if u["pr"] is pr]
            outs.append(jnp.where(head_of_lane == 0, u0["o"], u1["o"]))
            lses.append(jnp.where(head_of_lane == 0, u0["lse"], u1["lse"]))
        res.append((outs[0] if len(outs) == 1 else jnp.concatenate(outs, axis=1),
                    lses[0] if len(lses) == 1 else jnp.concatenate(lses, axis=1)))
    return res


def _no_prev_mask():
    ki = lax.broadcasted_iota(jnp.int32, (ATTN_BLOCK, 2 * ATTN_BLOCK), 1)
    return jnp.logical_and(pl.program_id(1) == 0, ki < ATTN_BLOCK)


def _attn_kernel(q_ref, kp_ref, kc_ref, vp_ref, vc_ref, bias_ref, qg_ref, kg_ref, o_ref, lse_ref):
    (o, lse), = _attn_blocks([(q_ref[...], kp_ref[...], kc_ref[...], vp_ref[...], vc_ref[...])], bias_ref,
                             qg_ref[...], kg_ref[...], _no_prev_mask())
    o_ref[...] = o
    lse_ref[...] = lse


def _attn_dilated_kernel(q_ref, kp_ref, kc_ref, vp_ref, vc_ref, bias_ref, qg_ref, kg_ref, o_ref, lse_ref, *, dil):
    no_prev = _no_prev_mask()
    group = min(dil, ATTN_GROUP)

    def body(rg, carry):
        rows = [pl.ds(rg * group + j, ATTN_BLOCK, stride=dil) for j in range(group)]
        blocks = [(q_ref[rw, :], kp_ref[rw, :], kc_ref[rw, :], vp_ref[rw, :], vc_ref[rw, :]) for rw in rows]
        for rw, (o, lse) in zip(rows, _attn_blocks(blocks, bias_ref, qg_ref[...], kg_ref[...], no_prev)):
            o_ref[rw, :] = o
            lse_ref[rw, :] = lse
        return carry

    lax.fori_loop(0, dil // group, body, 0)


def attn_dilated(proj, bias, q_g, k_g, gi, dil, batch, seq, width):
    n = proj.shape[0]
    npat = len(DILATED_PATTERNS)
    pw = width // npat
    rows = ATTN_BLOCK * dil
    nb = seq // rows
    sec = width // LANES
    hp_per = pw // LANES
    heads_pp = LANES // HEAD_DIM

    def spec(section, prev):
        def imap(b, nblk, hp):
            blk = jnp.maximum(nblk - 1, 0) if prev else nblk
            return (b * nb + blk, section * sec + gi * hp_per + hp)
        return pl.BlockSpec((rows, LANES), imap)

    bias_spec = pl.BlockSpec((heads_pp,) + bias.shape[1:], lambda b, nblk, hp: (hp, 0, 0))
    vec = pl.BlockSpec((1, LANES), lambda b, nblk, hp: (0, 0))
    q_g, k_g = (jnp.tile(g, (1, heads_pp)) for g in (q_g, k_g))
    ospec = pl.BlockSpec((rows, LANES), lambda b, nblk, hp: (b * nb + nblk, hp))
    return pl.pallas_call(
        functools.partial(_attn_dilated_kernel, dil=dil), grid=(batch, nb, hp_per),
        in_specs=[spec(0, False), spec(1, True), spec(1, False), spec(2, True), spec(2, False), bias_spec, vec, vec],
        out_specs=[ospec, ospec],
        out_shape=[jax.ShapeDtypeStruct((n, pw), F32)] * 2,
        compiler_params=_cparams("parallel", "parallel", "parallel"), name=f"attn_d{dil}",
    )(proj, proj, proj, proj, proj, bias, q_g, k_g)


def attn_pattern(proj, bias, q_g, k_g, gi, dil, batch, seq, width):
    n = proj.shape[0]
    npat = len(DILATED_PATTERNS)
    pw = width // npat
    nsub = seq // dil
    nb = nsub // ATTN_BLOCK
    sec = width // pw
    rowlen = 3 * sec
    x = proj.reshape(batch * nsub, dil * 3 * width)

    def spec(section, prev):
        def imap(b, nblk, r):
            blk = jnp.maximum(nblk - 1, 0) if prev else nblk
            return (b * nb + blk, r * rowlen + section * sec + gi)
        return pl.BlockSpec((ATTN_BLOCK, pw), imap)

    fix3 = pl.BlockSpec(bias.shape, lambda b, nblk, r: (0, 0, 0))
    vec = pl.BlockSpec((1, LANES), lambda b, nblk, r: (0, 0))
    q_g, k_g = (jnp.tile(g, (1, LANES // HEAD_DIM)) for g in (q_g, k_g))
    ospec = pl.BlockSpec((ATTN_BLOCK, pw), lambda b, nblk, r: (b * nb + nblk, r))
    o, lse = pl.pallas_call(
        _attn_kernel, grid=(batch, nb, dil),
        in_specs=[spec(0, False), spec(1, True), spec(1, False), spec(2, True), spec(2, False), fix3, vec, vec],
        out_specs=[ospec, ospec],
        out_shape=[jax.ShapeDtypeStruct((batch * nsub, dil * pw), F32)] * 2,
        compiler_params=_cparams("parallel", "parallel", "parallel"), name=f"attn_d{dil}",
    )(x, x, x, x, x, bias, q_g, k_g)
    return o.reshape(n, pw), lse.reshape(n, pw)


def _attn_mix_kernel(o0, o1, o2, l0, l1, l2, out_ref):
    a, b, c = l0[...], l1[...], l2[...]
    m = jnp.maximum(jnp.maximum(a, b), c)
    ea, eb, ec = jnp.exp(a - m), jnp.exp(b - m), jnp.exp(c - m)
    inv = 1.0 / (ea + eb + ec)
    out_ref[...] = jnp.concatenate([o0[...] * (ea * inv), o1[...] * (eb * inv), o2[...] * (ec * inv)],
                                   axis=1).astype(out_ref.dtype)


def attn_mix(outs, lses, tm=512):
    n, pw = outs[0].shape
    tm = _pick(n, (tm, 256, 128))
    row = pl.BlockSpec((tm, pw), lambda i: (i, 0))
    return pl.pallas_call(
        _attn_mix_kernel, grid=(n // tm,), in_specs=[row] * 6,
        out_specs=pl.BlockSpec((tm, 3 * pw), lambda i: (i, 0)),
        out_shape=jax.ShapeDtypeStruct((n, 3 * pw), BF16),
        compiler_params=_cparams("parallel"), name="attn_mix",
    )(*outs, *lses)


def _conv_kernel(x_ref, halo_ref, w_ref, b_ref, lg_ref, lb_ref, o_ref, ext_ref, shift_ref, *, seq, width):
    t = x_ref.shape[0]
    i = pl.program_id(0)
    seq_start = (i * t) % seq == 0

    def glu(x):
        return x[:, :width] * _sigmoid(x[:, width:])

    ext_ref[0:CONV_HALO, :] = jnp.where(seq_start, 0.0, glu(halo_ref[...]))
    ext_ref[CONV_HALO:, :] = glu(x_ref[...])
    base = CONV_HALO - (CONV_TAPS - 1)
    acc = jnp.broadcast_to(b_ref[...], (t, width))
    for rho in range(SUBLANES):
        offs = [base + j for j in range(CONV_TAPS) if (base + j) % SUBLANES == rho]
        if not offs:
            continue
        lo = offs[0]
        span = offs[-1] - lo + t
        shift_ref[0:span, :] = ext_ref[lo:lo + span, :]
        for o in offs:
            acc = acc + w_ref[o - base:o - base + 1, :] * shift_ref[o - lo:o - lo + t, :]
    mu = jnp.mean(acc, axis=-1, keepdims=True)
    d = acc - mu
    var = jnp.mean(d * d, axis=-1, keepdims=True)
    y = d * lax.rsqrt(var + LN_EPS) * lg_ref[...] + lb_ref[...]
    o_ref[...] = (y * _sigmoid(y)).astype(o_ref.dtype)


def conv_module(proj, conv_w, conv_b, ln_g, ln_b, seq, tm=256):
    n, w2 = proj.shape
    width = w2 // 2
    tm = min(tm, seq)
    per = tm // CONV_HALO
    fix = lambda i: (0, 0)
    vec = pl.BlockSpec((1, width), fix)
    return pl.pallas_call(
        functools.partial(_conv_kernel, seq=seq, width=width), grid=(n // tm,),
        in_specs=[pl.BlockSpec((tm, w2), lambda i: (i, 0)),
                  pl.BlockSpec((CONV_HALO, w2), lambda i: (jnp.maximum(i * per - 1, 0), 0)),
                  pl.BlockSpec((CONV_TAPS, width), fix), vec, vec, vec],
        out_specs=pl.BlockSpec((tm, width), lambda i: (i, 0)),
        out_shape=jax.ShapeDtypeStruct((n, width), BF16),
        scratch_shapes=[pltpu.VMEM((tm + CONV_HALO, width), F32), pltpu.VMEM((tm + CONV_HALO, width), F32)],
        compiler_params=_cparams("parallel"), name="conv_module",
    )(proj, proj, conv_w, conv_b, ln_g, ln_b)


ROUTER_LANES = LANES


def _pack_bf16_pairs(x):
    w = x.shape[1] // 2
    hi = lax.bitcast_convert_type(x[:, :w].astype(BF16).astype(F32), jnp.uint32)
    lo = lax.bitcast_convert_type(x[:, w:].astype(BF16).astype(F32), jnp.uint32)
    return hi | (lo >> 16)


def _unpack_bf16_pairs(words):
    hi = lax.bitcast_convert_type(words & jnp.uint32(0xFFFF0000), F32)
    lo = lax.bitcast_convert_type(words << 16, F32)
    return hi, lo


def _store_token_major(ref, words):
    t = words.shape[0]
    chunks = words.shape[1] // LANES
    for c in range(chunks):
        ref[pl.ds(c, t, stride=chunks), :] = words[:, c * LANES:(c + 1) * LANES]


def _load_token_major(ref, t, chunks):
    return jnp.concatenate([ref[pl.ds(c, t, stride=chunks), :] for c in range(chunks)], axis=1)


def _route_kernel(x_ref, g_ref, rw_ref, bias_ref, info_ref, cnt_ref, xp_ref, run_ref):
    t = x_ref.shape[0]

    @pl.when(pl.program_id(0) == 0)
    def _():
        run_ref[...] = jnp.zeros_like(run_ref)

    x = x_ref[...]
    ms = jnp.mean(x * x, axis=-1, keepdims=True)
    xn = x * lax.rsqrt(ms + RMS_EPS) * g_ref[...]
    _store_token_major(xp_ref, _pack_bf16_pairs(xn))
    z = _dot3(xn, rw_ref[...]) + bias_ref[...]
    lane = lax.broadcasted_iota(jnp.int32, z.shape, 1).astype(F32)
    ninf = -jnp.inf
    big = float(ROUTER_LANES)
    zc = jnp.where(lane < N_GROUPS, z, ninf)
    mc = jnp.max(zc, axis=-1, keepdims=True)
    pg = 1.0 / jnp.sum(jnp.exp(zc - mc), axis=-1, keepdims=True)
    g_idx = jnp.min(jnp.where(zc == mc, lane, big), axis=-1, keepdims=True)
    lo = N_GROUPS + g_idx * EXPERTS_PER_GROUP
    zf = jnp.where(jnp.logical_and(lane >= lo, lane < lo + EXPERTS_PER_GROUP), z, ninf)
    m1 = jnp.max(zf, axis=-1, keepdims=True)
    i1 = jnp.min(jnp.where(zf == m1, lane, big), axis=-1, keepdims=True)
    zf2 = jnp.where(lane == i1, ninf, zf)
    m2 = jnp.max(zf2, axis=-1, keepdims=True)
    i2 = jnp.min(jnp.where(zf2 == m2, lane, big), axis=-1, keepdims=True)
    e2 = jnp.exp(m2 - m1)
    w1 = pg / (1.0 + e2)
    w2 = pg * e2 / (1.0 + e2)
    onehot = jnp.where(jnp.logical_or(lane == i1, lane == i2), 1.0, 0.0)
    tri = (lax.broadcasted_iota(jnp.int32, (t, t), 0) > lax.broadcasted_iota(jnp.int32, (t, t), 1)).astype(BF16)
    before = _dot(tri, onehot.astype(BF16)) + run_ref[...]
    r1 = jnp.sum(jnp.where(lane == i1, before, 0.0), axis=-1, keepdims=True)
    r2 = jnp.sum(jnp.where(lane == i2, before, 0.0), axis=-1, keepdims=True)
    run_ref[...] += jnp.sum(onehot, axis=0, keepdims=True)
    cnt_ref[...] = run_ref[...]
    cols = (i1 - N_GROUPS, i2 - N_GROUPS, w1, w2, r1, r2)
    info = jnp.zeros_like(z)
    for j, c in enumerate(cols):
        info = jnp.where(lane == float(j), c, info)
    info_ref[...] = info


def route(x, g, router_w, bias, tm=256):
    n, d = x.shape
    w = router_w.shape[1]
    chunks = d // 2 // LANES
    tm = _pick(n, (tm, 128))
    fix = lambda i: (0, 0)
    row = pl.BlockSpec((tm, w), lambda i: (i, 0))
    one = pl.BlockSpec((1, w), fix)
    return pl.pallas_call(
        _route_kernel, grid=(n // tm,),
        in_specs=[pl.BlockSpec((tm, d), lambda i: (i, 0)), pl.BlockSpec((1, d), fix), pl.BlockSpec((d, w), fix), one],
        out_specs=[row, one, pl.BlockSpec((tm * chunks, LANES), lambda i: (i, 0))],
        out_shape=[jax.ShapeDtypeStruct((n, w), F32), jax.ShapeDtypeStruct((1, w), F32),
                   jax.ShapeDtypeStruct((n * chunks, LANES), jnp.uint32)],
        scratch_shapes=[pltpu.VMEM((1, w), F32)],
        compiler_params=_cparams("arbitrary"), name="route",
    )(x, g.reshape(1, d), router_w, bias)


MOE_TILE = 256


def _moe_dest_kernel(info_ref, off_ref, dd_ref):
    info = info_ref[...]
    lane = lax.broadcasted_iota(jnp.int32, info.shape, 1).astype(F32)
    off = off_ref[...]

    def dest(e, r):
        return jnp.sum(jnp.where(lane == e + N_GROUPS, off, 0.0), axis=-1, keepdims=True) + r

    d1 = dest(info[:, 0:1], info[:, 4:5])
    d2 = dest(info[:, 1:2], info[:, 5:6])
    dd_ref[...] = jnp.where(lane == 0.0, d1, jnp.where(lane == 1.0, d2, 0.0)).astype(jnp.int32)


def moe_plan(info, counts, n_tiles, tm=512):
    n, w = info.shape
    cnt = counts[0, N_GROUPS:N_GROUPS + N_EXPERTS].astype(jnp.int32)
    padded = (cnt + MOE_TILE - 1) // MOE_TILE * MOE_TILE
    ends = jnp.cumsum(padded)
    off_row = jnp.pad((ends - padded).astype(F32), (N_GROUPS, w - N_GROUPS - N_EXPERTS)).reshape(1, w)
    tm = _pick(n, (tm, 256, 128))
    row = pl.BlockSpec((tm, w), lambda i: (i, 0))
    dd = pl.pallas_call(
        _moe_dest_kernel, grid=(n // tm,),
        in_specs=[row, pl.BlockSpec((1, w), lambda i: (0, 0))], out_specs=row,
        out_shape=jax.ShapeDtypeStruct((n, w), jnp.int32),
        compiler_params=_cparams("parallel"), name="moe_dest",
    )(info, off_row)
    tile_start = jnp.arange(n_tiles, dtype=jnp.int32) * MOE_TILE
    tile_expert = jnp.minimum(jnp.sum((tile_start[:, None] >= ends[None, :]).astype(jnp.int32), axis=1),
                              N_EXPERTS - 1)
    n_used = (ends[-1:] // MOE_TILE).astype(jnp.int32)
    return dd[:, 0], dd[:, 1], tile_expert, n_used


DMA_ISSUE_UNROLL = 8


def _moe_ffn_kernel(d1_ref, d2_ref, texp_ref, nused_ref, xp_hbm, wi_ref, wo_ref, ys_ref,
                    src_ref, xbuf, sem, *, n_tok, ff):
    i = pl.program_id(0)
    n_used = nused_ref[0]
    n_rows = src_ref.shape[0]
    half = wi_ref.shape[1] // 2
    chunks = half // LANES

    def gather(tile, slot):
        def body(j, carry):
            tok = src_ref[tile * MOE_TILE + j]
            pltpu.make_async_copy(xp_hbm.at[pl.ds(tok * chunks, chunks), :],
                                  xbuf.at[slot, pl.ds(j * chunks, chunks), :], sem.at[slot]).start()
            return carry
        lax.fori_loop(0, MOE_TILE, body, 0, unroll=DMA_ISSUE_UNROLL)

    @pl.when(i == 0)
    def _():
        def clear(j, carry):
            src_ref[j] = 0
            return carry
        lax.fori_loop(0, n_rows, clear, 0, unroll=DMA_ISSUE_UNROLL)

        def fill(tok, carry):
            src_ref[d1_ref[tok]] = tok
            src_ref[d2_ref[tok]] = tok
            return carry
        lax.fori_loop(0, n_tok, fill, 0, unroll=DMA_ISSUE_UNROLL)
        gather(0, 0)

    @pl.when(i < n_used)
    def _():
        slot = i % 2

        @pl.when(i + 1 < n_used)
        def _():
            gather(i + 1, 1 - slot)

        pltpu.make_async_copy(xp_hbm.at[pl.ds(0, MOE_TILE * chunks), :], xbuf.at[slot], sem.at[slot]).wait()
        x_a, x_b = _unpack_bf16_pairs(_load_token_major(xbuf.at[slot], MOE_TILE, chunks))
        gu = (_dot(x_a.astype(BF16), wi_ref[0, :half, :].astype(BF16))
              + _dot(x_b.astype(BF16), wi_ref[0, half:, :].astype(BF16)))
        gt, up = gu[:, :ff], gu[:, ff:]
        hid = gt * _sigmoid(gt) * up
        _store_token_major(ys_ref, _pack_bf16_pairs(_dot(hid.astype(BF16), wo_ref[0].astype(BF16))))

    @pl.when(i >= n_used)
    def _():
        ys_ref[...] = jnp.zeros_like(ys_ref)


def moe_ffn(xp, w_in_all, w_out_all, layer, d1, d2, tile_expert, n_used):
    w_in, w_out = w_in_all, w_out_all
    _, ne, d, ff2 = w_in.shape
    chunks = d // 2 // LANES
    n = xp.shape[0] // chunks
    n_tiles = tile_expert.shape[0]
    n_rows = n_tiles * MOE_TILE
    last = lambda i, nu: jnp.minimum(i, nu[0] - 1)
    grid_spec = pltpu.PrefetchScalarGridSpec(
        num_scalar_prefetch=4, grid=(n_tiles,),
        in_specs=[pl.BlockSpec(memory_space=pl.ANY),
                  pl.BlockSpec((None, 1, d, ff2), lambda i, d1, d2, te, nu: (layer, te[last(i, nu)], 0, 0)),
                  pl.BlockSpec((None, 1, ff2 // 2, d), lambda i, d1, d2, te, nu: (layer, te[last(i, nu)], 0, 0))],
        out_specs=pl.BlockSpec((MOE_TILE * chunks, LANES), lambda i, d1, d2, te, nu: (i, 0)),
        scratch_shapes=[pltpu.SMEM((n_rows,), jnp.int32), pltpu.VMEM((2, MOE_TILE * chunks, LANES), jnp.uint32),
                        pltpu.SemaphoreType.DMA((2,))])
    return pl.pallas_call(
        functools.partial(_moe_ffn_kernel, n_tok=n, ff=ff2 // 2), grid_spec=grid_spec,
        out_shape=jax.ShapeDtypeStruct((n_rows * chunks, LANES), jnp.uint32),
        compiler_params=_cparams("arbitrary"), name="moe_ffn",
    )(d1, d2, tile_expert, n_used, xp, w_in, w_out)


def _moe_combine_kernel(d1_ref, d2_ref, ys_hbm, h_ref, info_ref, g_ref, o_ref, xn_ref, ybuf, sem):
    t = h_ref.shape[0]
    half = h_ref.shape[1] // 2
    chunks = half // LANES
    base = pl.program_id(0) * t

    def issue(j, carry):
        dst = pl.ds(j * chunks, chunks)
        pltpu.make_async_copy(ys_hbm.at[pl.ds(d1_ref[base + j] * chunks, chunks), :], ybuf.at[0, dst, :], sem.at[0]).start()
        pltpu.make_async_copy(ys_hbm.at[pl.ds(d2_ref[base + j] * chunks, chunks), :], ybuf.at[1, dst, :], sem.at[1]).start()
        return carry
    lax.fori_loop(0, t, issue, 0, unroll=DMA_ISSUE_UNROLL)
    for s in range(2):
        pltpu.make_async_copy(ys_hbm.at[pl.ds(0, t * chunks), :], ybuf.at[s], sem.at[s]).wait()
    info = info_ref[...]
    w1, w2 = info[:, 2:3], info[:, 3:4]
    y1a, y1b = _unpack_bf16_pairs(_load_token_major(ybuf.at[0], t, chunks))
    y2a, y2b = _unpack_bf16_pairs(_load_token_major(ybuf.at[1], t, chunks))
    h_a = h_ref[:, :half] + w1 * y1a + w2 * y2a
    h_b = h_ref[:, half:] + w1 * y1b + w2 * y2b
    o_ref[:, :half] = h_a
    o_ref[:, half:] = h_b
    ms = (jnp.sum(h_a * h_a, axis=-1, keepdims=True) + jnp.sum(h_b * h_b, axis=-1, keepdims=True)) / (2 * half)
    scale = lax.rsqrt(ms + RMS_EPS)
    xn_ref[:, :half] = (h_a * scale * g_ref[:, :half]).astype(xn_ref.dtype)
    xn_ref[:, half:] = (h_b * scale * g_ref[:, half:]).astype(xn_ref.dtype)


def moe_combine(ys, h, info, g_next, d1, d2, tm=256):
    n, d = h.shape
    tm = _pick(n, (tm, 128))
    row = lambda i, d1, d2: (i, 0)
    tile = pl.BlockSpec((tm, d), row)
    grid_spec = pltpu.PrefetchScalarGridSpec(
        num_scalar_prefetch=2, grid=(n // tm,),
        in_specs=[pl.BlockSpec(memory_space=pl.ANY), tile, pl.BlockSpec((tm, info.shape[1]), row),
                  pl.BlockSpec((1, d), lambda i, d1, d2: (0, 0))],
        out_specs=[tile, tile],
        scratch_shapes=[pltpu.VMEM((2, tm * (d // 2 // LANES), LANES), jnp.uint32), pltpu.SemaphoreType.DMA((2,))])
    return pl.pallas_call(
        _moe_combine_kernel, grid_spec=grid_spec,
        out_shape=[jax.ShapeDtypeStruct((n, d), F32), jax.ShapeDtypeStruct((n, d), BF16)],
        compiler_params=_cparams("arbitrary"), name="moe_combine",
    )(d1, d2, ys, h, info, g_next.reshape(1, d))


def _ple_kernel(a_ref, w_ref, p_ref, pw_ref, h_ref, o_ref):
    gate = _sigmoid(_dot(a_ref[...], w_ref[...]))
    o_ref[...] = h_ref[...] + _dot(p_ref[...], pw_ref[...]) * gate


def ple(xn, gate_w_all, p_all, proj_w_all, h, layer, tm=1024, tn=512):
    n, d = xn.shape
    pd = p_all.shape[2]
    tm = _pick(n, (tm, 512, 256, 128))
    tn = _pick(d, (tn, 256, 128))
    tile = pl.BlockSpec((tm, tn), lambda i, j: (i, j))
    gate_w, p, proj_w = gate_w_all, p_all, proj_w_all
    return pl.pallas_call(
        _ple_kernel, grid=(n // tm, d // tn),
        in_specs=[pl.BlockSpec((tm, d), lambda i, j: (i, 0)), pl.BlockSpec((None, d, tn), lambda i, j: (layer, 0, j)),
                  pl.BlockSpec((None, tm, pd), lambda i, j: (layer, i, 0)),
                  pl.BlockSpec((None, pd, tn), lambda i, j: (layer, 0, j)), tile],
        out_specs=tile,
        out_shape=jax.ShapeDtypeStruct((n, d), F32),
        compiler_params=_cparams("parallel", "arbitrary"), name="ple",
    )(xn, gate_w, p, proj_w, h)


def _pad_cols(x, total):
    return jnp.pad(x, ((0, 0),) * (x.ndim - 1) + ((0, total - x.shape[-1]),))


def _pad_rows(x, total):
    return jnp.pad(x, ((0, 0),) * (x.ndim - 2) + ((0, total - x.shape[-2]), (0, 0)))


def _pack_rwkv_cols(x, width, lora):
    dl, al, gl = lora
    o = 3 * width
    return jnp.concatenate([
        x[..., :o], _pad_cols(x[..., o:o + dl], LANES), _pad_cols(x[..., o + dl:o + dl + al], LANES),
        _pad_cols(x[..., o + dl + al:o + dl + al + gl], 2 * LANES)], axis=-1)


def _transpose_cast_kernel(off_ref, valid_ref, w_ref, o_ref):
    j = pl.program_id(1)
    x = w_ref[0]
    rows = lax.broadcasted_iota(jnp.int32, x.shape, 0)
    o_ref[0] = jnp.where(rows < valid_ref[j], x, 0.0).T.astype(o_ref.dtype)


def repack_w_in(w_in, rw, lora, aw):
    depth, d, total = w_in.shape
    dl, al, gl = lora
    w_t = jnp.transpose(w_in, (0, 2, 1))
    o = 3 * rw
    base = o + dl + al + gl
    rwkv_blocks = [(c, LANES) for c in range(0, o, LANES)] + [(o, dl), (o + dl, al)]
    rwkv_blocks += [(o + dl + al + c, max(0, min(LANES, gl - c))) for c in range(0, 2 * LANES, LANES)]
    attn_blocks = [(base + c, LANES) for c in range(0, 3 * aw, LANES)]
    conv_blocks = [(base + 3 * aw + c, LANES) for c in range(0, total - base - 3 * aw, LANES)]
    outs = []
    for name, blocks in (("rwkv", rwkv_blocks), ("attn", attn_blocks), ("conv", conv_blocks)):
        assert all(off % SUBLANES == 0 and off + LANES <= total and 0 <= v <= LANES for off, v in blocks), blocks
        offs = jnp.asarray([b_[0] // SUBLANES for b_ in blocks], jnp.int32)
        valid = jnp.asarray([b_[1] for b_ in blocks], jnp.int32)
        grid_spec = pltpu.PrefetchScalarGridSpec(
            num_scalar_prefetch=2, grid=(depth, len(blocks)),
            in_specs=[pl.BlockSpec((pl.Element(1), pl.Element(LANES), pl.Element(d)),
                                   lambda l, j, off, val: (l, pl.multiple_of(off[j] * SUBLANES, SUBLANES), 0))],
            out_specs=pl.BlockSpec((1, d, LANES), lambda l, j, off, val: (l, 0, j)))
        outs.append(pl.pallas_call(
            _transpose_cast_kernel, grid_spec=grid_spec,
            out_shape=jax.ShapeDtypeStruct((depth, d, len(blocks) * LANES), BF16),
            compiler_params=_cparams("parallel", "parallel"), name=f"repack_w_{name}",
        )(offs, valid, w_t))
    return outs


def kernel(x, p, norm_mix_g, w_in, rwkv_mu, rwkv_w0, rwkv_w2, rwkv_a0, rwkv_a2, rwkv_g2, rwkv_k_k, rwkv_k_a, rwkv_r_k, rwkv_lnx_g, rwkv_lnx_b, q_norm_g, k_norm_g, rel_bias, conv_w, conv_b, conv_ln_g, conv_ln_b, w_out, norm_ffn_g, router_c_w, router_c_b, router_f_w, router_f_b, expert_w_in, expert_w_out, ple_norm_g, ple_gate_w, ple_proj):
    batch, seq, d = x.shape
    depth = w_in.shape[0]
    n = batch * seq
    rw = rwkv_w0.shape[-1]
    lora = (rwkv_w2.shape[1], rwkv_a2.shape[1], rwkv_g2.shape[1])
    assert max(lora[0], lora[1]) <= LANES and lora[2] <= 2 * LANES
    rwkv_proj = 3 * rw + sum(lora)
    cw = conv_w.shape[-1]
    aw = d - rw - cw
    npat = len(DILATED_PATTERNS)
    hpp = aw // HEAD_DIM // npat
    row = lambda v: v.reshape(1, -1)

    biases = [_pattern_bias(rel_bias[:, gi * hpp:(gi + 1) * hpp], window, dil)
              for gi, (window, dil) in enumerate(DILATED_PATTERNS)]

    w_rwkv_all, w_attn_all, w_conv_all = repack_w_in(w_in, rw, lora, aw)
    w_out_all = w_out.astype(BF16)
    ple_gate_all = ple_gate_w.astype(BF16)
    ple_proj_all = ple_proj.astype(BF16)
    p_all = p.reshape(depth, n, -1).astype(BF16)
    mu_all = _pack_rwkv_cols(rwkv_mu[:, None, :], rw, lora)
    w2_all = _pad_rows(rwkv_w2, LANES)
    a2_all = _pad_rows(rwkv_a2, LANES)
    g2_all = _pad_rows(rwkv_g2, 2 * LANES)
    router_w_all = _pad_cols(jnp.concatenate([router_c_w, router_f_w], axis=-1), ROUTER_LANES)
    router_b_all = _pad_cols(jnp.concatenate([router_c_b, router_f_b], axis=-1)[:, None, :], ROUTER_LANES)

    h = x.reshape(n, d)
    for i in range(depth):
        xn = rmsnorm_cast(h, norm_mix_g[i])
        p_rwkv = matmul(xn, w_rwkv_all, i, name="proj_rwkv")
        p_attn = matmul(xn, w_attn_all, i, name="proj_attn")
        p_conv = matmul(xn, w_conv_all, i, name="proj_conv")

        r, lw, k, v, a, b, g, bonus = rwkv_prep(
            p_rwkv, mu_all[i], row(rwkv_w0[i]), w2_all[i], row(rwkv_a0[i]), a2_all[i], g2_all[i],
            row(rwkv_k_k[i]), row(rwkv_k_a[i]), row(rwkv_r_k[i]), seq)
        y = rwkv_scan(r, lw, k, v, a, b, batch, seq)
        y_rwkv = rwkv_post(y, bonus, g, row(rwkv_lnx_g[i]), row(rwkv_lnx_b[i]))

        outs, lses = [], []
        for gi, (window, dil) in enumerate(DILATED_PATTERNS):
            attn = attn_pattern if dil == 1 else attn_dilated
            o, lse = attn(p_attn, biases[gi], row(q_norm_g[i]), row(k_norm_g[i]), gi, dil, batch, seq, aw)
            outs.append(o)
            lses.append(lse)
        y_attn = attn_mix(outs, lses)

        y_conv = conv_module(p_conv, conv_w[i], row(conv_b[i]), row(conv_ln_g[i]), row(conv_ln_b[i]), seq)

        h = matmul([y_rwkv, y_attn, y_conv], w_out_all, i, residual=h, name="out_proj")

        info, counts, xp = route(h, norm_ffn_g[i], router_w_all[i], router_b_all[i])
        n_tiles = (2 * n) // MOE_TILE + N_EXPERTS
        d1, d2, tile_expert, n_used = moe_plan(info, counts, n_tiles)
        ys = moe_ffn(xp, expert_w_in, expert_w_out, i, d1, d2, tile_expert, n_used)
        h, xn = moe_combine(ys, h, info, ple_norm_g[i], d1, d2)

        h = ple(xn, ple_gate_all, p_all, ple_proj_all, h, i)
    return h.reshape(batch, seq, d)
```

```python
import functools
import math

import numpy as np
import jax
import jax.numpy as jnp
from jax import lax
from jax.experimental import pallas as pl
from jax.experimental.pallas import tpu as pltpu

F32 = jnp.float32
BF16 = jnp.bfloat16

HEAD_DIM = 64
DILATED_PATTERNS = ((128, 1), (512, 4), (2048, 16))
ATTN_BLOCK = 128
REL_BUCKETS = 32
REL_MAX_DIST = 2048
NEG_INF = -1e30
CONV_TAPS = 31
N_GROUPS = 8
EXPERTS_PER_GROUP = 8
N_EXPERTS = N_GROUPS * EXPERTS_PER_GROUP
RMS_EPS = 1e-6
LN_EPS = 1e-5
RWKV_LN_EPS = 64e-5

LANES = 128
SUBLANES = 8
VMEM_LIMIT_BYTES = 56 * 1024 * 1024

RWKV_CHUNK = 64
RWKV_HEADS_PER_STEP = 4
RWKV_WAVE = 4
RWKV_TBLK = 1024
ATTN_GROUP = 4
CONV_HALO = 32

NN = (((1,), (0,)), ((), ()))
NT = (((1,), (1,)), ((), ()))
TN = (((0,), (0,)), ((), ()))


def _cparams(*sem):
    return pltpu.CompilerParams(dimension_semantics=sem, vmem_limit_bytes=VMEM_LIMIT_BYTES)


def _dot(a, b, dims=NN):
    return lax.dot_general(a, b, dims, preferred_element_type=F32)


def _split(x):
    hi = x.astype(BF16)
    lo = (x - hi.astype(F32)).astype(BF16)
    return hi, lo


def _dot3(a, b, dims=NN):
    ah, al = _split(a)
    bh, bl = _split(b)
    return _dot(ah, bh, dims) + (_dot(ah, bl, dims) + _dot(al, bh, dims))


def _dot2(a, b_bf16, dims=NN):
    ah, al = _split(a)
    return _dot(ah, b_bf16, dims) + _dot(al, b_bf16, dims)


def _sigmoid(x):
    return 1.0 / (1.0 + jnp.exp(-x))


def _seg_matrix():
    r = lax.broadcasted_iota(jnp.int32, (LANES, LANES), 0) // HEAD_DIM
    c = lax.broadcasted_iota(jnp.int32, (LANES, LANES), 1) // HEAD_DIM
    return (r == c).astype(BF16)


def _head_sum(x, seg):
    w = x.shape[1]
    cols = [_dot2(x[:, j:j + LANES], seg) for j in range(0, w, LANES)]
    return cols[0] if len(cols) == 1 else jnp.concatenate(cols, axis=1)


def _rmsnorm_kernel(x_ref, g_ref, o_ref):
    x = x_ref[...]
    ms = jnp.mean(x * x, axis=-1, keepdims=True)
    o_ref[...] = (x * lax.rsqrt(ms + RMS_EPS) * g_ref[...]).astype(o_ref.dtype)


def rmsnorm_cast(x, g, tm=512):
    n, d = x.shape
    tm = min(tm, n)
    return pl.pallas_call(
        _rmsnorm_kernel,
        grid=(n // tm,),
        in_specs=[pl.BlockSpec((tm, d), lambda i: (i, 0)), pl.BlockSpec((1, d), lambda i: (0, 0))],
        out_specs=pl.BlockSpec((tm, d), lambda i: (i, 0)),
        out_shape=jax.ShapeDtypeStruct((n, d), BF16),
        compiler_params=_cparams("parallel"),
        name="rmsnorm",
    )(x, g.reshape(1, d))


def _mm_kernel(a_ref, w_ref, o_ref):
    o_ref[...] = _dot(a_ref[...], w_ref[...]).astype(o_ref.dtype)


def _mm_cat_res_kernel(*refs):
    *a_refs, w_ref, r_ref, o_ref = refs
    acc = r_ref[...]
    lo = 0
    for a_ref in a_refs:
        k = a_ref.shape[1]
        acc = acc + _dot(a_ref[...], w_ref[lo:lo + k, :])
        lo += k
    o_ref[...] = acc


def _pick(n, pref):
    for t in pref:
        if n % t == 0:
            return t
    return n


def matmul(a, w_all, layer, out_dtype=F32, residual=None, tm=1024, tn=None, name="matmul"):
    pieces = list(a) if isinstance(a, (list, tuple)) else [a]
    m = pieces[0].shape[0]
    k, n = w_all.shape[1:]
    assert sum(p_.shape[1] for p_ in pieces) == k
    tm = _pick(m, (tm, 512, 256, 128))
    tn = tn or _pick(n, (1024, 768, 512, 384, 256, 128))
    grid = (m // tm, n // tn)
    a_specs = [pl.BlockSpec((tm, p_.shape[1]), lambda i, j: (i, 0)) for p_ in pieces]
    w_spec = pl.BlockSpec((None, k, tn), lambda i, j: (layer, 0, j))
    o_spec = pl.BlockSpec((tm, tn), lambda i, j: (i, j))
    if residual is None:
        assert len(pieces) == 1
        return pl.pallas_call(
            _mm_kernel, grid=grid, in_specs=a_specs + [w_spec], out_specs=o_spec,
            out_shape=jax.ShapeDtypeStruct((m, n), out_dtype),
            compiler_params=_cparams("parallel", "arbitrary"), name=name,
        )(*pieces, w_all)
    return pl.pallas_call(
        _mm_cat_res_kernel, grid=grid, in_specs=a_specs + [w_spec, o_spec], out_specs=o_spec,
        out_shape=jax.ShapeDtypeStruct((m, n), F32),
        compiler_params=_cparams("parallel", "arbitrary"), name=name,
    )(*pieces, w_all, residual)


def _rwkv_prep_kernel(x_ref, prev_ref, mu_ref, w0_ref, w2_ref, a0_ref, a2_ref, g2_ref, kk_ref, ka_ref,
                      rk_ref, r_o, lw_o, k_o, v_o, a_o, b_o, g_o, bonus_o, *, seq, width):
    t = x_ref.shape[0]
    i = pl.program_id(0)
    x = x_ref[...]
    seq_start = (i * t) % seq == 0
    prev = jnp.where(seq_start, 0.0, prev_ref[SUBLANES - 1:SUBLANES, :])
    rows = lax.broadcasted_iota(jnp.int32, x.shape, 0)
    shifted = jnp.where(rows == 0, prev, pltpu.roll(x, 1, axis=0))
    x = x + (shifted - x) * mu_ref[...]
    w = width
    r = x[:, 0:w]
    k = x[:, w:2 * w]
    v = x[:, 2 * w:3 * w]
    o = 3 * w
    w_lo = x[:, o:o + LANES]
    a_lo = x[:, o + LANES:o + 2 * LANES]
    g_lo = x[:, o + 2 * LANES:o + 4 * LANES]
    seg = _seg_matrix()

    wv = w0_ref[...] + _dot3(jnp.tanh(w_lo), w2_ref[...])
    w_log = -(jnp.maximum(-wv, 0.0) + jnp.log(1.0 + jnp.exp(-jnp.abs(wv)))) - 0.5
    lw_o[...] = -jnp.exp(w_log)
    a = _sigmoid(a0_ref[...] + _dot3(a_lo, a2_ref[...]))
    g_o[...] = _dot3(_sigmoid(g_lo), g2_ref[...]).astype(g_o.dtype)

    kk = k * kk_ref[...]
    nrm = jnp.sqrt(_head_sum(kk * kk, seg))
    kk = kk / jnp.maximum(nrm, 1e-12)
    k = k * (1.0 + (a - 1.0) * ka_ref[...])
    r_o[...] = r.astype(r_o.dtype)
    k_o[...] = k.astype(k_o.dtype)
    v_o[...] = v.astype(v_o.dtype)
    a_o[...] = (-kk).astype(a_o.dtype)
    b_o[...] = (kk * a).astype(b_o.dtype)
    bonus_o[...] = (_head_sum(r * k * rk_ref[...], seg) * v).astype(bonus_o.dtype)


def rwkv_prep(proj, mu, w0, w2, a0, a2, g2, k_k, k_a, r_k, seq, tm=128):
    n, wp = proj.shape
    width = w0.shape[-1]
    tm = min(tm, seq)
    row = lambda i: (i, 0)
    fix = lambda i: (0, 0)
    vec = pl.BlockSpec((1, width), fix)
    out = pl.BlockSpec((tm, width), row)
    per = tm // SUBLANES
    return pl.pallas_call(
        functools.partial(_rwkv_prep_kernel, seq=seq, width=width),
        grid=(n // tm,),
        in_specs=[pl.BlockSpec((tm, wp), row),
                  pl.BlockSpec((SUBLANES, wp), lambda i: (jnp.maximum(i * per - 1, 0), 0)),
                  pl.BlockSpec((1, wp), fix), vec, pl.BlockSpec(w2.shape, fix), vec,
                  pl.BlockSpec(a2.shape, fix), pl.BlockSpec(g2.shape, fix), vec, vec, vec],
        out_specs=[out] * 8,
        out_shape=[jax.ShapeDtypeStruct((n, width), F32 if j == 1 else BF16) for j in range(8)],
        compiler_params=_cparams("parallel"), name="rwkv_prep",
    )(proj, proj, mu, w0, w2, a0, a2, g2, k_k, k_a, r_k)


SCAN_PASSES = dict(gram=1, gkv=1, inv=1, t=1, r=1, y=1, m=1, s=3)


def _pdot(a, b, dims, passes):
    if passes == 1:
        return _dot(a.astype(BF16), b.astype(BF16), dims)
    return _dot3(a, b, dims)


def _rwkv_scan_kernel(r_ref, lw_ref, k_ref, v_ref, a_ref, b_ref, y_ref, st_ref, *, chunk, wave):
    c_len = chunk
    hd = HEAD_DIM
    nheads = r_ref.shape[1] // hd
    ps = SCAN_PASSES

    @pl.when(pl.program_id(2) == 0)
    def _():
        st_ref[...] = jnp.zeros_like(st_ref)

    nchunks = r_ref.shape[0] // c_len
    row2 = lax.broadcasted_iota(jnp.int32, (2 * c_len, 2 * c_len), 0)
    col2 = lax.broadcasted_iota(jnp.int32, (2 * c_len, 2 * c_len), 1) % c_len
    mask2 = col2 <= jnp.where(row2 < c_len, row2 - 1, row2 - c_len)
    rowc = lax.broadcasted_iota(jnp.int32, (c_len, 2 * c_len), 0)
    colc = lax.broadcasted_iota(jnp.int32, (c_len, 2 * c_len), 1)
    right = colc >= c_len
    eye_right = (colc - c_len == rowc).astype(F32)
    zeros_h = jnp.zeros((c_len, hd), F32)
    zeros_w = jnp.zeros((c_len, 2 * c_len), F32)
    eye_k = (lax.broadcasted_iota(jnp.int32, (hd, hd), 0)
             == lax.broadcasted_iota(jnp.int32, (hd, hd), 1)).astype(F32)
    ridx = lax.broadcasted_iota(jnp.int32, (c_len, r_ref.shape[1]), 0)

    def load_units(chunks):
        units = []
        for c in chunks:
            rows = slice(c * c_len, (c + 1) * c_len)
            lw = lw_ref[rows, :]
            cum = lw
            sh = 1
            while sh < c_len:
                cum = cum + jnp.where(ridx >= sh, pltpu.roll(cum, sh, axis=0), 0.0)
                sh *= 2
            total = cum[c_len - 1:c_len, :]
            e_neg = jnp.exp(-cum)
            e_end = jnp.exp(total - cum)
            rt = r_ref[rows, :].astype(F32) * jnp.exp(cum)
            at = a_ref[rows, :].astype(F32) * jnp.exp(cum - lw)
            b = b_ref[rows, :].astype(F32)
            k = k_ref[rows, :].astype(F32)
            bt, kt, bh, kh = b * e_neg, k * e_neg, b * e_end, k * e_end
            v = v_ref[rows, :].astype(F32)
            g_end = jnp.exp(total)
            for h in range(nheads):
                sl = slice(h * hd, (h + 1) * hd)
                units.append(dict(c=c, h=h, at=at[:, sl], rt=rt[:, sl], v=v[:, sl], bt=bt[:, sl], kt=kt[:, sl],
                                  bh=bh[:, sl], kh=kh[:, sl], g=g_end[:, sl]))
        return units

    def phase1(units):
        def below(x):
            return jnp.concatenate([zeros_h, x], axis=0)

        def above(x):
            return jnp.concatenate([x, zeros_h], axis=0)

        for u in units:
            l2 = jnp.concatenate([u["at"], u["rt"]], axis=0)
            bk = jnp.concatenate([u["bt"], u["kt"]], axis=0)
            u["gm"] = jnp.where(mask2, _pdot(l2, bk, NT, ps["gram"]), 0.0)
        yield
        for u in units:
            u["gkv"] = _pdot(u["gm"], below(u["v"]), NN, ps["gkv"])
            u["w"] = jnp.where(right, eye_right, u["gm"][:c_len])
        yield
        p = 1
        while p < c_len:
            for u in units:
                w = u["w"]
                u["w"] = _pdot(w, jnp.concatenate([w, zeros_w], axis=0), NN, ps["inv"]) + jnp.where(right, w, 0.0)
            yield
            p *= 2
        for u in units:
            u["ta"] = _pdot(u["w"], below(u["at"]), NN, ps["t"])
        yield
        for u in units:
            u["u0"] = _pdot(u["w"], below(u["gkv"][:c_len]), NN, ps["t"])
        yield
        for u in units:
            u["rhat"] = u["rt"] + _pdot(u["gm"][c_len:], above(u["ta"]), NN, ps["r"])
        yield
        for u in units:
            u["y1"] = _pdot(u["gm"][c_len:], above(u["u0"]), NN, ps["r"]) + u["gkv"][c_len:]
        yield
        for u in units:
            u["m"] = eye_k * u["g"] + _pdot(u["bh"], u["ta"], TN, ps["m"])
        yield
        for u in units:
            u["nm"] = _pdot(jnp.concatenate([u["bh"], u["kh"]], axis=0),
                            jnp.concatenate([u["u0"], u["v"]], axis=0), TN, ps["m"])
        yield

    state = [st_ref[h] for h in range(nheads)]

    def phase2_steps(units):
        by_chunk = {}
        for u in units:
            by_chunk.setdefault(u["c"], []).append(u)

        def step(c):
            ys = []
            for u in by_chunk[c]:
                h = u["h"]
                ys.append(_pdot(u["rhat"], state[h], NN, ps["y"]) + u["y1"])
                state[h] = _pdot(u["m"], state[h], NN, ps["s"]) + u["nm"]
            y_ref[c * c_len:(c + 1) * c_len, :] = jnp.concatenate(ys, axis=1)

        return [functools.partial(step, c) for c in sorted(by_chunk)]

    pending = []
    for w0 in range(0, nchunks, wave):
        units = load_units(range(w0, min(w0 + wave, nchunks)))
        for _ in phase1(units):
            if pending:
                pending.pop(0)()
        for step in pending:
            step()
        pending = phase2_steps(units)
    for step in pending:
        step()
    for h in range(nheads):
        st_ref[h] = state[h]


def rwkv_scan(r, lw, k, v, a, b, batch, seq, tblk=RWKV_TBLK):
    n, width = r.shape
    tblk = min(tblk, seq)
    nt = seq // tblk
    wl = RWKV_HEADS_PER_STEP * HEAD_DIM
    assert width % wl == 0, (width, wl)
    spec = pl.BlockSpec((tblk, wl), lambda bi, hp, tb: (bi * nt + tb, hp))
    return pl.pallas_call(
        functools.partial(_rwkv_scan_kernel, chunk=RWKV_CHUNK, wave=RWKV_WAVE),
        grid=(batch, width // wl, nt),
        in_specs=[spec] * 6,
        out_specs=spec,
        out_shape=jax.ShapeDtypeStruct((n, width), F32),
        scratch_shapes=[pltpu.VMEM((RWKV_HEADS_PER_STEP, HEAD_DIM, HEAD_DIM), F32)],
        compiler_params=_cparams("parallel", "parallel", "arbitrary"), name="rwkv_scan",
    )(r, lw, k, v, a, b)


def _rwkv_post_kernel(y_ref, bonus_ref, g_ref, lg_ref, lb_ref, o_ref):
    seg = _seg_matrix()
    y = y_ref[...]
    inv_n = 1.0 / HEAD_DIM
    mean = _head_sum(y, seg) * inv_n
    yc = y - mean
    var = _head_sum(yc * yc, seg) * inv_n
    yn = yc * lax.rsqrt(var + RWKV_LN_EPS) * lg_ref[...] + lb_ref[...]
    o_ref[...] = ((yn + bonus_ref[...].astype(F32)) * g_ref[...].astype(F32)).astype(o_ref.dtype)


def rwkv_post(y, bonus, g, lnx_g, lnx_b, tm=256):
    n, width = y.shape
    tm = _pick(n, (tm, 128))
    row = pl.BlockSpec((tm, width), lambda i: (i, 0))
    vec = pl.BlockSpec((1, width), lambda i: (0, 0))
    return pl.pallas_call(
        _rwkv_post_kernel, grid=(n // tm,),
        in_specs=[row, row, row, vec, vec], out_specs=row,
        out_shape=jax.ShapeDtypeStruct((n, width), BF16),
        compiler_params=_cparams("parallel"), name="rwkv_post",
    )(y, bonus, g, lnx_g, lnx_b)


def _t5_causal_bucket(dist):
    max_exact = REL_BUCKETS // 2
    d_f = jnp.maximum(dist, 1).astype(F32)
    large = max_exact + (jnp.log(d_f / max_exact) / math.log(REL_MAX_DIST / max_exact)
                         * (REL_BUCKETS - max_exact)).astype(jnp.int32)
    large = jnp.minimum(large, REL_BUCKETS - 1)
    return jnp.where(dist < max_exact, dist, large)


def _pattern_bias(rel_bias_g, window, dil):
    span = window // dil
    qi = np.arange(ATTN_BLOCK)[:, None]
    ki = np.arange(2 * ATTN_BLOCK)[None, :]
    off = qi + ATTN_BLOCK - ki
    band = (off >= 0) & (off <= span)
    bucket = _t5_causal_bucket(jnp.asarray(np.clip(off, 0, span) * dil, jnp.int32))
    onehot = (bucket[:, :, None] == jnp.arange(REL_BUCKETS)[None, None, :]).astype(F32)
    bias = jnp.einsum("qkb,bh->hqk", onehot, rel_bias_g.astype(F32), precision=lax.Precision.HIGHEST)
    return jnp.where(jnp.asarray(band)[None], bias, NEG_INF)


def _attn_blocks(blocks, bias_ref, qg, kg, no_prev):
    hd = HEAD_DIM
    scale = hd ** -0.5
    seg = _seg_matrix()
    ones_k = jnp.ones((2 * ATTN_BLOCK, LANES), BF16)
    head_of_lane = lax.broadcasted_iota(jnp.int32, (ATTN_BLOCK, LANES), 1) // hd

    def norm(x, g):
        ms = _dot((x * x).astype(BF16), seg) * (1.0 / hd)
        return x * lax.rsqrt(ms + RMS_EPS) * g

    pairs = []
    for bi, (q, kp, kc, vp, vc) in enumerate(blocks):
        for c in range(q.shape[1] // LANES):
            sl = slice(c * LANES, (c + 1) * LANES)
            pairs.append(dict(b=bi, c=c, q=q[:, sl], k=jnp.concatenate([kp[:, sl], kc[:, sl]], axis=0),
                              v=jnp.concatenate([vp[:, sl], vc[:, sl]], axis=0)))
    for pr in pairs:
        pr["qn"] = norm(pr["q"], qg)
        pr["kn"] = norm(pr["k"], kg).astype(BF16)
        pr["vb"] = pr["v"].astype(BF16)
    units = [dict(pr=pr, h=h) for pr in pairs for h in range(LANES // hd)]
    for u in units:
        pr = u["pr"]
        qm = jnp.where(head_of_lane == u["h"], pr["qn"], 0.0).astype(BF16)
        s = _dot(qm, pr["kn"], NT) * scale + bias_ref[pr["c"] * (LANES // hd) + u["h"]]
        u["s"] = jnp.where(no_prev, NEG_INF, s)
    for u in units:
        u["m"] = jnp.max(u["s"], axis=-1, keepdims=True)
    for u in units:
        u["p"] = jnp.exp(u["s"] - u["m"]).astype(BF16)
    for u in units:
        u["den"] = _dot(u["p"], ones_k)
    for u in units:
        u["o"] = _dot(u["p"], u["pr"]["vb"]) / u["den"]
        u["lse"] = u["m"] + jnp.log(u["den"])
    res = []
    for bi in range(len(blocks)):
        outs, lses = [], []
        for pr in (p_ for p_ in pairs if p_["b"] == bi):
            u0, u1 = [u for u in units if u["pr"] is pr]
            outs.append(jnp.where(head_of_lane == 0, u0["o"], u1["o"]))
            lses.append(jnp.where(head_of_lane == 0, u0["lse"], u1["lse"]))
        res.append((outs[0] if len(outs) == 1 else jnp.concatenate(outs, axis=1),
                    lses[0] if len(lses) == 1 else jnp.concatenate(lses, axis=1)))
    return res


def _no_prev_mask():
    ki = lax.broadcasted_iota(jnp.int32, (ATTN_BLOCK, 2 * ATTN_BLOCK), 1)
    return jnp.logical_and(pl.program_id(1) == 0, ki < ATTN_BLOCK)


def _attn_kernel(q_ref, kp_ref, kc_ref, vp_ref, vc_ref, bias_ref, qg_ref, kg_ref, o_ref, lse_ref):
    (o, lse), = _attn_blocks([(q_ref[...], kp_ref[...], kc_ref[...], vp_ref[...], vc_ref[...])], bias_ref,
                             qg_ref[...], kg_ref[...], _no_prev_mask())
    o_ref[...] = o
    lse_ref[...] = lse


def _attn_dilated_kernel(q_ref, kp_ref, kc_ref, vp_ref, vc_ref, bias_ref, qg_ref, kg_ref, o_ref, lse_ref, *, dil):
    no_prev = _no_prev_mask()
    group = min(dil, ATTN_GROUP)

    def body(rg, carry):
        rows = [pl.ds(rg * group + j, ATTN_BLOCK, stride=dil) for j in range(group)]
        blocks = [(q_ref[rw, :], kp_ref[rw, :], kc_ref[rw, :], vp_ref[rw, :], vc_ref[rw, :]) for rw in rows]
        for rw, (o, lse) in zip(rows, _attn_blocks(blocks, bias_ref, qg_ref[...], kg_ref[...], no_prev)):
            o_ref[rw, :] = o
            lse_ref[rw, :] = lse
        return carry

    lax.fori_loop(0, dil // group, body, 0)


def attn_dilated(proj, bias, q_g, k_g, gi, dil, batch, seq, width):
    n = proj.shape[0]
    npat = len(DILATED_PATTERNS)
    pw = width // npat
    rows = ATTN_BLOCK * dil
    nb = seq // rows
    sec = width // LANES
    hp_per = pw // LANES
    heads_pp = LANES // HEAD_DIM

    def spec(section, prev):
        def imap(b, nblk, hp):
            blk = jnp.maximum(nblk - 1, 0) if prev else nblk
            return (b * nb + blk, section * sec + gi * hp_per + hp)
        return pl.BlockSpec((rows, LANES), imap)

    bias_spec = pl.BlockSpec((heads_pp,) + bias.shape[1:], lambda b, nblk, hp: (hp, 0, 0))
    vec = pl.BlockSpec((1, LANES), lambda b, nblk, hp: (0, 0))
    q_g, k_g = (jnp.tile(g, (1, heads_pp)) for g in (q_g, k_g))
    ospec = pl.BlockSpec((rows, LANES), lambda b, nblk, hp: (b * nb + nblk, hp))
    return pl.pallas_call(
        functools.partial(_attn_dilated_kernel, dil=dil), grid=(batch, nb, hp_per),
        in_specs=[spec(0, False), spec(1, True), spec(1, False), spec(2, True), spec(2, False), bias_spec, vec, vec],
        out_specs=[ospec, ospec],
        out_shape=[jax.ShapeDtypeStruct((n, pw), F32)] * 2,
        compiler_params=_cparams("parallel", "parallel", "parallel"), name=f"attn_d{dil}",
    )(proj, proj, proj, proj, proj, bias, q_g, k_g)


def attn_pattern(proj, bias, q_g, k_g, gi, dil, batch, seq, width):
    n = proj.shape[0]
    npat = len(DILATED_PATTERNS)
    pw = width // npat
    nsub = seq // dil
    nb = nsub // ATTN_BLOCK
    sec = width // pw
    rowlen = 3 * sec
    x = proj.reshape(batch * nsub, dil * 3 * width)

    def spec(section, prev):
        def imap(b, nblk, r):
            blk = jnp.maximum(nblk - 1, 0) if prev else nblk
            return (b * nb + blk, r * rowlen + section * sec + gi)
        return pl.BlockSpec((ATTN_BLOCK, pw), imap)

    fix3 = pl.BlockSpec(bias.shape, lambda b, nblk, r: (0, 0, 0))
    vec = pl.BlockSpec((1, LANES), lambda b, nblk, r: (0, 0))
    q_g, k_g = (jnp.tile(g, (1, LANES // HEAD_DIM)) for g in (q_g, k_g))
    ospec = pl.BlockSpec((ATTN_BLOCK, pw), lambda b, nblk, r: (b * nb + nblk, r))
    o, lse = pl.pallas_call(
        _attn_kernel, grid=(batch, nb, dil),
        in_specs=[spec(0, False), spec(1, True), spec(1, False), spec(2, True), spec(2, False), fix3, vec, vec],
        out_specs=[ospec, ospec],
        out_shape=[jax.ShapeDtypeStruct((batch * nsub, dil * pw), F32)] * 2,
        compiler_params=_cparams("parallel", "parallel", "parallel"), name=f"attn_d{dil}",
    )(x, x, x, x, x, bias, q_g, k_g)
    return o.reshape(n, pw), lse.reshape(n, pw)


def _attn_mix_kernel(o0, o1, o2, l0, l1, l2, out_ref):
    a, b, c = l0[...], l1[...], l2[...]
    m = jnp.maximum(jnp.maximum(a, b), c)
    ea, eb, ec = jnp.exp(a - m), jnp.exp(b - m), jnp.exp(c - m)
    inv = 1.0 / (ea + eb + ec)
    out_ref[...] = jnp.concatenate([o0[...] * (ea * inv), o1[...] * (eb * inv), o2[...] * (ec * inv)],
                                   axis=1).astype(out_ref.dtype)


def attn_mix(outs, lses, tm=512):
    n, pw = outs[0].shape
    tm = _pick(n, (tm, 256, 128))
    row = pl.BlockSpec((tm, pw), lambda i: (i, 0))
    return pl.pallas_call(
        _attn_mix_kernel, grid=(n // tm,), in_specs=[row] * 6,
        out_specs=pl.BlockSpec((tm, 3 * pw), lambda i: (i, 0)),
        out_shape=jax.ShapeDtypeStruct((n, 3 * pw), BF16),
        compiler_params=_cparams("parallel"), name="attn_mix",
    )(*outs, *lses)


def _conv_kernel(x_ref, halo_ref, w_ref, b_ref, lg_ref, lb_ref, o_ref, ext_ref, shift_ref, *, seq, width):
    t = x_ref.shape[0]
    i = pl.program_id(0)
    seq_start = (i * t) % seq == 0

    def glu(x):
        return x[:, :width] * _sigmoid(x[:, width:])

    ext_ref[0:CONV_HALO, :] = jnp.where(seq_start, 0.0, glu(halo_ref[...]))
    ext_ref[CONV_HALO:, :] = glu(x_ref[...])
    base = CONV_HALO - (CONV_TAPS - 1)
    acc = jnp.broadcast_to(b_ref[...], (t, width))
    for rho in range(SUBLANES):
        offs = [base + j for j in range(CONV_TAPS) if (base + j) % SUBLANES == rho]
        if not offs:
            continue
        lo = offs[0]
        span = offs[-1] - lo + t
        shift_ref[0:span, :] = ext_ref[lo:lo + span, :]
        for o in offs:
            acc = acc + w_ref[o - base:o - base + 1, :] * shift_ref[o - lo:o - lo + t, :]
    mu = jnp.mean(acc, axis=-1, keepdims=True)
    d = acc - mu
    var = jnp.mean(d * d, axis=-1, keepdims=True)
    y = d * lax.rsqrt(var + LN_EPS) * lg_ref[...] + lb_ref[...]
    o_ref[...] = (y * _sigmoid(y)).astype(o_ref.dtype)


def conv_module(proj, conv_w, conv_b, ln_g, ln_b, seq, tm=256):
    n, w2 = proj.shape
    width = w2 // 2
    tm = min(tm, seq)
    per = tm // CONV_HALO
    fix = lambda i: (0, 0)
    vec = pl.BlockSpec((1, width), fix)
    return pl.pallas_call(
        functools.partial(_conv_kernel, seq=seq, width=width), grid=(n // tm,),
        in_specs=[pl.BlockSpec((tm, w2), lambda i: (i, 0)),
                  pl.BlockSpec((CONV_HALO, w2), lambda i: (jnp.maximum(i * per - 1, 0), 0)),
                  pl.BlockSpec((CONV_TAPS, width), fix), vec, vec, vec],
        out_specs=pl.BlockSpec((tm, width), lambda i: (i, 0)),
        out_shape=jax.ShapeDtypeStruct((n, width), BF16),
        scratch_shapes=[pltpu.VMEM((tm + CONV_HALO, width), F32), pltpu.VMEM((tm + CONV_HALO, width), F32)],
        compiler_params=_cparams("parallel"), name="conv_module",
    )(proj, proj, conv_w, conv_b, ln_g, ln_b)


ROUTER_LANES = LANES


def _pack_bf16_pairs(x):
    w = x.shape[1] // 2
    hi = lax.bitcast_convert_type(x[:, :w].astype(BF16).astype(F32), jnp.uint32)
    lo = lax.bitcast_convert_type(x[:, w:].astype(BF16).astype(F32), jnp.uint32)
    return hi | (lo >> 16)


def _unpack_bf16_pairs(words):
    hi = lax.bitcast_convert_type(words & jnp.uint32(0xFFFF0000), F32)
    lo = lax.bitcast_convert_type(words << 16, F32)
    return hi, lo


def _store_token_major(ref, words):
    t = words.shape[0]
    chunks = words.shape[1] // LANES
    for c in range(chunks):
        ref[pl.ds(c, t, stride=chunks), :] = words[:, c * LANES:(c + 1) * LANES]


def _load_token_major(ref, t, chunks):
    return jnp.concatenate([ref[pl.ds(c, t, stride=chunks), :] for c in range(chunks)], axis=1)


def _route_kernel(x_ref, g_ref, rw_ref, bias_ref, info_ref, cnt_ref, xp_ref, run_ref):
    t = x_ref.shape[0]

    @pl.when(pl.program_id(0) == 0)
    def _():
        run_ref[...] = jnp.zeros_like(run_ref)

    x = x_ref[...]
    ms = jnp.mean(x * x, axis=-1, keepdims=True)
    xn = x * lax.rsqrt(ms + RMS_EPS) * g_ref[...]
    _store_token_major(xp_ref, _pack_bf16_pairs(xn))
    z = _dot3(xn, rw_ref[...]) + bias_ref[...]
    lane = lax.broadcasted_iota(jnp.int32, z.shape, 1).astype(F32)
    ninf = -jnp.inf
    big = float(ROUTER_LANES)
    zc = jnp.where(lane < N_GROUPS, z, ninf)
    mc = jnp.max(zc, axis=-1, keepdims=True)
    pg = 1.0 / jnp.sum(jnp.exp(zc - mc), axis=-1, keepdims=True)
    g_idx = jnp.min(jnp.where(zc == mc, lane, big), axis=-1, keepdims=True)
    lo = N_GROUPS + g_idx * EXPERTS_PER_GROUP
    zf = jnp.where(jnp.logical_and(lane >= lo, lane < lo + EXPERTS_PER_GROUP), z, ninf)
    m1 = jnp.max(zf, axis=-1, keepdims=True)
    i1 = jnp.min(jnp.where(zf == m1, lane, big), axis=-1, keepdims=True)
    zf2 = jnp.where(lane == i1, ninf, zf)
    m2 = jnp.max(zf2, axis=-1, keepdims=True)
    i2 = jnp.min(jnp.where(zf2 == m2, lane, big), axis=-1, keepdims=True)
    e2 = jnp.exp(m2 - m1)
    w1 = pg / (1.0 + e2)
    w2 = pg * e2 / (1.0 + e2)
    onehot = jnp.where(jnp.logical_or(lane == i1, lane == i2), 1.0, 0.0)
    tri = (lax.broadcasted_iota(jnp.int32, (t, t), 0) > lax.broadcasted_iota(jnp.int32, (t, t), 1)).astype(BF16)
    before = _dot(tri, onehot.astype(BF16)) + run_ref[...]
    r1 = jnp.sum(jnp.where(lane == i1, before, 0.0), axis=-1, keepdims=True)
    r2 = jnp.sum(jnp.where(lane == i2, before, 0.0), axis=-1, keepdims=True)
    run_ref[...] += jnp.sum(onehot, axis=0, keepdims=True)
    cnt_ref[...] = run_ref[...]
    cols = (i1 - N_GROUPS, i2 - N_GROUPS, w1, w2, r1, r2)
    info = jnp.zeros_like(z)
    for j, c in enumerate(cols):
        info = jnp.where(lane == float(j), c, info)
    info_ref[...] = info


def route(x, g, router_w, bias, tm=256):
    n, d = x.shape
    w = router_w.shape[1]
    chunks = d // 2 // LANES
    tm = _pick(n, (tm, 128))
    fix = lambda i: (0, 0)
    row = pl.BlockSpec((tm, w), lambda i: (i, 0))
    one = pl.BlockSpec((1, w), fix)
    return pl.pallas_call(
        _route_kernel, grid=(n // tm,),
        in_specs=[pl.BlockSpec((tm, d), lambda i: (i, 0)), pl.BlockSpec((1, d), fix), pl.BlockSpec((d, w), fix), one],
        out_specs=[row, one, pl.BlockSpec((tm * chunks, LANES), lambda i: (i, 0))],
        out_shape=[jax.ShapeDtypeStruct((n, w), F32), jax.ShapeDtypeStruct((1, w), F32),
                   jax.ShapeDtypeStruct((n * chunks, LANES), jnp.uint32)],
        scratch_shapes=[pltpu.VMEM((1, w), F32)],
        compiler_params=_cparams("arbitrary"), name="route",
    )(x, g.reshape(1, d), router_w, bias)


MOE_TILE = 256


def _moe_dest_kernel(info_ref, off_ref, dd_ref):
    info = info_ref[...]
    lane = lax.broadcasted_iota(jnp.int32, info.shape, 1).astype(F32)
    off = off_ref[...]

    def dest(e, r):
        return jnp.sum(jnp.where(lane == e + N_GROUPS, off, 0.0), axis=-1, keepdims=True) + r

    d1 = dest(info[:, 0:1], info[:, 4:5])
    d2 = dest(info[:, 1:2], info[:, 5:6])
    dd_ref[...] = jnp.where(lane == 0.0, d1, jnp.where(lane == 1.0, d2, 0.0)).astype(jnp.int32)


def moe_plan(info, counts, n_tiles, tm=512):
    n, w = info.shape
    cnt = counts[0, N_GROUPS:N_GROUPS + N_EXPERTS].astype(jnp.int32)
    padded = (cnt + MOE_TILE - 1) // MOE_TILE * MOE_TILE
    ends = jnp.cumsum(padded)
    off_row = jnp.pad((ends - padded).astype(F32), (N_GROUPS, w - N_GROUPS - N_EXPERTS)).reshape(1, w)
    tm = _pick(n, (tm, 256, 128))
    row = pl.BlockSpec((tm, w), lambda i: (i, 0))
    dd = pl.pallas_call(
        _moe_dest_kernel, grid=(n // tm,),
        in_specs=[row, pl.BlockSpec((1, w), lambda i: (0, 0))], out_specs=row,
        out_shape=jax.ShapeDtypeStruct((n, w), jnp.int32),
        compiler_params=_cparams("parallel"), name="moe_dest",
    )(info, off_row)
    tile_start = jnp.arange(n_tiles, dtype=jnp.int32) * MOE_TILE
    tile_expert = jnp.minimum(jnp.sum((tile_start[:, None] >= ends[None, :]).astype(jnp.int32), axis=1),
                              N_EXPERTS - 1)
    n_used = (ends[-1:] // MOE_TILE).astype(jnp.int32)
    return dd[:, 0], dd[:, 1], tile_expert, n_used


DMA_ISSUE_UNROLL = 8


def _moe_ffn_kernel(d1_ref, d2_ref, texp_ref, nused_ref, xp_hbm, wi_ref, wo_ref, ys_ref,
                    src_ref, xbuf, sem, *, n_tok, ff):
    i = pl.program_id(0)
    n_used = nused_ref[0]
    n_rows = src_ref.shape[0]
    half = wi_ref.shape[1] // 2
    chunks = half // LANES

    def gather(tile, slot):
        def body(j, carry):
            tok = src_ref[tile * MOE_TILE + j]
            pltpu.make_async_copy(xp_hbm.at[pl.ds(tok * chunks, chunks), :],
                                  xbuf.at[slot, pl.ds(j * chunks, chunks), :], sem.at[slot]).start()
            return carry
        lax.fori_loop(0, MOE_TILE, body, 0, unroll=DMA_ISSUE_UNROLL)

    @pl.when(i == 0)
    def _():
        def clear(j, carry):
            src_ref[j] = 0
            return carry
        lax.fori_loop(0, n_rows, clear, 0, unroll=DMA_ISSUE_UNROLL)

        def fill(tok, carry):
            src_ref[d1_ref[tok]] = tok
            src_ref[d2_ref[tok]] = tok
            return carry
        lax.fori_loop(0, n_tok, fill, 0, unroll=DMA_ISSUE_UNROLL)
        gather(0, 0)

    @pl.when(i < n_used)
    def _():
        slot = i % 2

        @pl.when(i + 1 < n_used)
        def _():
            gather(i + 1, 1 - slot)

        pltpu.make_async_copy(xp_hbm.at[pl.ds(0, MOE_TILE * chunks), :], xbuf.at[slot], sem.at[slot]).wait()
        x_a, x_b = _unpack_bf16_pairs(_load_token_major(xbuf.at[slot], MOE_TILE, chunks))
        gu = (_dot(x_a.astype(BF16), wi_ref[0, :half, :].astype(BF16))
              + _dot(x_b.astype(BF16), wi_ref[0, half:, :].astype(BF16)))
        gt, up = gu[:, :ff], gu[:, ff:]
        hid = gt * _sigmoid(gt) * up
        _store_token_major(ys_ref, _pack_bf16_pairs(_dot(hid.astype(BF16), wo_ref[0].astype(BF16))))

    @pl.when(i >= n_used)
    def _():
        ys_ref[...] = jnp.zeros_like(ys_ref)


def moe_ffn(xp, w_in_all, w_out_all, layer, d1, d2, tile_expert, n_used):
    w_in, w_out = w_in_all, w_out_all
    _, ne, d, ff2 = w_in.shape
    chunks = d // 2 // LANES
    n = xp.shape[0] // chunks
    n_tiles = tile_expert.shape[0]
    n_rows = n_tiles * MOE_TILE
    last = lambda i, nu: jnp.minimum(i, nu[0] - 1)
    grid_spec = pltpu.PrefetchScalarGridSpec(
        num_scalar_prefetch=4, grid=(n_tiles,),
        in_specs=[pl.BlockSpec(memory_space=pl.ANY),
                  pl.BlockSpec((None, 1, d, ff2), lambda i, d1, d2, te, nu: (layer, te[last(i, nu)], 0, 0)),
                  pl.BlockSpec((None, 1, ff2 // 2, d), lambda i, d1, d2, te, nu: (layer, te[last(i, nu)], 0, 0))],
        out_specs=pl.BlockSpec((MOE_TILE * chunks, LANES), lambda i, d1, d2, te, nu: (i, 0)),
        scratch_shapes=[pltpu.SMEM((n_rows,), jnp.int32), pltpu.VMEM((2, MOE_TILE * chunks, LANES), jnp.uint32),
                        pltpu.SemaphoreType.DMA((2,))])
    return pl.pallas_call(
        functools.partial(_moe_ffn_kernel, n_tok=n, ff=ff2 // 2), grid_spec=grid_spec,
        out_shape=jax.ShapeDtypeStruct((n_rows * chunks, LANES), jnp.uint32),
        compiler_params=_cparams("arbitrary"), name="moe_ffn",
    )(d1, d2, tile_expert, n_used, xp, w_in, w_out)


def _moe_combine_kernel(d1_ref, d2_ref, ys_hbm, h_ref, info_ref, g_ref, o_ref, xn_ref, ybuf, sem):
    t = h_ref.shape[0]
    half = h_ref.shape[1] // 2
    chunks = half // LANES
    i = pl.program_id(0)
    slot = i % 2

    def gather(tile, slot_):
        def issue(j, carry):
            dst = pl.ds(j * chunks, chunks)
            tok = tile * t + j
            pltpu.make_async_copy(ys_hbm.at[pl.ds(d1_ref[tok] * chunks, chunks), :], ybuf.at[slot_, 0, dst, :],
                                  sem.at[slot_, 0]).start()
            pltpu.make_async_copy(ys_hbm.at[pl.ds(d2_ref[tok] * chunks, chunks), :], ybuf.at[slot_, 1, dst, :],
                                  sem.at[slot_, 1]).start()
            return carry
        lax.fori_loop(0, t, issue, 0, unroll=DMA_ISSUE_UNROLL)

    @pl.when(i == 0)
    def _():
        gather(0, 0)

    @pl.when(i + 1 < pl.num_programs(0))
    def _():
        gather(i + 1, 1 - slot)

    for s in range(2):
        pltpu.make_async_copy(ys_hbm.at[pl.ds(0, t * chunks), :], ybuf.at[slot, s], sem.at[slot, s]).wait()
    info = info_ref[...]
    w1, w2 = info[:, 2:3], info[:, 3:4]
    y1a, y1b = _unpack_bf16_pairs(_load_token_major(ybuf.at[slot, 0], t, chunks))
    y2a, y2b = _unpack_bf16_pairs(_load_token_major(ybuf.at[slot, 1], t, chunks))
    h_a = h_ref[:, :half] + w1 * y1a + w2 * y2a
    h_b = h_ref[:, half:] + w1 * y1b + w2 * y2b
    o_ref[:, :half] = h_a
    o_ref[:, half:] = h_b
    ms = (jnp.sum(h_a * h_a, axis=-1, keepdims=True) + jnp.sum(h_b * h_b, axis=-1, keepdims=True)) / (2 * half)
    scale = lax.rsqrt(ms + RMS_EPS)
    xn_ref[:, :half] = (h_a * scale * g_ref[:, :half]).astype(xn_ref.dtype)
    xn_ref[:, half:] = (h_b * scale * g_ref[:, half:]).astype(xn_ref.dtype)


def moe_combine(ys, h, info, g_next, d1, d2, tm=256):
    n, d = h.shape
    tm = _pick(n, (tm, 128))
    row = lambda i, d1, d2: (i, 0)
    tile = pl.BlockSpec((tm, d), row)
    grid_spec = pltpu.PrefetchScalarGridSpec(
        num_scalar_prefetch=2, grid=(n // tm,),
        in_specs=[pl.BlockSpec(memory_space=pl.ANY), tile, pl.BlockSpec((tm, info.shape[1]), row),
                  pl.BlockSpec((1, d), lambda i, d1, d2: (0, 0))],
        out_specs=[tile, tile],
        scratch_shapes=[pltpu.VMEM((2, 2, tm * (d // 2 // LANES), LANES), jnp.uint32),
                        pltpu.SemaphoreType.DMA((2, 2))])
    return pl.pallas_call(
        _moe_combine_kernel, grid_spec=grid_spec,
        out_shape=[jax.ShapeDtypeStruct((n, d), F32), jax.ShapeDtypeStruct((n, d), BF16)],
        compiler_params=_cparams("arbitrary"), name="moe_combine",
    )(d1, d2, ys, h, info, g_next.reshape(1, d))


def _ple_kernel(a_ref, w_ref, p_ref, pw_ref, h_ref, o_ref):
    gate = _sigmoid(_dot(a_ref[...], w_ref[...]))
    o_ref[...] = h_ref[...] + _dot(p_ref[...], pw_ref[...]) * gate


def ple(xn, gate_w_all, p_all, proj_w_all, h, layer, tm=1024, tn=512):
    n, d = xn.shape
    pd = p_all.shape[2]
    tm = _pick(n, (tm, 512, 256, 128))
    tn = _pick(d, (tn, 256, 128))
    tile = pl.BlockSpec((tm, tn), lambda i, j: (i, j))
    gate_w, p, proj_w = gate_w_all, p_all, proj_w_all
    return pl.pallas_call(
        _ple_kernel, grid=(n // tm, d // tn),
        in_specs=[pl.BlockSpec((tm, d), lambda i, j: (i, 0)), pl.BlockSpec((None, d, tn), lambda i, j: (layer, 0, j)),
                  pl.BlockSpec((None, tm, pd), lambda i, j: (layer, i, 0)),
                  pl.BlockSpec((None, pd, tn), lambda i, j: (layer, 0, j)), tile],
        out_specs=tile,
        out_shape=jax.ShapeDtypeStruct((n, d), F32),
        compiler_params=_cparams("parallel", "arbitrary"), name="ple",
    )(xn, gate_w, p, proj_w, h)


def _pad_cols(x, total):
    return jnp.pad(x, ((0, 0),) * (x.ndim - 1) + ((0, total - x.shape[-1]),))


def _pad_rows(x, total):
    return jnp.pad(x, ((0, 0),) * (x.ndim - 2) + ((0, total - x.shape[-2]), (0, 0)))


def _pack_rwkv_cols(x, width, lora):
    dl, al, gl = lora
    o = 3 * width
    return jnp.concatenate([
        x[..., :o], _pad_cols(x[..., o:o + dl], LANES), _pad_cols(x[..., o + dl:o + dl + al], LANES),
        _pad_cols(x[..., o + dl + al:o + dl + al + gl], 2 * LANES)], axis=-1)


def _transpose_cast_kernel(off_ref, valid_ref, w_ref, o_ref):
    j = pl.program_id(1)
    x = w_ref[0]
    rows = lax.broadcasted_iota(jnp.int32, x.shape, 0)
    o_ref[0] = jnp.where(rows < valid_ref[j], x, 0.0).T.astype(o_ref.dtype)


def repack_w_in(w_in, rw, lora, aw):
    depth, d, total = w_in.shape
    dl, al, gl = lora
    w_t = jnp.transpose(w_in, (0, 2, 1))
    o = 3 * rw
    base = o + dl + al + gl
    rwkv_blocks = [(c, LANES) for c in range(0, o, LANES)] + [(o, dl), (o + dl, al)]
    rwkv_blocks += [(o + dl + al + c, max(0, min(LANES, gl - c))) for c in range(0, 2 * LANES, LANES)]
    attn_blocks = [(base + c, LANES) for c in range(0, 3 * aw, LANES)]
    conv_blocks = [(base + 3 * aw + c, LANES) for c in range(0, total - base - 3 * aw, LANES)]
    outs = []
    for name, blocks in (("rwkv", rwkv_blocks), ("attn", attn_blocks), ("conv", conv_blocks)):
        assert all(off % SUBLANES == 0 and off + LANES <= total and 0 <= v <= LANES for off, v in blocks), blocks
        offs = jnp.asarray([b_[0] // SUBLANES for b_ in blocks], jnp.int32)
        valid = jnp.asarray([b_[1] for b_ in blocks], jnp.int32)
        grid_spec = pltpu.PrefetchScalarGridSpec(
            num_scalar_prefetch=2, grid=(depth, len(blocks)),
            in_specs=[pl.BlockSpec((pl.Element(1), pl.Element(LANES), pl.Element(d)),
                                   lambda l, j, off, val: (l, pl.multiple_of(off[j] * SUBLANES, SUBLANES), 0))],
            out_specs=pl.BlockSpec((1, d, LANES), lambda l, j, off, val: (l, 0, j)))
        outs.append(pl.pallas_call(
            _transpose_cast_kernel, grid_spec=grid_spec,
            out_shape=jax.ShapeDtypeStruct((depth, d, len(blocks) * LANES), BF16),
            compiler_params=_cparams("parallel", "parallel"), name=f"repack_w_{name}",
        )(offs, valid, w_t))
    return outs


def kernel(x, p, norm_mix_g, w_in, rwkv_mu, rwkv_w0, rwkv_w2, rwkv_a0, rwkv_a2, rwkv_g2, rwkv_k_k, rwkv_k_a, rwkv_r_k, rwkv_lnx_g, rwkv_lnx_b, q_norm_g, k_norm_g, rel_bias, conv_w, conv_b, conv_ln_g, conv_ln_b, w_out, norm_ffn_g, router_c_w, router_c_b, router_f_w, router_f_b, expert_w_in, expert_w_out, ple_norm_g, ple_gate_w, ple_proj):
    batch, seq, d = x.shape
    depth = w_in.shape[0]
    n = batch * seq
    rw = rwkv_w0.shape[-1]
    lora = (rwkv_w2.shape[1], rwkv_a2.shape[1], rwkv_g2.shape[1])
    assert max(lora[0], lora[1]) <= LANES and lora[2] <= 2 * LANES
    rwkv_proj = 3 * rw + sum(lora)
    cw = conv_w.shape[-1]
    aw = d - rw - cw
    npat = len(DILATED_PATTERNS)
    hpp = aw // HEAD_DIM // npat
    row = lambda v: v.reshape(1, -1)

    biases = [_pattern_bias(rel_bias[:, gi * hpp:(gi + 1) * hpp], window, dil)
              for gi, (window, dil) in enumerate(DILATED_PATTERNS)]

    w_rwkv_all, w_attn_all, w_conv_all = repack_w_in(w_in, rw, lora, aw)
    w_out_all = w_out.astype(BF16)
    ple_gate_all = ple_gate_w.astype(BF16)
    ple_proj_all = ple_proj.astype(BF16)
    p_all = p.reshape(depth, n, -1).astype(BF16)
    mu_all = _pack_rwkv_cols(rwkv_mu[:, None, :], rw, lora)
    w2_all = _pad_rows(rwkv_w2, LANES)
    a2_all = _pad_rows(rwkv_a2, LANES)
    g2_all = _pad_rows(rwkv_g2, 2 * LANES)
    router_w_all = _pad_cols(jnp.concatenate([router_c_w, router_f_w], axis=-1), ROUTER_LANES)
    router_b_all = _pad_cols(jnp.concatenate([router_c_b, router_f_b], axis=-1)[:, None, :], ROUTER_LANES)

    h = x.reshape(n, d)
    for i in range(depth):
        xn = rmsnorm_cast(h, norm_mix_g[i])
        p_rwkv = matmul(xn, w_rwkv_all, i, name="proj_rwkv")
        p_attn = matmul(xn, w_attn_all, i, name="proj_attn")
        p_conv = matmul(xn, w_conv_all, i, name="proj_conv")

        r, lw, k, v, a, b, g, bonus = rwkv_prep(
            p_rwkv, mu_all[i], row(rwkv_w0[i]), w2_all[i], row(rwkv_a0[i]), a2_all[i], g2_all[i],
            row(rwkv_k_k[i]), row(rwkv_k_a[i]), row(rwkv_r_k[i]), seq)
        y = rwkv_scan(r, lw, k, v, a, b, batch, seq)
        y_rwkv = rwkv_post(y, bonus, g, row(rwkv_lnx_g[i]), row(rwkv_lnx_b[i]))

        outs, lses = [], []
        for gi, (window, dil) in enumerate(DILATED_PATTERNS):
            attn = attn_pattern if dil == 1 else attn_dilated
            o, lse = attn(p_attn, biases[gi], row(q_norm_g[i]), row(k_norm_g[i]), gi, dil, batch, seq, aw)
            outs.append(o)
            lses.append(lse)
        y_attn = attn_mix(outs, lses)

        y_conv = conv_module(p_conv, conv_w[i], row(conv_b[i]), row(conv_ln_g[i]), row(conv_ln_b[i]), seq)

        h = matmul([y_rwkv, y_attn, y_conv], w_out_all, i, residual=h, name="out_proj")

        info, counts, xp = route(h, norm_ffn_g[i], router_w_all[i], router_b_all[i])
        n_tiles = (2 * n) // MOE_TILE + N_EXPERTS
        d1, d2, tile_expert, n_used = moe_plan(info, counts, n_tiles)
        ys = moe_ffn(xp, expert_w_in, expert_w_out, i, d1, d2, tile_expert, n_used)
        h, xn = moe_combine(ys, h, info, ple_norm_g[i], d1, d2)

        h = ple(xn, ple_gate_all, p_all, ple_proj_all, h, i)
    return h.reshape(batch, seq, d)
```

```python
import functools
import math

import numpy as np
import jax
import jax.numpy as jnp
from jax import lax
from jax.experimental import pallas as pl
from jax.experimental.pallas import tpu as pltpu

F32 = jnp.float32
BF16 = jnp.bfloat16

HEAD_DIM = 64
DILATED_PATTERNS = ((128, 1), (512, 4), (2048, 16))
ATTN_BLOCK = 128
REL_BUCKETS = 32
REL_MAX_DIST = 2048
NEG_INF = -1e30
CONV_TAPS = 31
N_GROUPS = 8
EXPERTS_PER_GROUP = 8
N_EXPERTS = N_GROUPS * EXPERTS_PER_GROUP
RMS_EPS = 1e-6
LN_EPS = 1e-5
RWKV_LN_EPS = 64e-5

LANES = 128
SUBLANES = 8
VMEM_LIMIT_BYTES = 56 * 1024 * 1024

RWKV_CHUNK = 64
RWKV_HEADS_PER_STEP = 6
RWKV_WAVE = 4
RWKV_TBLK = 1024
ATTN_GROUP = 4
CONV_HALO = 32

NN = (((1,), (0,)), ((), ()))
NT = (((1,), (1,)), ((), ()))
TN = (((0,), (0,)), ((), ()))


def _cparams(*sem):
    return pltpu.CompilerParams(dimension_semantics=sem, vmem_limit_bytes=VMEM_LIMIT_BYTES)


def _dot(a, b, dims=NN):
    return lax.dot_general(a, b, dims, preferred_element_type=F32)


def _split(x):
    hi = x.astype(BF16)
    lo = (x - hi.astype(F32)).astype(BF16)
    return hi, lo


def _dot3(a, b, dims=NN):
    ah, al = _split(a)
    bh, bl = _split(b)
    return _dot(ah, bh, dims) + (_dot(ah, bl, dims) + _dot(al, bh, dims))


def _dot3w(a, w_ref):
    ah, al = _split(a)
    return _dot(ah, w_ref[0]) + (_dot(ah, w_ref[1]) + _dot(al, w_ref[0]))


def _split_weight(w):
    hi = w.astype(BF16)
    lo = (w - hi.astype(F32)).astype(BF16)
    return jnp.stack([hi, lo], axis=-3)


def _dot2(a, b_bf16, dims=NN):
    ah, al = _split(a)
    return _dot(ah, b_bf16, dims) + _dot(al, b_bf16, dims)


def _sigmoid(x):
    return 0.5 * jnp.tanh(0.5 * x) + 0.5


def _seg_matrix():
    r = lax.broadcasted_iota(jnp.int32, (LANES, LANES), 0) // HEAD_DIM
    c = lax.broadcasted_iota(jnp.int32, (LANES, LANES), 1) // HEAD_DIM
    return (r == c).astype(BF16)


def _head_sum(x, seg):
    w = x.shape[1]
    cols = [_dot2(x[:, j:j + LANES], seg) for j in range(0, w, LANES)]
    return cols[0] if len(cols) == 1 else jnp.concatenate(cols, axis=1)


def _rmsnorm_kernel(x_ref, g_ref, o_ref):
    x = x_ref[...]
    ms = jnp.mean(x * x, axis=-1, keepdims=True)
    o_ref[...] = (x * lax.rsqrt(ms + RMS_EPS) * g_ref[...]).astype(o_ref.dtype)


def rmsnorm_cast(x, g, tm=512):
    n, d = x.shape
    tm = min(tm, n)
    return pl.pallas_call(
        _rmsnorm_kernel,
        grid=(n // tm,),
        in_specs=[pl.BlockSpec((tm, d), lambda i: (i, 0)), pl.BlockSpec((1, d), lambda i: (0, 0))],
        out_specs=pl.BlockSpec((tm, d), lambda i: (i, 0)),
        out_shape=jax.ShapeDtypeStruct((n, d), BF16),
        compiler_params=_cparams("parallel"),
        name="rmsnorm",
    )(x, g.reshape(1, d))


def _mm_kernel(a_ref, w_ref, o_ref):
    o_ref[...] = _dot(a_ref[...], w_ref[...]).astype(o_ref.dtype)


def _mm_cat_res_kernel(*refs):
    *a_refs, w_ref, r_ref, o_ref = refs
    acc = r_ref[...]
    lo = 0
    for a_ref in a_refs:
        k = a_ref.shape[1]
        acc = acc + _dot(a_ref[...], w_ref[lo:lo + k, :])
        lo += k
    o_ref[...] = acc


def _pick(n, pref):
    for t in pref:
        if n % t == 0:
            return t
    return n


def matmul(a, w_all, layer, out_dtype=F32, residual=None, tm=1024, tn=None, name="matmul"):
    pieces = list(a) if isinstance(a, (list, tuple)) else [a]
    m = pieces[0].shape[0]
    k, n = w_all.shape[1:]
    assert sum(p_.shape[1] for p_ in pieces) == k
    tm = _pick(m, (tm, 512, 256, 128))
    tn = tn or _pick(n, (1024, 768, 512, 384, 256, 128))
    grid = (m // tm, n // tn)
    a_specs = [pl.BlockSpec((tm, p_.shape[1]), lambda i, j: (i, 0)) for p_ in pieces]
    w_spec = pl.BlockSpec((None, k, tn), lambda i, j: (layer, 0, j))
    o_spec = pl.BlockSpec((tm, tn), lambda i, j: (i, j))
    if residual is None:
        assert len(pieces) == 1
        return pl.pallas_call(
            _mm_kernel, grid=grid, in_specs=a_specs + [w_spec], out_specs=o_spec,
            out_shape=jax.ShapeDtypeStruct((m, n), out_dtype),
            compiler_params=_cparams("parallel", "arbitrary"), name=name,
        )(*pieces, w_all)
    return pl.pallas_call(
        _mm_cat_res_kernel, grid=grid, in_specs=a_specs + [w_spec, o_spec], out_specs=o_spec,
        out_shape=jax.ShapeDtypeStruct((m, n), F32),
        compiler_params=_cparams("parallel", "arbitrary"), name=name,
    )(*pieces, w_all, residual)


def _rwkv_prep_kernel(x_ref, prev_ref, mu_ref, w0_ref, w2_ref, a0_ref, a2_ref, g2_ref, kk_ref, ka_ref,
                      rk_ref, r_o, lw_o, k_o, v_o, a_o, b_o, g_o, bonus_o, *, seq, width):
    t = x_ref.shape[0]
    i = pl.program_id(0)
    x = x_ref[...]
    seq_start = (i * t) % seq == 0
    prev = jnp.where(seq_start, 0.0, prev_ref[SUBLANES - 1:SUBLANES, :])
    rows = lax.broadcasted_iota(jnp.int32, x.shape, 0)
    shifted = jnp.where(rows == 0, prev, pltpu.roll(x, 1, axis=0))
    x = x + (shifted - x) * mu_ref[...]
    w = width
    r = x[:, 0:w]
    k = x[:, w:2 * w]
    v = x[:, 2 * w:3 * w]
    o = 3 * w
    w_lo = x[:, o:o + LANES]
    a_lo = x[:, o + LANES:o + 2 * LANES]
    g_lo = x[:, o + 2 * LANES:o + 4 * LANES]
    seg = _seg_matrix()

    wv = w0_ref[...] + _dot3w(jnp.tanh(w_lo), w2_ref)
    w_log = -(jnp.maximum(-wv, 0.0) + jnp.log(1.0 + jnp.exp(-jnp.abs(wv)))) - 0.5
    lw_o[...] = -jnp.exp(w_log)
    a = _sigmoid(a0_ref[...] + _dot3w(a_lo, a2_ref))
    g_o[...] = _dot3w(_sigmoid(g_lo), g2_ref).astype(g_o.dtype)

    kk = k * kk_ref[...]
    nrm = jnp.sqrt(_head_sum(kk * kk, seg))
    kk = kk / jnp.maximum(nrm, 1e-12)
    k = k * (1.0 + (a - 1.0) * ka_ref[...])
    r_o[...] = r.astype(r_o.dtype)
    k_o[...] = k.astype(k_o.dtype)
    v_o[...] = v.astype(v_o.dtype)
    a_o[...] = (-kk).astype(a_o.dtype)
    b_o[...] = (kk * a).astype(b_o.dtype)
    bonus_o[...] = (_head_sum(r * k * rk_ref[...], seg) * v).astype(bonus_o.dtype)


def rwkv_prep(proj, mu, w0, w2, a0, a2, g2, k_k, k_a, r_k, seq, tm=128):
    n, wp = proj.shape
    width = w0.shape[-1]
    tm = min(tm, seq)
    row = lambda i: (i, 0)
    fix = lambda i: (0, 0)
    fix3 = lambda i: (0, 0, 0)
    vec = pl.BlockSpec((1, width), fix)
    out = pl.BlockSpec((tm, width), row)
    per = tm // SUBLANES
    return pl.pallas_call(
        functools.partial(_rwkv_prep_kernel, seq=seq, width=width),
        grid=(n // tm,),
        in_specs=[pl.BlockSpec((tm, wp), row),
                  pl.BlockSpec((SUBLANES, wp), lambda i: (jnp.maximum(i * per - 1, 0), 0)),
                  pl.BlockSpec((1, wp), fix), vec, pl.BlockSpec(w2.shape, fix3), vec,
                  pl.BlockSpec(a2.shape, fix3), pl.BlockSpec(g2.shape, fix3), vec, vec, vec],
        out_specs=[out] * 8,
        out_shape=[jax.ShapeDtypeStruct((n, width), F32 if j == 1 else BF16) for j in range(8)],
        compiler_params=_cparams("parallel"), name="rwkv_prep",
    )(proj, proj, mu, w0, w2, a0, a2, g2, k_k, k_a, r_k)


SCAN_PASSES = dict(gram=1, gkv=1, inv=1, t=1, r=1, y=1, m=1, s=3)


def _pdot(a, b, dims, passes):
    if passes == 1:
        return _dot(a.astype(BF16), b.astype(BF16), dims)
    return _dot3(a, b, dims)


def _rwkv_scan_kernel(r_ref, lw_ref, k_ref, v_ref, a_ref, b_ref, y_ref, st_ref, *, chunk, wave):
    c_len = chunk
    hd = HEAD_DIM
    nheads = r_ref.shape[1] // hd
    ps = SCAN_PASSES

    @pl.when(pl.program_id(2) == 0)
    def _():
        st_ref[...] = jnp.zeros_like(st_ref)

    nchunks = r_ref.shape[0] // c_len
    row2 = lax.broadcasted_iota(jnp.int32, (2 * c_len, 2 * c_len), 0)
    col2 = lax.broadcasted_iota(jnp.int32, (2 * c_len, 2 * c_len), 1) % c_len
    mask2 = col2 <= jnp.where(row2 < c_len, row2 - 1, row2 - c_len)
    rowc = lax.broadcasted_iota(jnp.int32, (c_len, 2 * c_len), 0)
    colc = lax.broadcasted_iota(jnp.int32, (c_len, 2 * c_len), 1)
    right = colc >= c_len
    eye_right = (colc - c_len == rowc).astype(F32)
    zeros_h = jnp.zeros((c_len, hd), F32)
    zeros_w = jnp.zeros((c_len, 2 * c_len), F32)
    eye_k = (lax.broadcasted_iota(jnp.int32, (hd, hd), 0)
             == lax.broadcasted_iota(jnp.int32, (hd, hd), 1)).astype(F32)
    ridx = lax.broadcasted_iota(jnp.int32, (c_len, r_ref.shape[1]), 0)

    def load_units(chunks):
        units = []
        for c in chunks:
            rows = slice(c * c_len, (c + 1) * c_len)
            lw = lw_ref[rows, :]
            cum = lw
            sh = 1
            while sh < c_len:
                cum = cum + jnp.where(ridx >= sh, pltpu.roll(cum, sh, axis=0), 0.0)
                sh *= 2
            total = cum[c_len - 1:c_len, :]
            e_neg = jnp.exp(-cum)
            e_end = jnp.exp(total - cum)
            rt = r_ref[rows, :].astype(F32) * jnp.exp(cum)
            at = a_ref[rows, :].astype(F32) * jnp.exp(cum - lw)
            b = b_ref[rows, :].astype(F32)
            k = k_ref[rows, :].astype(F32)
            bt, kt, bh, kh = b * e_neg, k * e_neg, b * e_end, k * e_end
            v = v_ref[rows, :].astype(F32)
            g_end = jnp.exp(total)
            for h in range(nheads):
                sl = slice(h * hd, (h + 1) * hd)
                units.append(dict(c=c, h=h, at=at[:, sl], rt=rt[:, sl], v=v[:, sl], bt=bt[:, sl], kt=kt[:, sl],
                                  bh=bh[:, sl], kh=kh[:, sl], g=g_end[:, sl]))
        return units

    def phase1(units):
        def below(x):
            return jnp.concatenate([zeros_h, x], axis=0)

        def above(x):
            return jnp.concatenate([x, zeros_h], axis=0)

        for u in units:
            l2 = jnp.concatenate([u["at"], u["rt"]], axis=0)
            bk = jnp.concatenate([u["bt"], u["kt"]], axis=0)
            u["gm"] = jnp.where(mask2, _pdot(l2, bk, NT, ps["gram"]), 0.0)
        yield
        for u in units:
            u["gkv"] = _pdot(u["gm"], below(u["v"]), NN, ps["gkv"])
            u["w"] = jnp.where(right, eye_right, u["gm"][:c_len])
        yield
        p = 1
        while p < c_len:
            for u in units:
                w = u["w"]
                u["w"] = _pdot(w, jnp.concatenate([w, zeros_w], axis=0), NN, ps["inv"]) + jnp.where(right, w, 0.0)
            yield
            p *= 2
        for u in units:
            u["ta"] = _pdot(u["w"], below(u["at"]), NN, ps["t"])
        yield
        for u in units:
            u["u0"] = _pdot(u["w"], below(u["gkv"][:c_len]), NN, ps["t"])
        yield
        for u in units:
            u["rhat"] = u["rt"] + _pdot(u["gm"][c_len:], above(u["ta"]), NN, ps["r"])
        yield
        for u in units:
            u["y1"] = _pdot(u["gm"][c_len:], above(u["u0"]), NN, ps["r"]) + u["gkv"][c_len:]
        yield
        for u in units:
            u["m"] = eye_k * u["g"] + _pdot(u["bh"], u["ta"], TN, ps["m"])
        yield
        for u in units:
            u["nm"] = _pdot(jnp.concatenate([u["bh"], u["kh"]], axis=0),
                            jnp.concatenate([u["u0"], u["v"]], axis=0), TN, ps["m"])
        yield

    state = [st_ref[h] for h in range(nheads)]

    def phase2_steps(units):
        by_chunk = {}
        for u in units:
            by_chunk.setdefault(u["c"], []).append(u)

        def step(c):
            ys = []
            for u in by_chunk[c]:
                h = u["h"]
                ys.append(_pdot(u["rhat"], state[h], NN, ps["y"]) + u["y1"])
                state[h] = _pdot(u["m"], state[h], NN, ps["s"]) + u["nm"]
            y_ref[c * c_len:(c + 1) * c_len, :] = jnp.concatenate(ys, axis=1)

        return [functools.partial(step, c) for c in sorted(by_chunk)]

    pending = []
    for w0 in range(0, nchunks, wave):
        units = load_units(range(w0, min(w0 + wave, nchunks)))
        for _ in phase1(units):
            if pending:
                pending.pop(0)()
        for step in pending:
            step()
        pending = phase2_steps(units)
    for step in pending:
        step()
    for h in range(nheads):
        st_ref[h] = state[h]


def rwkv_scan(r, lw, k, v, a, b, batch, seq, tblk=RWKV_TBLK):
    n, width = r.shape
    tblk = min(tblk, seq)
    nt = seq // tblk
    wl = RWKV_HEADS_PER_STEP * HEAD_DIM
    assert width % wl == 0, (width, wl)
    spec = pl.BlockSpec((tblk, wl), lambda bi, hp, tb: (bi * nt + tb, hp))
    return pl.pallas_call(
        functools.partial(_rwkv_scan_kernel, chunk=RWKV_CHUNK, wave=RWKV_WAVE),
        grid=(batch, width // wl, nt),
        in_specs=[spec] * 6,
        out_specs=spec,
        out_shape=jax.ShapeDtypeStruct((n, width), F32),
        scratch_shapes=[pltpu.VMEM((RWKV_HEADS_PER_STEP, HEAD_DIM, HEAD_DIM), F32)],
        compiler_params=_cparams("parallel", "parallel", "arbitrary"), name="rwkv_scan",
    )(r, lw, k, v, a, b)


def _rwkv_post_kernel(y_ref, bonus_ref, g_ref, lg_ref, lb_ref, o_ref):
    seg = _seg_matrix()
    y = y_ref[...]
    inv_n = 1.0 / HEAD_DIM
    mean = _head_sum(y, seg) * inv_n
    yc = y - mean
    var = _head_sum(yc * yc, seg) * inv_n
    yn = yc * lax.rsqrt(var + RWKV_LN_EPS) * lg_ref[...] + lb_ref[...]
    o_ref[...] = ((yn + bonus_ref[...].astype(F32)) * g_ref[...].astype(F32)).astype(o_ref.dtype)


def rwkv_post(y, bonus, g, lnx_g, lnx_b, tm=256):
    n, width = y.shape
    tm = _pick(n, (tm, 128))
    row = pl.BlockSpec((tm, width), lambda i: (i, 0))
    vec = pl.BlockSpec((1, width), lambda i: (0, 0))
    return pl.pallas_call(
        _rwkv_post_kernel, grid=(n // tm,),
        in_specs=[row, row, row, vec, vec], out_specs=row,
        out_shape=jax.ShapeDtypeStruct((n, width), BF16),
        compiler_params=_cparams("parallel"), name="rwkv_post",
    )(y, bonus, g, lnx_g, lnx_b)


def _t5_causal_bucket(dist):
    max_exact = REL_BUCKETS // 2
    d_f = jnp.maximum(dist, 1).astype(F32)
    large = max_exact + (jnp.log(d_f / max_exact) / math.log(REL_MAX_DIST / max_exact)
                         * (REL_BUCKETS - max_exact)).astype(jnp.int32)
    large = jnp.minimum(large, REL_BUCKETS - 1)
    return jnp.where(dist < max_exact, dist, large)


def _pattern_bias(rel_bias_g, window, dil):
    span = window // dil
    qi = np.arange(ATTN_BLOCK)[:, None]
    ki = np.arange(2 * ATTN_BLOCK)[None, :]
    off = qi + ATTN_BLOCK - ki
    band = (off >= 0) & (off <= span)
    bucket = _t5_causal_bucket(jnp.asarray(np.clip(off, 0, span) * dil, jnp.int32))
    onehot = (bucket[:, :, None] == jnp.arange(REL_BUCKETS)[None, None, :]).astype(F32)
    bias = jnp.einsum("qkb,bh->hqk", onehot, rel_bias_g.astype(F32), precision=lax.Precision.HIGHEST)
    return jnp.where(jnp.asarray(band)[None], bias, NEG_INF)


def _attn_blocks(blocks, bias_ref, qg, kg, no_prev):
    hd = HEAD_DIM
    scale = hd ** -0.5
    seg = _seg_matrix()
    ones_k = jnp.ones((2 * ATTN_BLOCK, LANES), BF16)
    head_of_lane = lax.broadcasted_iota(jnp.int32, (ATTN_BLOCK, LANES), 1) // hd

    def norm(x, g):
        ms = _dot((x * x).astype(BF16), seg) * (1.0 / hd)
        return x * lax.rsqrt(ms + RMS_EPS) * g

    pairs = []
    for bi, (q, kp, kc, vp, vc) in enumerate(blocks):
        for c in range(q.shape[1] // LANES):
            sl = slice(c * LANES, (c + 1) * LANES)
            pairs.append(dict(b=bi, c=c, q=q[:, sl], k=jnp.concatenate([kp[:, sl], kc[:, sl]], axis=0),
                              v=jnp.concatenate([vp[:, sl], vc[:, sl]], axis=0)))
    for pr in pairs:
        pr["qn"] = norm(pr["q"], qg)
        pr["kn"] = norm(pr["k"], kg).astype(BF16)
        pr["vb"] = pr["v"].astype(BF16)
    units = [dict(pr=pr, h=h) for pr in pairs for h in range(LANES // hd)]
    for u in units:
        pr = u["pr"]
        qm = jnp.where(head_of_lane == u["h"], pr["qn"], 0.0).astype(BF16)
        s = _dot(qm, pr["kn"], NT) * scale + bias_ref[pr["c"] * (LANES // hd) + u["h"]]
        u["s"] = jnp.where(no_prev, NEG_INF, s)
    for u in units:
        u["m"] = jnp.max(u["s"], axis=-1, keepdims=True)
    for u in units:
        u["p"] = jnp.exp(u["s"] - u["m"]).astype(BF16)
    for u in units:
        u["den"] = _dot(u["p"], ones_k)
    for u in units:
        u["o"] = _dot(u["p"], u["pr"]["vb"]) / u["den"]
        u["lse"] = u["m"] + jnp.log(u["den"])
    res = []
    for bi in range(len(blocks)):
        outs, lses = [], []
        for pr in (p_ for p_ in pairs if p_["b"] == bi):
            u0, u1 = [u for u in units if u["pr"] is pr]
            outs.append(jnp.where(head_of_lane == 0, u0["o"], u1["o"]))
            lses.append(jnp.where(head_of_lane == 0, u0["lse"], u1["lse"]))
        res.append((outs[0] if len(outs) == 1 else jnp.concatenate(outs, axis=1),
                    lses[0] if len(lses) == 1 else jnp.concatenate(lses, axis=1)))
    return res


def _no_prev_mask():
    ki = lax.broadcasted_iota(jnp.int32, (ATTN_BLOCK, 2 * ATTN_BLOCK), 1)
    return jnp.logical_and(pl.program_id(1) == 0, ki < ATTN_BLOCK)


def _attn_kernel(q_ref, kp_ref, kc_ref, vp_ref, vc_ref, bias_ref, qg_ref, kg_ref, o_ref, lse_ref):
    (o, lse), = _attn_blocks([(q_ref[...], kp_ref[...], kc_ref[...], vp_ref[...], vc_ref[...])], bias_ref,
                             qg_ref[...], kg_ref[...], _no_prev_mask())
    o_ref[...] = o
    lse_ref[...] = lse


def _attn_dilated_kernel(q_ref, kp_ref, kc_ref, vp_ref, vc_ref, bias_ref, qg_ref, kg_ref, o_ref, lse_ref, *, dil):
    no_prev = _no_prev_mask()
    group = min(dil, ATTN_GROUP)

    def body(rg, carry):
        rows = [pl.ds(rg * group + j, ATTN_BLOCK, stride=dil) for j in range(group)]
        blocks = [(q_ref[rw, :], kp_ref[rw, :], kc_ref[rw, :], vp_ref[rw, :], vc_ref[rw, :]) for rw in rows]
        for rw, (o, lse) in zip(rows, _attn_blocks(blocks, bias_ref, qg_ref[...], kg_ref[...], no_prev)):
            o_ref[rw, :] = o
            lse_ref[rw, :] = lse
        return carry

    lax.fori_loop(0, dil // group, body, 0)


def attn_dilated(proj, bias, q_g, k_g, gi, dil, batch, seq, width):
    n = proj.shape[0]
    npat = len(DILATED_PATTERNS)
    pw = width // npat
    rows = ATTN_BLOCK * dil
    nb = seq // rows
    sec = width // LANES
    hp_per = pw // LANES
    heads_pp = LANES // HEAD_DIM

    def spec(section, prev):
        def imap(b, nblk, hp):
            blk = jnp.maximum(nblk - 1, 0) if prev else nblk
            return (b * nb + blk, section * sec + gi * hp_per + hp)
        return pl.BlockSpec((rows, LANES), imap)

    bias_spec = pl.BlockSpec((heads_pp,) + bias.shape[1:], lambda b, nblk, hp: (hp, 0, 0))
    vec = pl.BlockSpec((1, LANES), lambda b, nblk, hp: (0, 0))
    q_g, k_g = (jnp.tile(g, (1, heads_pp)) for g in (q_g, k_g))
    ospec = pl.BlockSpec((rows, LANES), lambda b, nblk, hp: (b * nb + nblk, hp))
    return pl.pallas_call(
        functools.partial(_attn_dilated_kernel, dil=dil), grid=(batch, nb, hp_per),
        in_specs=[spec(0, False), spec(1, True), spec(1, False), spec(2, True), spec(2, False), bias_spec, vec, vec],
        out_specs=[ospec, ospec],
        out_shape=[jax.ShapeDtypeStruct((n, pw), F32)] * 2,
        compiler_params=_cparams("parallel", "parallel", "parallel"), name=f"attn_d{dil}",
    )(proj, proj, proj, proj, proj, bias, q_g, k_g)


def attn_pattern(proj, bias, q_g, k_g, gi, dil, batch, seq, width):
    n = proj.shape[0]
    npat = len(DILATED_PATTERNS)
    pw = width // npat
    nsub = seq // dil
    nb = nsub // ATTN_BLOCK
    sec = width // pw
    rowlen = 3 * sec
    x = proj.reshape(batch * nsub, dil * 3 * width)

    def spec(section, prev):
        def imap(b, nblk, r):
            blk = jnp.maximum(nblk - 1, 0) if prev else nblk
            return (b * nb + blk, r * rowlen + section * sec + gi)
        return pl.BlockSpec((ATTN_BLOCK, pw), imap)

    fix3 = pl.BlockSpec(bias.shape, lambda b, nblk, r: (0, 0, 0))
    vec = pl.BlockSpec((1, LANES), lambda b, nblk, r: (0, 0))
    q_g, k_g = (jnp.tile(g, (1, LANES // HEAD_DIM)) for g in (q_g, k_g))
    ospec = pl.BlockSpec((ATTN_BLOCK, pw), lambda b, nblk, r: (b * nb + nblk, r))
    o, lse = pl.pallas_call(
        _attn_kernel, grid=(batch, nb, dil),
        in_specs=[spec(0, False), spec(1, True), spec(1, False), spec(2, True), spec(2, False), fix3, vec, vec],
        out_specs=[ospec, ospec],
        out_shape=[jax.ShapeDtypeStruct((batch * nsub, dil * pw), F32)] * 2,
        compiler_params=_cparams("parallel", "parallel", "parallel"), name=f"attn_d{dil}",
    )(x, x, x, x, x, bias, q_g, k_g)
    return o.reshape(n, pw), lse.reshape(n, pw)


def _attn_mix_kernel(o0, o1, o2, l0, l1, l2, out_ref):
    a, b, c = l0[...], l1[...], l2[...]
    m = jnp.maximum(jnp.maximum(a, b), c)
    ea, eb, ec = jnp.exp(a - m), jnp.exp(b - m), jnp.exp(c - m)
    inv = 1.0 / (ea + eb + ec)
    out_ref[...] = jnp.concatenate([o0[...] * (ea * inv), o1[...] * (eb * inv), o2[...] * (ec * inv)],
                                   axis=1).astype(out_ref.dtype)


def attn_mix(outs, lses, tm=512):
    n, pw = outs[0].shape
    tm = _pick(n, (tm, 256, 128))
    row = pl.BlockSpec((tm, pw), lambda i: (i, 0))
    return pl.pallas_call(
        _attn_mix_kernel, grid=(n // tm,), in_specs=[row] * 6,
        out_specs=pl.BlockSpec((tm, 3 * pw), lambda i: (i, 0)),
        out_shape=jax.ShapeDtypeStruct((n, 3 * pw), BF16),
        compiler_params=_cparams("parallel"), name="attn_mix",
    )(*outs, *lses)


def _conv_kernel(x_ref, halo_ref, w_ref, b_ref, lg_ref, lb_ref, o_ref, ext_ref, shift_ref, *, seq, width):
    t = x_ref.shape[0]
    i = pl.program_id(0)
    seq_start = (i * t) % seq == 0

    def glu(x):
        return x[:, :width] * _sigmoid(x[:, width:])

    ext_ref[0:CONV_HALO, :] = jnp.where(seq_start, 0.0, glu(halo_ref[...]))
    ext_ref[CONV_HALO:, :] = glu(x_ref[...])
    base = CONV_HALO - (CONV_TAPS - 1)
    acc = jnp.broadcast_to(b_ref[...], (t, width))
    for rho in range(SUBLANES):
        offs = [base + j for j in range(CONV_TAPS) if (base + j) % SUBLANES == rho]
        if not offs:
            continue
        lo = offs[0]
        span = offs[-1] - lo + t
        shift_ref[0:span, :] = ext_ref[lo:lo + span, :]
        for o in offs:
            acc = acc + w_ref[o - base:o - base + 1, :] * shift_ref[o - lo:o - lo + t, :]
    mu = jnp.mean(acc, axis=-1, keepdims=True)
    d = acc - mu
    var = jnp.mean(d * d, axis=-1, keepdims=True)
    y = d * lax.rsqrt(var + LN_EPS) * lg_ref[...] + lb_ref[...]
    o_ref[...] = (y * _sigmoid(y)).astype(o_ref.dtype)


def conv_module(proj, conv_w, conv_b, ln_g, ln_b, seq, tm=256):
    n, w2 = proj.shape
    width = w2 // 2
    tm = min(tm, seq)
    per = tm // CONV_HALO
    fix = lambda i: (0, 0)
    vec = pl.BlockSpec((1, width), fix)
    return pl.pallas_call(
        functools.partial(_conv_kernel, seq=seq, width=width), grid=(n // tm,),
        in_specs=[pl.BlockSpec((tm, w2), lambda i: (i, 0)),
                  pl.BlockSpec((CONV_HALO, w2), lambda i: (jnp.maximum(i * per - 1, 0), 0)),
                  pl.BlockSpec((CONV_TAPS, width), fix), vec, vec, vec],
        out_specs=pl.BlockSpec((tm, width), lambda i: (i, 0)),
        out_shape=jax.ShapeDtypeStruct((n, width), BF16),
        scratch_shapes=[pltpu.VMEM((tm + CONV_HALO, width), F32), pltpu.VMEM((tm + CONV_HALO, width), F32)],
        compiler_params=_cparams("parallel"), name="conv_module",
    )(proj, proj, conv_w, conv_b, ln_g, ln_b)


ROUTER_LANES = LANES


def _pack_bf16_pairs(x):
    w = x.shape[1] // 2
    hi = lax.bitcast_convert_type(x[:, :w].astype(BF16).astype(F32), jnp.uint32)
    lo = lax.bitcast_convert_type(x[:, w:].astype(BF16).astype(F32), jnp.uint32)
    return hi | (lo >> 16)


def _unpack_bf16_pairs(words):
    hi = lax.bitcast_convert_type(words & jnp.uint32(0xFFFF0000), F32)
    lo = lax.bitcast_convert_type(words << 16, F32)
    return hi, lo


def _store_token_major(ref, words):
    t = words.shape[0]
    chunks = words.shape[1] // LANES
    for c in range(chunks):
        ref[pl.ds(c, t, stride=chunks), :] = words[:, c * LANES:(c + 1) * LANES]


def _load_token_major(ref, t, chunks):
    return jnp.concatenate([ref[pl.ds(c, t, stride=chunks), :] for c in range(chunks)], axis=1)


def _route_kernel(x_ref, g_ref, rw_ref, bias_ref, info_ref, cnt_ref, xp_ref, run_ref):
    t = x_ref.shape[0]

    @pl.when(pl.program_id(0) == 0)
    def _():
        run_ref[...] = jnp.zeros_like(run_ref)

    x = x_ref[...]
    ms = jnp.mean(x * x, axis=-1, keepdims=True)
    xn = x * lax.rsqrt(ms + RMS_EPS) * g_ref[...]
    _store_token_major(xp_ref, _pack_bf16_pairs(xn))
    z = _dot3w(xn, rw_ref) + bias_ref[...]
    lane = lax.broadcasted_iota(jnp.int32, z.shape, 1).astype(F32)
    ninf = -jnp.inf
    big = float(ROUTER_LANES)
    zc = jnp.where(lane < N_GROUPS, z, ninf)
    mc = jnp.max(zc, axis=-1, keepdims=True)
    pg = 1.0 / jnp.sum(jnp.exp(zc - mc), axis=-1, keepdims=True)
    g_idx = jnp.min(jnp.where(zc == mc, lane, big), axis=-1, keepdims=True)
    lo = N_GROUPS + g_idx * EXPERTS_PER_GROUP
    zf = jnp.where(jnp.logical_and(lane >= lo, lane < lo + EXPERTS_PER_GROUP), z, ninf)
    m1 = jnp.max(zf, axis=-1, keepdims=True)
    i1 = jnp.min(jnp.where(zf == m1, lane, big), axis=-1, keepdims=True)
    zf2 = jnp.where(lane == i1, ninf, zf)
    m2 = jnp.max(zf2, axis=-1, keepdims=True)
    i2 = jnp.min(jnp.where(zf2 == m2, lane, big), axis=-1, keepdims=True)
    e2 = jnp.exp(m2 - m1)
    w1 = pg / (1.0 + e2)
    w2 = pg * e2 / (1.0 + e2)
    onehot = jnp.where(jnp.logical_or(lane == i1, lane == i2), 1.0, 0.0)
    tri = (lax.broadcasted_iota(jnp.int32, (t, t), 0) > lax.broadcasted_iota(jnp.int32, (t, t), 1)).astype(BF16)
    before = _dot(tri, onehot.astype(BF16)) + run_ref[...]
    r1 = jnp.sum(jnp.where(lane == i1, before, 0.0), axis=-1, keepdims=True)
    r2 = jnp.sum(jnp.where(lane == i2, before, 0.0), axis=-1, keepdims=True)
    run_ref[...] += jnp.sum(onehot, axis=0, keepdims=True)
    cnt_ref[...] = run_ref[...]
    cols = (i1 - N_GROUPS, i2 - N_GROUPS, w1, w2, r1, r2)
    info = jnp.zeros_like(z)
    for j, c in enumerate(cols):
        info = jnp.where(lane == float(j), c, info)
    info_ref[...] = info


def route(x, g, router_w, bias, tm=256):
    n, d = x.shape
    w = router_w.shape[-1]
    chunks = d // 2 // LANES
    tm = _pick(n, (tm, 128))
    fix = lambda i: (0, 0)
    row = pl.BlockSpec((tm, w), lambda i: (i, 0))
    one = pl.BlockSpec((1, w), fix)
    return pl.pallas_call(
        _route_kernel, grid=(n // tm,),
        in_specs=[pl.BlockSpec((tm, d), lambda i: (i, 0)), pl.BlockSpec((1, d), fix),
                  pl.BlockSpec((2, d, w), lambda i: (0, 0, 0)), one],
        out_specs=[row, one, pl.BlockSpec((tm * chunks, LANES), lambda i: (i, 0))],
        out_shape=[jax.ShapeDtypeStruct((n, w), F32), jax.ShapeDtypeStruct((1, w), F32),
                   jax.ShapeDtypeStruct((n * chunks, LANES), jnp.uint32)],
        scratch_shapes=[pltpu.VMEM((1, w), F32)],
        compiler_params=_cparams("arbitrary"), name="route",
    )(x, g.reshape(1, d), router_w, bias)


MOE_TILE = 256


def _moe_dest_kernel(info_ref, off_ref, dd_ref):
    info = info_ref[...]
    lane = lax.broadcasted_iota(jnp.int32, info.shape, 1).astype(F32)
    off = off_ref[...]

    def dest(e, r):
        return jnp.sum(jnp.where(lane == e + N_GROUPS, off, 0.0), axis=-1, keepdims=True) + r

    d1 = dest(info[:, 0:1], info[:, 4:5])
    d2 = dest(info[:, 1:2], info[:, 5:6])
    dd_ref[...] = jnp.where(lane == 0.0, d1, jnp.where(lane == 1.0, d2, 0.0)).astype(jnp.int32)


def moe_plan(info, counts, n_tiles, tm=512):
    n, w = info.shape
    cnt = counts[0, N_GROUPS:N_GROUPS + N_EXPERTS].astype(jnp.int32)
    padded = (cnt + MOE_TILE - 1) // MOE_TILE * MOE_TILE
    ends = jnp.cumsum(padded)
    off_row = jnp.pad((ends - padded).astype(F32), (N_GROUPS, w - N_GROUPS - N_EXPERTS)).reshape(1, w)
    tm = _pick(n, (tm, 256, 128))
    row = pl.BlockSpec((tm, w), lambda i: (i, 0))
    dd = pl.pallas_call(
        _moe_dest_kernel, grid=(n // tm,),
        in_specs=[row, pl.BlockSpec((1, w), lambda i: (0, 0))], out_specs=row,
        out_shape=jax.ShapeDtypeStruct((n, w), jnp.int32),
        compiler_params=_cparams("parallel"), name="moe_dest",
    )(info, off_row)
    tile_start = jnp.arange(n_tiles, dtype=jnp.int32) * MOE_TILE
    tile_expert = jnp.minimum(jnp.sum((tile_start[:, None] >= ends[None, :]).astype(jnp.int32), axis=1),
                              N_EXPERTS - 1)
    n_used = (ends[-1:] // MOE_TILE).astype(jnp.int32)
    return dd[:, 0], dd[:, 1], tile_expert, n_used


DMA_ISSUE_UNROLL = 8


def _moe_ffn_kernel(d1_ref, d2_ref, texp_ref, nused_ref, xp_hbm, wi_ref, wo_ref, ys_ref,
                    src_ref, xbuf, sem, wi_bf, wo_bf, *, n_tok, ff):
    i = pl.program_id(0)
    n_used = nused_ref[0]
    n_rows = src_ref.shape[0]
    half = wi_ref.shape[1] // 2
    chunks = half // LANES

    def gather(tile, slot):
        def body(j, carry):
            tok = src_ref[tile * MOE_TILE + j]
            pltpu.make_async_copy(xp_hbm.at[pl.ds(tok * chunks, chunks), :],
                                  xbuf.at[slot, pl.ds(j * chunks, chunks), :], sem.at[slot]).start()
            return carry
        lax.fori_loop(0, MOE_TILE, body, 0, unroll=DMA_ISSUE_UNROLL)

    @pl.when(i == 0)
    def _():
        def clear(j, carry):
            src_ref[j] = 0
            return carry
        lax.fori_loop(0, n_rows, clear, 0, unroll=DMA_ISSUE_UNROLL)

        def fill(tok, carry):
            src_ref[d1_ref[tok]] = tok
            src_ref[d2_ref[tok]] = tok
            return carry
        lax.fori_loop(0, n_tok, fill, 0, unroll=DMA_ISSUE_UNROLL)
        gather(0, 0)

    @pl.when(i < n_used)
    def _():
        slot = i % 2

        @pl.when(i + 1 < n_used)
        def _():
            gather(i + 1, 1 - slot)

        pltpu.make_async_copy(xp_hbm.at[pl.ds(0, MOE_TILE * chunks), :], xbuf.at[slot], sem.at[slot]).wait()

        @pl.when(jnp.logical_or(i == 0, texp_ref[i] != texp_ref[jnp.maximum(i - 1, 0)]))
        def _():
            wi_bf[...] = wi_ref[0].astype(BF16)
            wo_bf[...] = wo_ref[0].astype(BF16)

        x_a, x_b = _unpack_bf16_pairs(_load_token_major(xbuf.at[slot], MOE_TILE, chunks))
        gu = _dot(x_a.astype(BF16), wi_bf[:half, :]) + _dot(x_b.astype(BF16), wi_bf[half:, :])
        gt, up = gu[:, :ff], gu[:, ff:]
        hid = gt * _sigmoid(gt) * up
        _store_token_major(ys_ref, _pack_bf16_pairs(_dot(hid.astype(BF16), wo_bf[...])))

    @pl.when(i >= n_used)
    def _():
        ys_ref[...] = jnp.zeros_like(ys_ref)


def moe_ffn(xp, w_in_all, w_out_all, layer, d1, d2, tile_expert, n_used):
    w_in, w_out = w_in_all, w_out_all
    _, ne, d, ff2 = w_in.shape
    chunks = d // 2 // LANES
    n = xp.shape[0] // chunks
    n_tiles = tile_expert.shape[0]
    n_rows = n_tiles * MOE_TILE
    last = lambda i, nu: jnp.minimum(i, nu[0] - 1)
    grid_spec = pltpu.PrefetchScalarGridSpec(
        num_scalar_prefetch=4, grid=(n_tiles,),
        in_specs=[pl.BlockSpec(memory_space=pl.ANY),
                  pl.BlockSpec((None, 1, d, ff2), lambda i, d1, d2, te, nu: (layer, te[last(i, nu)], 0, 0)),
                  pl.BlockSpec((None, 1, ff2 // 2, d), lambda i, d1, d2, te, nu: (layer, te[last(i, nu)], 0, 0))],
        out_specs=pl.BlockSpec((MOE_TILE * chunks, LANES), lambda i, d1, d2, te, nu: (i, 0)),
        scratch_shapes=[pltpu.SMEM((n_rows,), jnp.int32), pltpu.VMEM((2, MOE_TILE * chunks, LANES), jnp.uint32),
                        pltpu.SemaphoreType.DMA((2,)), pltpu.VMEM((d, ff2), BF16), pltpu.VMEM((ff2 // 2, d), BF16)])
    return pl.pallas_call(
        functools.partial(_moe_ffn_kernel, n_tok=n, ff=ff2 // 2), grid_spec=grid_spec,
        out_shape=jax.ShapeDtypeStruct((n_rows * chunks, LANES), jnp.uint32),
        compiler_params=_cparams("arbitrary"), name="moe_ffn",
    )(d1, d2, tile_expert, n_used, xp, w_in, w_out)


def _moe_combine_kernel(d1_ref, d2_ref, ys_hbm, h_ref, info_ref, g_ref, o_ref, xn_ref, ybuf, sem):
    t = h_ref.shape[0]
    half = h_ref.shape[1] // 2
    chunks = half // LANES
    i = pl.program_id(0)
    slot = i % 2

    def gather(tile, slot_):
        def issue(j, carry):
            dst = pl.ds(j * chunks, chunks)
            tok = tile * t + j
            pltpu.make_async_copy(ys_hbm.at[pl.ds(d1_ref[tok] * chunks, chunks), :], ybuf.at[slot_, 0, dst, :],
                                  sem.at[slot_, 0]).start()
            pltpu.make_async_copy(ys_hbm.at[pl.ds(d2_ref[tok] * chunks, chunks), :], ybuf.at[slot_, 1, dst, :],
                                  sem.at[slot_, 1]).start()
            return carry
        lax.fori_loop(0, t, issue, 0, unroll=DMA_ISSUE_UNROLL)

    @pl.when(i == 0)
    def _():
        gather(0, 0)

    @pl.when(i + 1 < pl.num_programs(0))
    def _():
        gather(i + 1, 1 - slot)

    for s in range(2):
        pltpu.make_async_copy(ys_hbm.at[pl.ds(0, t * chunks), :], ybuf.at[slot, s], sem.at[slot, s]).wait()
    info = info_ref[...]
    w1, w2 = info[:, 2:3], info[:, 3:4]
    y1a, y1b = _unpack_bf16_pairs(_load_token_major(ybuf.at[slot, 0], t, chunks))
    y2a, y2b = _unpack_bf16_pairs(_load_token_major(ybuf.at[slot, 1], t, chunks))
    h_a = h_ref[:, :half] + w1 * y1a + w2 * y2a
    h_b = h_ref[:, half:] + w1 * y1b + w2 * y2b
    o_ref[:, :half] = h_a
    o_ref[:, half:] = h_b
    ms = (jnp.sum(h_a * h_a, axis=-1, keepdims=True) + jnp.sum(h_b * h_b, axis=-1, keepdims=True)) / (2 * half)
    scale = lax.rsqrt(ms + RMS_EPS)
    xn_ref[:, :half] = (h_a * scale * g_ref[:, :half]).astype(xn_ref.dtype)
    xn_ref[:, half:] = (h_b * scale * g_ref[:, half:]).astype(xn_ref.dtype)


def moe_combine(ys, h, info, g_next, d1, d2, tm=256):
    n, d = h.shape
    tm = _pick(n, (tm, 128))
    row = lambda i, d1, d2: (i, 0)
    tile = pl.BlockSpec((tm, d), row)
    grid_spec = pltpu.PrefetchScalarGridSpec(
        num_scalar_prefetch=2, grid=(n // tm,),
        in_specs=[pl.BlockSpec(memory_space=pl.ANY), tile, pl.BlockSpec((tm, info.shape[1]), row),
                  pl.BlockSpec((1, d), lambda i, d1, d2: (0, 0))],
        out_specs=[tile, tile],
        scratch_shapes=[pltpu.VMEM((2, 2, tm * (d // 2 // LANES), LANES), jnp.uint32),
                        pltpu.SemaphoreType.DMA((2, 2))])
    return pl.pallas_call(
        _moe_combine_kernel, grid_spec=grid_spec,
        out_shape=[jax.ShapeDtypeStruct((n, d), F32), jax.ShapeDtypeStruct((n, d), BF16)],
        compiler_params=_cparams("arbitrary"), name="moe_combine",
    )(d1, d2, ys, h, info, g_next.reshape(1, d))


def _ple_kernel(a_ref, w_ref, p_ref, pw_ref, h_ref, o_ref):
    gate = _sigmoid(_dot(a_ref[...], w_ref[...]))
    o_ref[...] = h_ref[...] + _dot(p_ref[...], pw_ref[...]) * gate


def ple(xn, gate_w_all, p_all, proj_w_all, h, layer, tm=1024, tn=512):
    n, d = xn.shape
    pd = p_all.shape[2]
    tm = _pick(n, (tm, 512, 256, 128))
    tn = _pick(d, (tn, 256, 128))
    tile = pl.BlockSpec((tm, tn), lambda i, j: (i, j))
    gate_w, p, proj_w = gate_w_all, p_all, proj_w_all
    return pl.pallas_call(
        _ple_kernel, grid=(n // tm, d // tn),
        in_specs=[pl.BlockSpec((tm, d), lambda i, j: (i, 0)), pl.BlockSpec((None, d, tn), lambda i, j: (layer, 0, j)),
                  pl.BlockSpec((None, tm, pd), lambda i, j: (layer, i, 0)),
                  pl.BlockSpec((None, pd, tn), lambda i, j: (layer, 0, j)), tile],
        out_specs=tile,
        out_shape=jax.ShapeDtypeStruct((n, d), F32),
        compiler_params=_cparams("parallel", "arbitrary"), name="ple",
    )(xn, gate_w, p, proj_w, h)


def _pad_cols(x, total):
    return jnp.pad(x, ((0, 0),) * (x.ndim - 1) + ((0, total - x.shape[-1]),))


def _pad_rows(x, total):
    return jnp.pad(x, ((0, 0),) * (x.ndim - 2) + ((0, total - x.shape[-2]), (0, 0)))


def _pack_rwkv_cols(x, width, lora):
    dl, al, gl = lora
    o = 3 * width
    return jnp.concatenate([
        x[..., :o], _pad_cols(x[..., o:o + dl], LANES), _pad_cols(x[..., o + dl:o + dl + al], LANES),
        _pad_cols(x[..., o + dl + al:o + dl + al + gl], 2 * LANES)], axis=-1)


def _transpose_cast_kernel(off_ref, valid_ref, w_ref, o_ref):
    j = pl.program_id(1)
    x = w_ref[0]
    rows = lax.broadcasted_iota(jnp.int32, x.shape, 0)
    o_ref[0] = jnp.where(rows < valid_ref[j], x, 0.0).T.astype(o_ref.dtype)


def repack_w_in(w_in, rw, lora, aw):
    depth, d, total = w_in.shape
    dl, al, gl = lora
    w_t = jnp.transpose(w_in, (0, 2, 1))
    o = 3 * rw
    base = o + dl + al + gl
    rwkv_blocks = [(c, LANES) for c in range(0, o, LANES)] + [(o, dl), (o + dl, al)]
    rwkv_blocks += [(o + dl + al + c, max(0, min(LANES, gl - c))) for c in range(0, 2 * LANES, LANES)]
    attn_blocks = [(base + c, LANES) for c in range(0, 3 * aw, LANES)]
    conv_blocks = [(base + 3 * aw + c, LANES) for c in range(0, total - base - 3 * aw, LANES)]
    outs = []
    for name, blocks in (("rwkv", rwkv_blocks), ("attn", attn_blocks), ("conv", conv_blocks)):
        assert all(off % SUBLANES == 0 and off + LANES <= total and 0 <= v <= LANES for off, v in blocks), blocks
        offs = jnp.asarray([b_[0] // SUBLANES for b_ in blocks], jnp.int32)
        valid = jnp.asarray([b_[1] for b_ in blocks], jnp.int32)
        grid_spec = pltpu.PrefetchScalarGridSpec(
            num_scalar_prefetch=2, grid=(depth, len(blocks)),
            in_specs=[pl.BlockSpec((pl.Element(1), pl.Element(LANES), pl.Element(d)),
                                   lambda l, j, off, val: (l, pl.multiple_of(off[j] * SUBLANES, SUBLANES), 0))],
            out_specs=pl.BlockSpec((1, d, LANES), lambda l, j, off, val: (l, 0, j)))
        outs.append(pl.pallas_call(
            _transpose_cast_kernel, grid_spec=grid_spec,
            out_shape=jax.ShapeDtypeStruct((depth, d, len(blocks) * LANES), BF16),
            compiler_params=_cparams("parallel", "parallel"), name=f"repack_w_{name}",
        )(offs, valid, w_t))
    return outs


def kernel(x, p, norm_mix_g, w_in, rwkv_mu, rwkv_w0, rwkv_w2, rwkv_a0, rwkv_a2, rwkv_g2, rwkv_k_k, rwkv_k_a, rwkv_r_k, rwkv_lnx_g, rwkv_lnx_b, q_norm_g, k_norm_g, rel_bias, conv_w, conv_b, conv_ln_g, conv_ln_b, w_out, norm_ffn_g, router_c_w, router_c_b, router_f_w, router_f_b, expert_w_in, expert_w_out, ple_norm_g, ple_gate_w, ple_proj):
    batch, seq, d = x.shape
    depth = w_in.shape[0]
    n = batch * seq
    rw = rwkv_w0.shape[-1]
    lora = (rwkv_w2.shape[1], rwkv_a2.shape[1], rwkv_g2.shape[1])
    assert max(lora[0], lora[1]) <= LANES and lora[2] <= 2 * LANES
    rwkv_proj = 3 * rw + sum(lora)
    cw = conv_w.shape[-1]
    aw = d - rw - cw
    npat = len(DILATED_PATTERNS)
    hpp = aw // HEAD_DIM // npat
    row = lambda v: v.reshape(1, -1)

    biases = [_pattern_bias(rel_bias[:, gi * hpp:(gi + 1) * hpp], window, dil)
              for gi, (window, dil) in enumerate(DILATED_PATTERNS)]

    w_rwkv_all, w_attn_all, w_conv_all = repack_w_in(w_in, rw, lora, aw)
    w_out_all = w_out.astype(BF16)
    ple_gate_all = ple_gate_w.astype(BF16)
    ple_proj_all = ple_proj.astype(BF16)
    p_all = p.reshape(depth, n, -1).astype(BF16)
    mu_all = _pack_rwkv_cols(rwkv_mu[:, None, :], rw, lora)
    w2_all = _split_weight(_pad_rows(rwkv_w2, LANES))
    a2_all = _split_weight(_pad_rows(rwkv_a2, LANES))
    g2_all = _split_weight(_pad_rows(rwkv_g2, 2 * LANES))
    router_w_all = _split_weight(_pad_cols(jnp.concatenate([router_c_w, router_f_w], axis=-1), ROUTER_LANES))
    router_b_all = _pad_cols(jnp.concatenate([router_c_b, router_f_b], axis=-1)[:, None, :], ROUTER_LANES)

    h = x.reshape(n, d)
    for i in range(depth):
        xn = rmsnorm_cast(h, norm_mix_g[i])
        p_rwkv = matmul(xn, w_rwkv_all, i, name="proj_rwkv")
        p_attn = matmul(xn, w_attn_all, i, name="proj_attn")
        p_conv = matmul(xn, w_conv_all, i, name="proj_conv")

        r, lw, k, v, a, b, g, bonus = rwkv_prep(
            p_rwkv, mu_all[i], row(rwkv_w0[i]), w2_all[i], row(rwkv_a0[i]), a2_all[i], g2_all[i],
            row(rwkv_k_k[i]), row(rwkv_k_a[i]), row(rwkv_r_k[i]), seq)
        y = rwkv_scan(r, lw, k, v, a, b, batch, seq)
        y_rwkv = rwkv_post(y, bonus, g, row(rwkv_lnx_g[i]), row(rwkv_lnx_b[i]))

        outs, lses = [], []
        for gi, (window, dil) in enumerate(DILATED_PATTERNS):
            attn = attn_pattern if dil == 1 else attn_dilated
            o, lse = attn(p_attn, biases[gi], row(q_norm_g[i]), row(k_norm_g[i]), gi, dil, batch, seq, aw)
            outs.append(o)
            lses.append(lse)
        y_attn = attn_mix(outs, lses)

        y_conv = conv_module(p_conv, conv_w[i], row(conv_b[i]), row(conv_ln_g[i]), row(conv_ln_b[i]), seq)

        h = matmul([y_rwkv, y_attn, y_conv], w_out_all, i, residual=h, name="out_proj")

        info, counts, xp = route(h, norm_ffn_g[i], router_w_all[i], router_b_all[i])
        n_tiles = (2 * n) // MOE_TILE + N_EXPERTS
        d1, d2, tile_expert, n_used = moe_plan(info, counts, n_tiles)
        ys = moe_ffn(xp, expert_w_in, expert_w_out, i, d1, d2, tile_expert, n_used)
        h, xn = moe_combine(ys, h, info, ple_norm_g[i], d1, d2)

        h = ple(xn, ple_gate_all, p_all, ple_proj_all, h, i)
    return h.reshape(batch, seq, d)
```

```python
import functools
import math

import numpy as np
import jax
import jax.numpy as jnp
from jax import lax
from jax.experimental import pallas as pl
from jax.experimental.pallas import tpu as pltpu

F32 = jnp.float32
BF16 = jnp.bfloat16

HEAD_DIM = 64
DILATED_PATTERNS = ((128, 1), (512, 4), (2048, 16))
ATTN_BLOCK = 128
REL_BUCKETS = 32
REL_MAX_DIST = 2048
NEG_INF = -1e30
CONV_TAPS = 31
N_GROUPS = 8
EXPERTS_PER_GROUP = 8
N_EXPERTS = N_GROUPS * EXPERTS_PER_GROUP
RMS_EPS = 1e-6
LN_EPS = 1e-5
RWKV_LN_EPS = 64e-5

LANES = 128
SUBLANES = 8
VMEM_LIMIT_BYTES = 56 * 1024 * 1024

RWKV_CHUNK = 64
RWKV_HEADS_PER_STEP = 6
RWKV_WAVE = 4
RWKV_TBLK = 1024
ATTN_GROUP = 4
CONV_HALO = 32

NN = (((1,), (0,)), ((), ()))
NT = (((1,), (1,)), ((), ()))
TN = (((0,), (0,)), ((), ()))


def _cparams(*sem):
    return pltpu.CompilerParams(dimension_semantics=sem, vmem_limit_bytes=VMEM_LIMIT_BYTES)


def _dot(a, b, dims=NN):
    return lax.dot_general(a, b, dims, preferred_element_type=F32)


def _split(x):
    hi = x.astype(BF16)
    lo = (x - hi.astype(F32)).astype(BF16)
    return hi, lo


def _dot3(a, b, dims=NN):
    ah, al = _split(a)
    bh, bl = _split(b)
    return _dot(ah, bh, dims) + (_dot(ah, bl, dims) + _dot(al, bh, dims))


def _dot3w(a, w_ref):
    ah, al = _split(a)
    return _dot(ah, w_ref[0]) + (_dot(ah, w_ref[1]) + _dot(al, w_ref[0]))


def _split_weight(w):
    hi = w.astype(BF16)
    lo = (w - hi.astype(F32)).astype(BF16)
    return jnp.stack([hi, lo], axis=-3)


def _dot2(a, b_bf16, dims=NN):
    ah, al = _split(a)
    return _dot(ah, b_bf16, dims) + _dot(al, b_bf16, dims)


def _sigmoid(x):
    return 0.5 * jnp.tanh(0.5 * x) + 0.5


def _seg_matrix():
    r = lax.broadcasted_iota(jnp.int32, (LANES, LANES), 0) // HEAD_DIM
    c = lax.broadcasted_iota(jnp.int32, (LANES, LANES), 1) // HEAD_DIM
    return (r == c).astype(BF16)


def _head_sum(x, seg):
    w = x.shape[1]
    cols = [_dot2(x[:, j:j + LANES], seg) for j in range(0, w, LANES)]
    return cols[0] if len(cols) == 1 else jnp.concatenate(cols, axis=1)


def _rmsnorm_kernel(x_ref, g_ref, o_ref):
    x = x_ref[...]
    ms = jnp.mean(x * x, axis=-1, keepdims=True)
    o_ref[...] = (x * lax.rsqrt(ms + RMS_EPS) * g_ref[...]).astype(o_ref.dtype)


def rmsnorm_cast(x, g, tm=512):
    n, d = x.shape
    tm = min(tm, n)
    return pl.pallas_call(
        _rmsnorm_kernel,
        grid=(n // tm,),
        in_specs=[pl.BlockSpec((tm, d), lambda i: (i, 0)), pl.BlockSpec((1, d), lambda i: (0, 0))],
        out_specs=pl.BlockSpec((tm, d), lambda i: (i, 0)),
        out_shape=jax.ShapeDtypeStruct((n, d), BF16),
        compiler_params=_cparams("parallel"),
        name="rmsnorm",
    )(x, g.reshape(1, d))


def _mm_kernel(a_ref, w_ref, o_ref):
    o_ref[...] = _dot(a_ref[...], w_ref[...]).astype(o_ref.dtype)


def _mm_cat_res_kernel(*refs):
    *a_refs, w_ref, r_ref, o_ref = refs
    acc = r_ref[...]
    lo = 0
    for a_ref in a_refs:
        k = a_ref.shape[1]
        acc = acc + _dot(a_ref[...], w_ref[lo:lo + k, :])
        lo += k
    o_ref[...] = acc


def _pick(n, pref):
    for t in pref:
        if n % t == 0:
            return t
    return n


def matmul(a, w_all, layer, out_dtype=F32, residual=None, tm=1024, tn=None, name="matmul"):
    pieces = list(a) if isinstance(a, (list, tuple)) else [a]
    m = pieces[0].shape[0]
    k, n = w_all.shape[1:]
    assert sum(p_.shape[1] for p_ in pieces) == k
    tm = _pick(m, (tm, 512, 256, 128))
    tn = tn or _pick(n, (1024, 768, 512, 384, 256, 128))
    grid = (m // tm, n // tn)
    a_specs = [pl.BlockSpec((tm, p_.shape[1]), lambda i, j: (i, 0)) for p_ in pieces]
    w_spec = pl.BlockSpec((None, k, tn), lambda i, j: (layer, 0, j))
    o_spec = pl.BlockSpec((tm, tn), lambda i, j: (i, j))
    if residual is None:
        assert len(pieces) == 1
        return pl.pallas_call(
            _mm_kernel, grid=grid, in_specs=a_specs + [w_spec], out_specs=o_spec,
            out_shape=jax.ShapeDtypeStruct((m, n), out_dtype),
            compiler_params=_cparams("parallel", "arbitrary"), name=name,
        )(*pieces, w_all)
    return pl.pallas_call(
        _mm_cat_res_kernel, grid=grid, in_specs=a_specs + [w_spec, o_spec], out_specs=o_spec,
        out_shape=jax.ShapeDtypeStruct((m, n), F32),
        compiler_params=_cparams("parallel", "arbitrary"), name=name,
    )(*pieces, w_all, residual)


def _rwkv_prep_kernel(x_ref, prev_ref, mu_ref, w0_ref, w2_ref, a0_ref, a2_ref, g2_ref, kk_ref, ka_ref,
                      rk_ref, r_o, lw_o, k_o, v_o, a_o, b_o, g_o, bonus_o, *, seq, width):
    t = x_ref.shape[0]
    i = pl.program_id(0)
    x = x_ref[...]
    seq_start = (i * t) % seq == 0
    prev = jnp.where(seq_start, 0.0, prev_ref[SUBLANES - 1:SUBLANES, :])
    rows = lax.broadcasted_iota(jnp.int32, x.shape, 0)
    shifted = jnp.where(rows == 0, prev, pltpu.roll(x, 1, axis=0))
    x = x + (shifted - x) * mu_ref[...]
    w = width
    r = x[:, 0:w]
    k = x[:, w:2 * w]
    v = x[:, 2 * w:3 * w]
    o = 3 * w
    w_lo = x[:, o:o + LANES]
    a_lo = x[:, o + LANES:o + 2 * LANES]
    g_lo = x[:, o + 2 * LANES:o + 4 * LANES]
    seg = _seg_matrix()

    wv = w0_ref[...] + _dot3w(jnp.tanh(w_lo), w2_ref)
    w_log = -(jnp.maximum(-wv, 0.0) + jnp.log(1.0 + jnp.exp(-jnp.abs(wv)))) - 0.5
    lw_o[...] = -jnp.exp(w_log)
    a = _sigmoid(a0_ref[...] + _dot3w(a_lo, a2_ref))
    g_o[...] = _dot3w(_sigmoid(g_lo), g2_ref).astype(g_o.dtype)

    kk = k * kk_ref[...]
    nrm = jnp.sqrt(_head_sum(kk * kk, seg))
    kk = kk / jnp.maximum(nrm, 1e-12)
    k = k * (1.0 + (a - 1.0) * ka_ref[...])
    r_o[...] = r.astype(r_o.dtype)
    k_o[...] = k.astype(k_o.dtype)
    v_o[...] = v.astype(v_o.dtype)
    a_o[...] = (-kk).astype(a_o.dtype)
    b_o[...] = (kk * a).astype(b_o.dtype)
    bonus_o[...] = (_head_sum(r * k * rk_ref[...], seg) * v).astype(bonus_o.dtype)


def rwkv_prep(proj, mu, w0, w2, a0, a2, g2, k_k, k_a, r_k, seq, tm=128):
    n, wp = proj.shape
    width = w0.shape[-1]
    tm = min(tm, seq)
    row = lambda i: (i, 0)
    fix = lambda i: (0, 0)
    fix3 = lambda i: (0, 0, 0)
    vec = pl.BlockSpec((1, width), fix)
    out = pl.BlockSpec((tm, width), row)
    per = tm // SUBLANES
    return pl.pallas_call(
        functools.partial(_rwkv_prep_kernel, seq=seq, width=width),
        grid=(n // tm,),
        in_specs=[pl.BlockSpec((tm, wp), row),
                  pl.BlockSpec((SUBLANES, wp), lambda i: (jnp.maximum(i * per - 1, 0), 0)),
                  pl.BlockSpec((1, wp), fix), vec, pl.BlockSpec(w2.shape, fix3), vec,
                  pl.BlockSpec(a2.shape, fix3), pl.BlockSpec(g2.shape, fix3), vec, vec, vec],
        out_specs=[out] * 8,
        out_shape=[jax.ShapeDtypeStruct((n, width), F32 if j == 1 else BF16) for j in range(8)],
        compiler_params=_cparams("parallel"), name="rwkv_prep",
    )(proj, proj, mu, w0, w2, a0, a2, g2, k_k, k_a, r_k)


SCAN_PASSES = dict(gram=1, gkv=1, inv=1, t=1, r=1, y=1, m=1, s=3)


def _pdot(a, b, dims, passes):
    if passes == 1:
        return _dot(a.astype(BF16), b.astype(BF16), dims)
    return _dot3(a, b, dims)


def _rwkv_scan_kernel(r_ref, lw_ref, k_ref, v_ref, a_ref, b_ref, y_ref, st_ref, *, chunk, wave):
    c_len = chunk
    hd = HEAD_DIM
    nheads = r_ref.shape[1] // hd
    ps = SCAN_PASSES

    @pl.when(pl.program_id(2) == 0)
    def _():
        st_ref[...] = jnp.zeros_like(st_ref)

    nchunks = r_ref.shape[0] // c_len
    row2 = lax.broadcasted_iota(jnp.int32, (2 * c_len, 2 * c_len), 0)
    col2 = lax.broadcasted_iota(jnp.int32, (2 * c_len, 2 * c_len), 1) % c_len
    mask2 = col2 <= jnp.where(row2 < c_len, row2 - 1, row2 - c_len)
    rowc = lax.broadcasted_iota(jnp.int32, (c_len, 2 * c_len), 0)
    colc = lax.broadcasted_iota(jnp.int32, (c_len, 2 * c_len), 1)
    right = colc >= c_len
    eye_right = (colc - c_len == rowc).astype(F32)
    zeros_h = jnp.zeros((c_len, hd), F32)
    zeros_w = jnp.zeros((c_len, 2 * c_len), F32)
    eye_k = (lax.broadcasted_iota(jnp.int32, (hd, hd), 0)
             == lax.broadcasted_iota(jnp.int32, (hd, hd), 1)).astype(F32)
    ridx = lax.broadcasted_iota(jnp.int32, (c_len, r_ref.shape[1]), 0)

    def load_units(chunks):
        units = []
        for c in chunks:
            rows = slice(c * c_len, (c + 1) * c_len)
            lw = lw_ref[rows, :]
            cum = lw
            sh = 1
            while sh < c_len:
                cum = cum + jnp.where(ridx >= sh, pltpu.roll(cum, sh, axis=0), 0.0)
                sh *= 2
            total = cum[c_len - 1:c_len, :]
            e_neg = jnp.exp(-cum)
            e_end = jnp.exp(total - cum)
            rt = r_ref[rows, :].astype(F32) * jnp.exp(cum)
            at = a_ref[rows, :].astype(F32) * jnp.exp(cum - lw)
            b = b_ref[rows, :].astype(F32)
            k = k_ref[rows, :].astype(F32)
            bt, kt, bh, kh = b * e_neg, k * e_neg, b * e_end, k * e_end
            v = v_ref[rows, :].astype(F32)
            g_end = jnp.exp(total)
            for h in range(nheads):
                sl = slice(h * hd, (h + 1) * hd)
                units.append(dict(c=c, h=h, at=at[:, sl], rt=rt[:, sl], v=v[:, sl], bt=bt[:, sl], kt=kt[:, sl],
                                  bh=bh[:, sl], kh=kh[:, sl], g=g_end[:, sl]))
        return units

    def phase1(units):
        def below(x):
            return jnp.concatenate([zeros_h, x], axis=0)

        def above(x):
            return jnp.concatenate([x, zeros_h], axis=0)

        for u in units:
            l2 = jnp.concatenate([u["at"], u["rt"]], axis=0)
            bk = jnp.concatenate([u["bt"], u["kt"]], axis=0)
            u["gm"] = jnp.where(mask2, _pdot(l2, bk, NT, ps["gram"]), 0.0)
        yield
        for u in units:
            u["gkv"] = _pdot(u["gm"], below(u["v"]), NN, ps["gkv"])
            u["w"] = jnp.where(right, eye_right, u["gm"][:c_len])
        yield
        p = 1
        while p < c_len:
            for u in units:
                w = u["w"]
                u["w"] = _pdot(w, jnp.concatenate([w, zeros_w], axis=0), NN, ps["inv"]) + jnp.where(right, w, 0.0)
            yield
            p *= 2
        for u in units:
            u["ta"] = _pdot(u["w"], below(u["at"]), NN, ps["t"])
        yield
        for u in units:
            u["u0"] = _pdot(u["w"], below(u["gkv"][:c_len]), NN, ps["t"])
        yield
        for u in units:
            u["rhat"] = u["rt"] + _pdot(u["gm"][c_len:], above(u["ta"]), NN, ps["r"])
        yield
        for u in units:
            u["y1"] = _pdot(u["gm"][c_len:], above(u["u0"]), NN, ps["r"]) + u["gkv"][c_len:]
        yield
        for u in units:
            u["m"] = eye_k * u["g"] + _pdot(u["bh"], u["ta"], TN, ps["m"])
        yield
        for u in units:
            u["nm"] = _pdot(jnp.concatenate([u["bh"], u["kh"]], axis=0),
                            jnp.concatenate([u["u0"], u["v"]], axis=0), TN, ps["m"])
        yield

    state = [st_ref[h] for h in range(nheads)]

    def phase2_steps(units):
        by_chunk = {}
        for u in units:
            by_chunk.setdefault(u["c"], []).append(u)

        def step(c):
            ys = []
            for u in by_chunk[c]:
                h = u["h"]
                ys.append(_pdot(u["rhat"], state[h], NN, ps["y"]) + u["y1"])
                state[h] = _pdot(u["m"], state[h], NN, ps["s"]) + u["nm"]
            y_ref[c * c_len:(c + 1) * c_len, :] = jnp.concatenate(ys, axis=1)

        return [functools.partial(step, c) for c in sorted(by_chunk)]

    pending = []
    for w0 in range(0, nchunks, wave):
        units = load_units(range(w0, min(w0 + wave, nchunks)))
        for _ in phase1(units):
            if pending:
                pending.pop(0)()
        for step in pending:
            step()
        pending = phase2_steps(units)
    for step in pending:
        step()
    for h in range(nheads):
        st_ref[h] = state[h]


def rwkv_scan(r, lw, k, v, a, b, batch, seq, tblk=RWKV_TBLK):
    n, width = r.shape
    tblk = min(tblk, seq)
    nt = seq // tblk
    wl = RWKV_HEADS_PER_STEP * HEAD_DIM
    assert width % wl == 0, (width, wl)
    spec = pl.BlockSpec((tblk, wl), lambda bi, hp, tb: (bi * nt + tb, hp))
    return pl.pallas_call(
        functools.partial(_rwkv_scan_kernel, chunk=RWKV_CHUNK, wave=RWKV_WAVE),
        grid=(batch, width // wl, nt),
        in_specs=[spec] * 6,
        out_specs=spec,
        out_shape=jax.ShapeDtypeStruct((n, width), F32),
        scratch_shapes=[pltpu.VMEM((RWKV_HEADS_PER_STEP, HEAD_DIM, HEAD_DIM), F32)],
        compiler_params=_cparams("parallel", "parallel", "arbitrary"), name="rwkv_scan",
    )(r, lw, k, v, a, b)


def _rwkv_post_kernel(y_ref, bonus_ref, g_ref, lg_ref, lb_ref, o_ref):
    seg = _seg_matrix()
    y = y_ref[...]
    inv_n = 1.0 / HEAD_DIM
    mean = _head_sum(y, seg) * inv_n
    yc = y - mean
    var = _head_sum(yc * yc, seg) * inv_n
    yn = yc * lax.rsqrt(var + RWKV_LN_EPS) * lg_ref[...] + lb_ref[...]
    o_ref[...] = ((yn + bonus_ref[...].astype(F32)) * g_ref[...].astype(F32)).astype(o_ref.dtype)


def rwkv_post(y, bonus, g, lnx_g, lnx_b, tm=256):
    n, width = y.shape
    tm = _pick(n, (tm, 128))
    row = pl.BlockSpec((tm, width), lambda i: (i, 0))
    vec = pl.BlockSpec((1, width), lambda i: (0, 0))
    return pl.pallas_call(
        _rwkv_post_kernel, grid=(n // tm,),
        in_specs=[row, row, row, vec, vec], out_specs=row,
        out_shape=jax.ShapeDtypeStruct((n, width), BF16),
        compiler_params=_cparams("parallel"), name="rwkv_post",
    )(y, bonus, g, lnx_g, lnx_b)


def _t5_causal_bucket(dist):
    max_exact = REL_BUCKETS // 2
    d_f = jnp.maximum(dist, 1).astype(F32)
    large = max_exact + (jnp.log(d_f / max_exact) / math.log(REL_MAX_DIST / max_exact)
                         * (REL_BUCKETS - max_exact)).astype(jnp.int32)
    large = jnp.minimum(large, REL_BUCKETS - 1)
    return jnp.where(dist < max_exact, dist, large)


def _pattern_bias(rel_bias_g, window, dil):
    span = window // dil
    qi = np.arange(ATTN_BLOCK)[:, None]
    ki = np.arange(2 * ATTN_BLOCK)[None, :]
    off = qi + ATTN_BLOCK - ki
    band = (off >= 0) & (off <= span)
    bucket = _t5_causal_bucket(jnp.asarray(np.clip(off, 0, span) * dil, jnp.int32))
    onehot = (bucket[:, :, None] == jnp.arange(REL_BUCKETS)[None, None, :]).astype(F32)
    bias = jnp.einsum("qkb,bh->hqk", onehot, rel_bias_g.astype(F32), precision=lax.Precision.HIGHEST)
    return jnp.where(jnp.asarray(band)[None], bias, NEG_INF)


def _attn_blocks(blocks, bias_ref, qg, kg, no_prev):
    hd = HEAD_DIM
    scale = hd ** -0.5
    seg = _seg_matrix()
    ones_k = jnp.ones((2 * ATTN_BLOCK, LANES), BF16)
    head_of_lane = lax.broadcasted_iota(jnp.int32, (ATTN_BLOCK, LANES), 1) // hd

    def norm(x, g):
        ms = _dot((x * x).astype(BF16), seg) * (1.0 / hd)
        return x * lax.rsqrt(ms + RMS_EPS) * g

    pairs = []
    for bi, (q, kp, kc, vp, vc) in enumerate(blocks):
        for c in range(q.shape[1] // LANES):
            sl = slice(c * LANES, (c + 1) * LANES)
            pairs.append(dict(b=bi, c=c, q=q[:, sl], k=jnp.concatenate([kp[:, sl], kc[:, sl]], axis=0),
                              v=jnp.concatenate([vp[:, sl], vc[:, sl]], axis=0)))
    for pr in pairs:
        pr["qn"] = norm(pr["q"], qg)
        pr["kn"] = norm(pr["k"], kg).astype(BF16)
        pr["vb"] = pr["v"].astype(BF16)
    units = [dict(pr=pr, h=h) for pr in pairs for h in range(LANES // hd)]
    for u in units:
        pr = u["pr"]
        qm = jnp.where(head_of_lane == u["h"], pr["qn"], 0.0).astype(BF16)
        s = _dot(qm, pr["kn"], NT) * scale + bias_ref[pr["c"] * (LANES // hd) + u["h"]]
        u["s"] = jnp.where(no_prev, NEG_INF, s)
    for u in units:
        u["m"] = jnp.max(u["s"], axis=-1, keepdims=True)
    for u in units:
        u["p"] = jnp.exp(u["s"] - u["m"]).astype(BF16)
    for u in units:
        u["den"] = _dot(u["p"], ones_k)
    for u in units:
        u["o"] = _dot(u["p"], u["pr"]["vb"]) / u["den"]
        u["lse"] = u["m"] + jnp.log(u["den"])
    res = []
    for bi in range(len(blocks)):
        outs, lses = [], []
        for pr in (p_ for p_ in pairs if p_["b"] == bi):
            u0, u1 = [u for u in units if u["pr"] is pr]
            outs.append(jnp.where(head_of_lane == 0, u0["o"], u1["o"]))
            lses.append(jnp.where(head_of_lane == 0, u0["lse"], u1["lse"]))
        res.append((outs[0] if len(outs) == 1 else jnp.concatenate(outs, axis=1),
                    lses[0] if len(lses) == 1 else jnp.concatenate(lses, axis=1)))
    return res


def _no_prev_mask():
    ki = lax.broadcasted_iota(jnp.int32, (ATTN_BLOCK, 2 * ATTN_BLOCK), 1)
    return jnp.logical_and(pl.program_id(1) == 0, ki < ATTN_BLOCK)


def _attn_kernel(q_ref, kp_ref, kc_ref, vp_ref, vc_ref, bias_ref, qg_ref, kg_ref, o_ref, lse_ref):
    (o, lse), = _attn_blocks([(q_ref[...], kp_ref[...], kc_ref[...], vp_ref[...], vc_ref[...])], bias_ref,
                             qg_ref[...], kg_ref[...], _no_prev_mask())
    o_ref[...] = o
    lse_ref[...] = lse


def _attn_dilated_kernel(q_ref, kp_ref, kc_ref, vp_ref, vc_ref, bias_ref, qg_ref, kg_ref, o_ref, lse_ref, *, dil):
    no_prev = _no_prev_mask()
    group = min(dil, ATTN_GROUP)

    def body(rg, carry):
        rows = [pl.ds(rg * group + j, ATTN_BLOCK, stride=dil) for j in range(group)]
        blocks = [(q_ref[rw, :], kp_ref[rw, :], kc_ref[rw, :], vp_ref[rw, :], vc_ref[rw, :]) for rw in rows]
        for rw, (o, lse) in zip(rows, _attn_blocks(blocks, bias_ref, qg_ref[...], kg_ref[...], no_prev)):
            o_ref[rw, :] = o
            lse_ref[rw, :] = lse
        return carry

    lax.fori_loop(0, dil // group, body, 0)


def attn_dilated(proj, bias, q_g, k_g, gi, dil, batch, seq, width):
    n = proj.shape[0]
    npat = len(DILATED_PATTERNS)
    pw = width // npat
    rows = ATTN_BLOCK * dil
    nb = seq // rows
    sec = width // LANES
    hp_per = pw // LANES
    heads_pp = LANES // HEAD_DIM

    def spec(section, prev):
        def imap(b, nblk, hp):
            blk = jnp.maximum(nblk - 1, 0) if prev else nblk
            return (b * nb + blk, section * sec + gi * hp_per + hp)
        return pl.BlockSpec((rows, LANES), imap)

    bias_spec = pl.BlockSpec((heads_pp,) + bias.shape[1:], lambda b, nblk, hp: (hp, 0, 0))
    vec = pl.BlockSpec((1, LANES), lambda b, nblk, hp: (0, 0))
    q_g, k_g = (jnp.tile(g, (1, heads_pp)) for g in (q_g, k_g))
    ospec = pl.BlockSpec((rows, LANES), lambda b, nblk, hp: (b * nb + nblk, hp))
    return pl.pallas_call(
        functools.partial(_attn_dilated_kernel, dil=dil), grid=(batch, nb, hp_per),
        in_specs=[spec(0, False), spec(1, True), spec(1, False), spec(2, True), spec(2, False), bias_spec, vec, vec],
        out_specs=[ospec, ospec],
        out_shape=[jax.ShapeDtypeStruct((n, pw), F32)] * 2,
        compiler_params=_cparams("parallel", "parallel", "parallel"), name=f"attn_d{dil}",
    )(proj, proj, proj, proj, proj, bias, q_g, k_g)


def attn_pattern(proj, bias, q_g, k_g, gi, dil, batch, seq, width):
    n = proj.shape[0]
    npat = len(DILATED_PATTERNS)
    pw = width // npat
    nsub = seq // dil
    nb = nsub // ATTN_BLOCK
    sec = width // pw
    rowlen = 3 * sec
    x = proj.reshape(batch * nsub, dil * 3 * width)

    def spec(section, prev):
        def imap(b, nblk, r):
            blk = jnp.maximum(nblk - 1, 0) if prev else nblk
            return (b * nb + blk, r * rowlen + section * sec + gi)
        return pl.BlockSpec((ATTN_BLOCK, pw), imap)

    fix3 = pl.BlockSpec(bias.shape, lambda b, nblk, r: (0, 0, 0))
    vec = pl.BlockSpec((1, LANES), lambda b, nblk, r: (0, 0))
    q_g, k_g = (jnp.tile(g, (1, LANES // HEAD_DIM)) for g in (q_g, k_g))
    ospec = pl.BlockSpec((ATTN_BLOCK, pw), lambda b, nblk, r: (b * nb + nblk, r))
    o, lse = pl.pallas_call(
        _attn_kernel, grid=(batch, nb, dil),
        in_specs=[spec(0, False), spec(1, True), spec(1, False), spec(2, True), spec(2, False), fix3, vec, vec],
        out_specs=[ospec, ospec],
        out_shape=[jax.ShapeDtypeStruct((batch * nsub, dil * pw), F32)] * 2,
        compiler_params=_cparams("parallel", "parallel", "parallel"), name=f"attn_d{dil}",
    )(x, x, x, x, x, bias, q_g, k_g)
    return o.reshape(n, pw), lse.reshape(n, pw)


def _attn_mix_kernel(o0, o1, o2, l0, l1, l2, out_ref):
    a, b, c = l0[...], l1[...], l2[...]
    m = jnp.maximum(jnp.maximum(a, b), c)
    ea, eb, ec = jnp.exp(a - m), jnp.exp(b - m), jnp.exp(c - m)
    inv = 1.0 / (ea + eb + ec)
    out_ref[...] = jnp.concatenate([o0[...] * (ea * inv), o1[...] * (eb * inv), o2[...] * (ec * inv)],
                                   axis=1).astype(out_ref.dtype)


def attn_mix(outs, lses, tm=512):
    n, pw = outs[0].shape
    tm = _pick(n, (tm, 256, 128))
    row = pl.BlockSpec((tm, pw), lambda i: (i, 0))
    return pl.pallas_call(
        _attn_mix_kernel, grid=(n // tm,), in_specs=[row] * 6,
        out_specs=pl.BlockSpec((tm, 3 * pw), lambda i: (i, 0)),
        out_shape=jax.ShapeDtypeStruct((n, 3 * pw), BF16),
        compiler_params=_cparams("parallel"), name="attn_mix",
    )(*outs, *lses)


def _conv_kernel(x_ref, halo_ref, w_ref, b_ref, lg_ref, lb_ref, o_ref, ext_ref, shift_ref, *, seq, width):
    t = x_ref.shape[0]
    i = pl.program_id(0)
    seq_start = (i * t) % seq == 0

    def glu(x):
        return x[:, :width] * _sigmoid(x[:, width:])

    ext_ref[0:CONV_HALO, :] = jnp.where(seq_start, 0.0, glu(halo_ref[...]))
    ext_ref[CONV_HALO:, :] = glu(x_ref[...])
    base = CONV_HALO - (CONV_TAPS - 1)
    acc = jnp.broadcast_to(b_ref[...], (t, width))
    for rho in range(SUBLANES):
        offs = [base + j for j in range(CONV_TAPS) if (base + j) % SUBLANES == rho]
        if not offs:
            continue
        lo = offs[0]
        span = offs[-1] - lo + t
        shift_ref[0:span, :] = ext_ref[lo:lo + span, :]
        for o in offs:
            acc = acc + w_ref[o - base:o - base + 1, :] * shift_ref[o - lo:o - lo + t, :]
    mu = jnp.mean(acc, axis=-1, keepdims=True)
    d = acc - mu
    var = jnp.mean(d * d, axis=-1, keepdims=True)
    y = d * lax.rsqrt(var + LN_EPS) * lg_ref[...] + lb_ref[...]
    o_ref[...] = (y * _sigmoid(y)).astype(o_ref.dtype)


def conv_module(proj, conv_w, conv_b, ln_g, ln_b, seq, tm=256):
    n, w2 = proj.shape
    width = w2 // 2
    tm = min(tm, seq)
    per = tm // CONV_HALO
    fix = lambda i: (0, 0)
    vec = pl.BlockSpec((1, width), fix)
    return pl.pallas_call(
        functools.partial(_conv_kernel, seq=seq, width=width), grid=(n // tm,),
        in_specs=[pl.BlockSpec((tm, w2), lambda i: (i, 0)),
                  pl.BlockSpec((CONV_HALO, w2), lambda i: (jnp.maximum(i * per - 1, 0), 0)),
                  pl.BlockSpec((CONV_TAPS, width), fix), vec, vec, vec],
        out_specs=pl.BlockSpec((tm, width), lambda i: (i, 0)),
        out_shape=jax.ShapeDtypeStruct((n, width), BF16),
        scratch_shapes=[pltpu.VMEM((tm + CONV_HALO, width), F32), pltpu.VMEM((tm + CONV_HALO, width), F32)],
        compiler_params=_cparams("parallel"), name="conv_module",
    )(proj, proj, conv_w, conv_b, ln_g, ln_b)


ROUTER_LANES = LANES


def _pack_bf16_pairs(x):
    w = x.shape[1] // 2
    hi = lax.bitcast_convert_type(x[:, :w].astype(BF16).astype(F32), jnp.uint32)
    lo = lax.bitcast_convert_type(x[:, w:].astype(BF16).astype(F32), jnp.uint32)
    return hi | (lo >> 16)


def _unpack_bf16_pairs(words):
    hi = lax.bitcast_convert_type(words & jnp.uint32(0xFFFF0000), F32)
    lo = lax.bitcast_convert_type(words << 16, F32)
    return hi, lo


def _store_token_major(ref, words):
    t = words.shape[0]
    chunks = words.shape[1] // LANES
    for c in range(chunks):
        ref[pl.ds(c, t, stride=chunks), :] = words[:, c * LANES:(c + 1) * LANES]


def _load_token_major(ref, t, chunks):
    return jnp.concatenate([ref[pl.ds(c, t, stride=chunks), :] for c in range(chunks)], axis=1)


def _route_kernel(x_ref, g_ref, rw_ref, bias_ref, info_ref, cnt_ref, xp_ref, run_ref):
    t = x_ref.shape[0]

    @pl.when(pl.program_id(0) == 0)
    def _():
        run_ref[...] = jnp.zeros_like(run_ref)

    x = x_ref[...]
    ms = jnp.mean(x * x, axis=-1, keepdims=True)
    xn = x * lax.rsqrt(ms + RMS_EPS) * g_ref[...]
    _store_token_major(xp_ref, _pack_bf16_pairs(xn))
    z = _dot3w(xn, rw_ref) + bias_ref[...]
    lane = lax.broadcasted_iota(jnp.int32, z.shape, 1).astype(F32)
    ninf = -jnp.inf
    big = float(ROUTER_LANES)
    zc = jnp.where(lane < N_GROUPS, z, ninf)
    mc = jnp.max(zc, axis=-1, keepdims=True)
    pg = 1.0 / jnp.sum(jnp.exp(zc - mc), axis=-1, keepdims=True)
    g_idx = jnp.min(jnp.where(zc == mc, lane, big), axis=-1, keepdims=True)
    lo = N_GROUPS + g_idx * EXPERTS_PER_GROUP
    zf = jnp.where(jnp.logical_and(lane >= lo, lane < lo + EXPERTS_PER_GROUP), z, ninf)
    m1 = jnp.max(zf, axis=-1, keepdims=True)
    i1 = jnp.min(jnp.where(zf == m1, lane, big), axis=-1, keepdims=True)
    zf2 = jnp.where(lane == i1, ninf, zf)
    m2 = jnp.max(zf2, axis=-1, keepdims=True)
    i2 = jnp.min(jnp.where(zf2 == m2, lane, big), axis=-1, keepdims=True)
    e2 = jnp.exp(m2 - m1)
    w1 = pg / (1.0 + e2)
    w2 = pg * e2 / (1.0 + e2)
    onehot = jnp.where(jnp.logical_or(lane == i1, lane == i2), 1.0, 0.0)
    tri = (lax.broadcasted_iota(jnp.int32, (t, t), 0) > lax.broadcasted_iota(jnp.int32, (t, t), 1)).astype(BF16)
    before = _dot(tri, onehot.astype(BF16)) + run_ref[...]
    r1 = jnp.sum(jnp.where(lane == i1, before, 0.0), axis=-1, keepdims=True)
    r2 = jnp.sum(jnp.where(lane == i2, before, 0.0), axis=-1, keepdims=True)
    run_ref[...] += jnp.sum(onehot, axis=0, keepdims=True)
    cnt_ref[...] = run_ref[...]
    cols = (i1 - N_GROUPS, i2 - N_GROUPS, w1, w2, r1, r2)
    info = jnp.zeros_like(z)
    for j, c in enumerate(cols):
        info = jnp.where(lane == float(j), c, info)
    info_ref[...] = info


def route(x, g, router_w, bias, tm=256):
    n, d = x.shape
    w = router_w.shape[-1]
    chunks = d // 2 // LANES
    tm = _pick(n, (tm, 128))
    fix = lambda i: (0, 0)
    row = pl.BlockSpec((tm, w), lambda i: (i, 0))
    one = pl.BlockSpec((1, w), fix)
    return pl.pallas_call(
        _route_kernel, grid=(n // tm,),
        in_specs=[pl.BlockSpec((tm, d), lambda i: (i, 0)), pl.BlockSpec((1, d), fix),
                  pl.BlockSpec((2, d, w), lambda i: (0, 0, 0)), one],
        out_specs=[row, one, pl.BlockSpec((tm * chunks, LANES), lambda i: (i, 0))],
        out_shape=[jax.ShapeDtypeStruct((n, w), F32), jax.ShapeDtypeStruct((1, w), F32),
                   jax.ShapeDtypeStruct((n * chunks, LANES), jnp.uint32)],
        scratch_shapes=[pltpu.VMEM((1, w), F32)],
        compiler_params=_cparams("arbitrary"), name="route",
    )(x, g.reshape(1, d), router_w, bias)


MOE_TILE = 256


def _moe_dest_kernel(info_ref, off_ref, dd_ref):
    info = info_ref[...]
    lane = lax.broadcasted_iota(jnp.int32, info.shape, 1).astype(F32)
    off = off_ref[...]

    def dest(e, r):
        return jnp.sum(jnp.where(lane == e + N_GROUPS, off, 0.0), axis=-1, keepdims=True) + r

    d1 = dest(info[:, 0:1], info[:, 4:5])
    d2 = dest(info[:, 1:2], info[:, 5:6])
    dd_ref[...] = jnp.where(lane == 0.0, d1, jnp.where(lane == 1.0, d2, 0.0)).astype(jnp.int32)


def moe_plan(info, counts, n_tiles, tm=512):
    n, w = info.shape
    cnt = counts[0, N_GROUPS:N_GROUPS + N_EXPERTS].astype(jnp.int32)
    padded = (cnt + MOE_TILE - 1) // MOE_TILE * MOE_TILE
    ends = jnp.cumsum(padded)
    off_row = jnp.pad((ends - padded).astype(F32), (N_GROUPS, w - N_GROUPS - N_EXPERTS)).reshape(1, w)
    tm = _pick(n, (tm, 256, 128))
    row = pl.BlockSpec((tm, w), lambda i: (i, 0))
    dd = pl.pallas_call(
        _moe_dest_kernel, grid=(n // tm,),
        in_specs=[row, pl.BlockSpec((1, w), lambda i: (0, 0))], out_specs=row,
        out_shape=jax.ShapeDtypeStruct((n, w), jnp.int32),
        compiler_params=_cparams("parallel"), name="moe_dest",
    )(info, off_row)
    tile_start = jnp.arange(n_tiles, dtype=jnp.int32) * MOE_TILE
    tile_expert = jnp.minimum(jnp.sum((tile_start[:, None] >= ends[None, :]).astype(jnp.int32), axis=1),
                              N_EXPERTS - 1)
    n_used = (ends[-1:] // MOE_TILE).astype(jnp.int32)
    ar = jnp.arange(N_EXPERTS, dtype=jnp.int32)
    nonempty = cnt > 0
    later = jnp.logical_and(ar[None, :] > ar[:, None], nonempty[None, :])
    nxt_of = jnp.min(jnp.where(later, ar[None, :], N_EXPERTS), axis=1)
    nxt_of = jnp.where(nxt_of == N_EXPERTS, ar, nxt_of)
    par_of = (jnp.cumsum(nonempty.astype(jnp.int32)) - 1) % 2
    pick = tile_expert[:, None] == ar[None, :]
    nxt_tile = jnp.sum(jnp.where(pick, nxt_of[None, :], 0), axis=1).astype(jnp.int32)
    par_tile = jnp.sum(jnp.where(pick, par_of[None, :], 0), axis=1).astype(jnp.int32)
    idx = jnp.arange(n_tiles, dtype=jnp.int32)
    nxt_te = jnp.concatenate([tile_expert[1:], tile_expert[-1:]])
    last_tile = jnp.logical_or(idx == n_used[0] - 1, nxt_te != tile_expert).astype(jnp.int32)
    return dd[:, 0], dd[:, 1], (tile_expert, n_used, nxt_tile, last_tile, par_tile)


DMA_ISSUE_UNROLL = 8


def _moe_ffn_kernel(d1_ref, d2_ref, texp_ref, nused_ref, nxt_ref, last_ref, par_ref, xp_hbm, wi_ref, wo_ref,
                    ys_ref, src_ref, xbuf, sem, wi_bf, wo_bf, *, n_tok, ff):
    step = pl.program_id(0)
    i = step - 1
    n_used = nused_ref[0]
    n_rows = src_ref.shape[0]
    half = wi_ref.shape[1] // 2
    chunks = half // LANES

    def gather(tile, slot):
        def body(j, carry):
            tok = src_ref[tile * MOE_TILE + j]
            pltpu.make_async_copy(xp_hbm.at[pl.ds(tok * chunks, chunks), :],
                                  xbuf.at[slot, pl.ds(j * chunks, chunks), :], sem.at[slot]).start()
            return carry
        lax.fori_loop(0, MOE_TILE, body, 0, unroll=DMA_ISSUE_UNROLL)

    def cache_weights(slot_):
        wi_bf[slot_] = wi_ref[0].astype(BF16)
        wo_bf[slot_] = wo_ref[0].astype(BF16)

    @pl.when(step == 0)
    def _():
        def clear(j, carry):
            src_ref[j] = 0
            return carry
        lax.fori_loop(0, n_rows, clear, 0, unroll=DMA_ISSUE_UNROLL)

        def fill(tok, carry):
            src_ref[d1_ref[tok]] = tok
            src_ref[d2_ref[tok]] = tok
            return carry
        lax.fori_loop(0, n_tok, fill, 0, unroll=DMA_ISSUE_UNROLL)
        gather(0, 0)
        cache_weights(0)

    @pl.when(jnp.logical_and(step > 0, i < n_used))
    def _():
        slot = i % 2

        @pl.when(i + 1 < n_used)
        def _():
            gather(i + 1, 1 - slot)

        pltpu.make_async_copy(xp_hbm.at[pl.ds(0, MOE_TILE * chunks), :], xbuf.at[slot], sem.at[slot]).wait()
        par = par_ref[i]
        x_a, x_b = _unpack_bf16_pairs(_load_token_major(xbuf.at[slot], MOE_TILE, chunks))
        gu = (_dot(x_a.astype(BF16), wi_bf[par, :half, :]) + _dot(x_b.astype(BF16), wi_bf[par, half:, :]))
        gt, up = gu[:, :ff], gu[:, ff:]
        hid = gt * _sigmoid(gt) * up
        _store_token_major(ys_ref, _pack_bf16_pairs(_dot(hid.astype(BF16), wo_bf[par])))

        @pl.when(jnp.logical_and(last_ref[i] == 1, i + 1 < n_used))
        def _():
            cache_weights(1 - par)

    @pl.when(jnp.logical_and(step > 0, i >= n_used))
    def _():
        ys_ref[...] = jnp.zeros_like(ys_ref)


def moe_ffn(xp, w_in_all, w_out_all, layer, d1, d2, tiles):
    w_in, w_out = w_in_all, w_out_all
    tile_expert = tiles[0]
    _, ne, d, ff2 = w_in.shape
    chunks = d // 2 // LANES
    n = xp.shape[0] // chunks
    n_tiles = tile_expert.shape[0]
    n_rows = n_tiles * MOE_TILE

    def w_map(s, d1, d2, te, nu, nxt, last, par):
        tile = jnp.clip(s - 1, 0, nu[0] - 1)
        return (layer, jnp.where(s == 0, te[0], nxt[tile]), 0, 0)

    grid_spec = pltpu.PrefetchScalarGridSpec(
        num_scalar_prefetch=7, grid=(n_tiles + 1,),
        in_specs=[pl.BlockSpec(memory_space=pl.ANY),
                  pl.BlockSpec((None, 1, d, ff2), w_map),
                  pl.BlockSpec((None, 1, ff2 // 2, d), w_map)],
        out_specs=pl.BlockSpec((MOE_TILE * chunks, LANES), lambda s, *_: (jnp.maximum(s - 1, 0), 0)),
        scratch_shapes=[pltpu.SMEM((n_rows,), jnp.int32), pltpu.VMEM((2, MOE_TILE * chunks, LANES), jnp.uint32),
                        pltpu.SemaphoreType.DMA((2,)), pltpu.VMEM((2, d, ff2), BF16),
                        pltpu.VMEM((2, ff2 // 2, d), BF16)])
    return pl.pallas_call(
        functools.partial(_moe_ffn_kernel, n_tok=n, ff=ff2 // 2), grid_spec=grid_spec,
        out_shape=jax.ShapeDtypeStruct((n_rows * chunks, LANES), jnp.uint32),
        compiler_params=_cparams("arbitrary"), name="moe_ffn",
    )(d1, d2, *tiles, xp, w_in, w_out)


def _moe_combine_kernel(d1_ref, d2_ref, ys_hbm, h_ref, info_ref, g_ref, o_ref, xn_ref, ybuf, sem):
    t = h_ref.shape[0]
    half = h_ref.shape[1] // 2
    chunks = half // LANES
    i = pl.program_id(0)
    slot = i % 2

    def gather(tile, slot_):
        def issue(j, carry):
            dst = pl.ds(j * chunks, chunks)
            tok = tile * t + j
            pltpu.make_async_copy(ys_hbm.at[pl.ds(d1_ref[tok] * chunks, chunks), :], ybuf.at[slot_, 0, dst, :],
                                  sem.at[slot_, 0]).start()
            pltpu.make_async_copy(ys_hbm.at[pl.ds(d2_ref[tok] * chunks, chunks), :], ybuf.at[slot_, 1, dst, :],
                                  sem.at[slot_, 1]).start()
            return carry
        lax.fori_loop(0, t, issue, 0, unroll=DMA_ISSUE_UNROLL)

    @pl.when(i == 0)
    def _():
        gather(0, 0)

    @pl.when(i + 1 < pl.num_programs(0))
    def _():
        gather(i + 1, 1 - slot)

    for s in range(2):
        pltpu.make_async_copy(ys_hbm.at[pl.ds(0, t * chunks), :], ybuf.at[slot, s], sem.at[slot, s]).wait()
    info = info_ref[...]
    w1, w2 = info[:, 2:3], info[:, 3:4]
    y1a, y1b = _unpack_bf16_pairs(_load_token_major(ybuf.at[slot, 0], t, chunks))
    y2a, y2b = _unpack_bf16_pairs(_load_token_major(ybuf.at[slot, 1], t, chunks))
    h_a = h_ref[:, :half] + w1 * y1a + w2 * y2a
    h_b = h_ref[:, half:] + w1 * y1b + w2 * y2b
    o_ref[:, :half] = h_a
    o_ref[:, half:] = h_b
    ms = (jnp.sum(h_a * h_a, axis=-1, keepdims=True) + jnp.sum(h_b * h_b, axis=-1, keepdims=True)) / (2 * half)
    scale = lax.rsqrt(ms + RMS_EPS)
    xn_ref[:, :half] = (h_a * scale * g_ref[:, :half]).astype(xn_ref.dtype)
    xn_ref[:, half:] = (h_b * scale * g_ref[:, half:]).astype(xn_ref.dtype)


def moe_combine(ys, h, info, g_next, d1, d2, tm=256):
    n, d = h.shape
    tm = _pick(n, (tm, 128))
    row = lambda i, d1, d2: (i, 0)
    tile = pl.BlockSpec((tm, d), row)
    grid_spec = pltpu.PrefetchScalarGridSpec(
        num_scalar_prefetch=2, grid=(n // tm,),
        in_specs=[pl.BlockSpec(memory_space=pl.ANY), tile, pl.BlockSpec((tm, info.shape[1]), row),
                  pl.BlockSpec((1, d), lambda i, d1, d2: (0, 0))],
        out_specs=[tile, tile],
        scratch_shapes=[pltpu.VMEM((2, 2, tm * (d // 2 // LANES), LANES), jnp.uint32),
                        pltpu.SemaphoreType.DMA((2, 2))])
    return pl.pallas_call(
        _moe_combine_kernel, grid_spec=grid_spec,
        out_shape=[jax.ShapeDtypeStruct((n, d), F32), jax.ShapeDtypeStruct((n, d), BF16)],
        compiler_params=_cparams("arbitrary"), name="moe_combine",
    )(d1, d2, ys, h, info, g_next.reshape(1, d))


def _ple_kernel(a_ref, w_ref, p_ref, pw_ref, h_ref, o_ref):
    gate = _sigmoid(_dot(a_ref[...], w_ref[...]))
    o_ref[...] = h_ref[...] + _dot(p_ref[...], pw_ref[...]) * gate


def ple(xn, gate_w_all, p_all, proj_w_all, h, layer, tm=1024, tn=512):
    n, d = xn.shape
    pd = p_all.shape[2]
    tm = _pick(n, (tm, 512, 256, 128))
    tn = _pick(d, (tn, 256, 128))
    tile = pl.BlockSpec((tm, tn), lambda i, j: (i, j))
    gate_w, p, proj_w = gate_w_all, p_all, proj_w_all
    return pl.pallas_call(
        _ple_kernel, grid=(n // tm, d // tn),
        in_specs=[pl.BlockSpec((tm, d), lambda i, j: (i, 0)), pl.BlockSpec((None, d, tn), lambda i, j: (layer, 0, j)),
                  pl.BlockSpec((None, tm, pd), lambda i, j: (layer, i, 0)),
                  pl.BlockSpec((None, pd, tn), lambda i, j: (layer, 0, j)), tile],
        out_specs=tile,
        out_shape=jax.ShapeDtypeStruct((n, d), F32),
        compiler_params=_cparams("parallel", "arbitrary"), name="ple",
    )(xn, gate_w, p, proj_w, h)


def _pad_cols(x, total):
    return jnp.pad(x, ((0, 0),) * (x.ndim - 1) + ((0, total - x.shape[-1]),))


def _pad_rows(x, total):
    return jnp.pad(x, ((0, 0),) * (x.ndim - 2) + ((0, total - x.shape[-2]), (0, 0)))


def _pack_rwkv_cols(x, width, lora):
    dl, al, gl = lora
    o = 3 * width
    return jnp.concatenate([
        x[..., :o], _pad_cols(x[..., o:o + dl], LANES), _pad_cols(x[..., o + dl:o + dl + al], LANES),
        _pad_cols(x[..., o + dl + al:o + dl + al + gl], 2 * LANES)], axis=-1)


def _transpose_cast_kernel(off_ref, valid_ref, w_ref, o_ref):
    j = pl.program_id(1)
    x = w_ref[0]
    rows = lax.broadcasted_iota(jnp.int32, x.shape, 0)
    o_ref[0] = jnp.where(rows < valid_ref[j], x, 0.0).T.astype(o_ref.dtype)


def repack_w_in(w_in, rw, lora, aw):
    depth, d, total = w_in.shape
    dl, al, gl = lora
    w_t = jnp.transpose(w_in, (0, 2, 1))
    o = 3 * rw
    base = o + dl + al + gl
    rwkv_blocks = [(c, LANES) for c in range(0, o, LANES)] + [(o, dl), (o + dl, al)]
    rwkv_blocks += [(o + dl + al + c, max(0, min(LANES, gl - c))) for c in range(0, 2 * LANES, LANES)]
    attn_blocks = [(base + c, LANES) for c in range(0, 3 * aw, LANES)]
    conv_blocks = [(base + 3 * aw + c, LANES) for c in range(0, total - base - 3 * aw, LANES)]
    outs = []
    for name, blocks in (("rwkv", rwkv_blocks), ("attn", attn_blocks), ("conv", conv_blocks)):
        assert all(off % SUBLANES == 0 and off + LANES <= total and 0 <= v <= LANES for off, v in blocks), blocks
        offs = jnp.asarray([b_[0] // SUBLANES for b_ in blocks], jnp.int32)
        valid = jnp.asarray([b_[1] for b_ in blocks], jnp.int32)
        grid_spec = pltpu.PrefetchScalarGridSpec(
            num_scalar_prefetch=2, grid=(depth, len(blocks)),
            in_specs=[pl.BlockSpec((pl.Element(1), pl.Element(LANES), pl.Element(d)),
                                   lambda l, j, off, val: (l, pl.multiple_of(off[j] * SUBLANES, SUBLANES), 0))],
            out_specs=pl.BlockSpec((1, d, LANES), lambda l, j, off, val: (l, 0, j)))
        outs.append(pl.pallas_call(
            _transpose_cast_kernel, grid_spec=grid_spec,
            out_shape=jax.ShapeDtypeStruct((depth, d, len(blocks) * LANES), BF16),
            compiler_params=_cparams("parallel", "parallel"), name=f"repack_w_{name}",
        )(offs, valid, w_t))
    return outs


def kernel(x, p, norm_mix_g, w_in, rwkv_mu, rwkv_w0, rwkv_w2, rwkv_a0, rwkv_a2, rwkv_g2, rwkv_k_k, rwkv_k_a, rwkv_r_k, rwkv_lnx_g, rwkv_lnx_b, q_norm_g, k_norm_g, rel_bias, conv_w, conv_b, conv_ln_g, conv_ln_b, w_out, norm_ffn_g, router_c_w, router_c_b, router_f_w, router_f_b, expert_w_in, expert_w_out, ple_norm_g, ple_gate_w, ple_proj):
    batch, seq, d = x.shape
    depth = w_in.shape[0]
    n = batch * seq
    rw = rwkv_w0.shape[-1]
    lora = (rwkv_w2.shape[1], rwkv_a2.shape[1], rwkv_g2.shape[1])
    assert max(lora[0], lora[1]) <= LANES and lora[2] <= 2 * LANES
    rwkv_proj = 3 * rw + sum(lora)
    cw = conv_w.shape[-1]
    aw = d - rw - cw
    npat = len(DILATED_PATTERNS)
    hpp = aw // HEAD_DIM // npat
    row = lambda v: v.reshape(1, -1)

    biases = [_pattern_bias(rel_bias[:, gi * hpp:(gi + 1) * hpp], window, dil)
              for gi, (window, dil) in enumerate(DILATED_PATTERNS)]

    w_rwkv_all, w_attn_all, w_conv_all = repack_w_in(w_in, rw, lora, aw)
    w_out_all = w_out.astype(BF16)
    ple_gate_all = ple_gate_w.astype(BF16)
    ple_proj_all = ple_proj.astype(BF16)
    p_all = p.reshape(depth, n, -1).astype(BF16)
    mu_all = _pack_rwkv_cols(rwkv_mu[:, None, :], rw, lora)
    w2_all = _split_weight(_pad_rows(rwkv_w2, LANES))
    a2_all = _split_weight(_pad_rows(rwkv_a2, LANES))
    g2_all = _split_weight(_pad_rows(rwkv_g2, 2 * LANES))
    router_w_all = _split_weight(_pad_cols(jnp.concatenate([router_c_w, router_f_w], axis=-1), ROUTER_LANES))
    router_b_all = _pad_cols(jnp.concatenate([router_c_b, router_f_b], axis=-1)[:, None, :], ROUTER_LANES)

    h = x.reshape(n, d)
    for i in range(depth):
        xn = rmsnorm_cast(h, norm_mix_g[i])
        p_rwkv = matmul(xn, w_rwkv_all, i, name="proj_rwkv")
        p_attn = matmul(xn, w_attn_all, i, name="proj_attn")
        p_conv = matmul(xn, w_conv_all, i, name="proj_conv")

        r, lw, k, v, a, b, g, bonus = rwkv_prep(
            p_rwkv, mu_all[i], row(rwkv_w0[i]), w2_all[i], row(rwkv_a0[i]), a2_all[i], g2_all[i],
            row(rwkv_k_k[i]), row(rwkv_k_a[i]), row(rwkv_r_k[i]), seq)
        y = rwkv_scan(r, lw, k, v, a, b, batch, seq)
        y_rwkv = rwkv_post(y, bonus, g, row(rwkv_lnx_g[i]), row(rwkv_lnx_b[i]))

        outs, lses = [], []
        for gi, (window, dil) in enumerate(DILATED_PATTERNS):
            attn = attn_pattern if dil == 1 else attn_dilated
            o, lse = attn(p_attn, biases[gi], row(q_norm_g[i]), row(k_norm_g[i]), gi, dil, batch, seq, aw)
            outs.append(o)
            lses.append(lse)
        y_attn = attn_mix(outs, lses)

        y_conv = conv_module(p_conv, conv_w[i], row(conv_b[i]), row(conv_ln_g[i]), row(conv_ln_b[i]), seq)

        h = matmul([y_rwkv, y_attn, y_conv], w_out_all, i, residual=h, name="out_proj")

        info, counts, xp = route(h, norm_ffn_g[i], router_w_all[i], router_b_all[i])
        n_tiles = (2 * n) // MOE_TILE + N_EXPERTS
        d1, d2, tiles = moe_plan(info, counts, n_tiles)
        ys = moe_ffn(xp, expert_w_in, expert_w_out, i, d1, d2, tiles)
        h, xn = moe_combine(ys, h, info, ple_norm_g[i], d1, d2)

        h = ple(xn, ple_gate_all, p_all, ple_proj_all, h, i)
    return h.reshape(batch, seq, d)
```
